```python
import math
import jax, jax.numpy as jnp
from jax import lax
import numpy as np

D_MODEL = 1024
BATCH = 8
SEQ = 2048
DEPTH = 2

CHUNK = 64
N_A = DEPTH // 2
N_B = DEPTH - N_A
GMLP_BLOCK = 128
GATE_DIM = 2 * D_MODEL
A_GROUPS = 8
A_GROUP_DIM = GATE_DIM // A_GROUPS
B_HEADS = 8
QK_NOPE = 128
QK_ROPE = 64
V_HEAD = 128
Q_LORA = 384
KV_LORA = 256
ROPE_THETA = 10000.0
Q_BLOCK = 128
D_FF = 4 * D_MODEL
EPS = 1e-6

kernel_name = "yoco_gmlp_mla_sqrelu_trunk"


def rmsnorm(x, g):
    xf = x.astype(jnp.float32)
    y = xf * lax.rsqrt(jnp.mean(xf * xf, axis=-1, keepdims=True) + EPS)
    return (y * g.astype(jnp.float32)).astype(x.dtype)


def layernorm(x, g, b):
    xf = x.astype(jnp.float32)
    mu = jnp.mean(xf, axis=-1, keepdims=True)
    var = jnp.mean(jnp.square(xf - mu), axis=-1, keepdims=True)
    y = (xf - mu) * lax.rsqrt(var + EPS)
    return (y * g.astype(jnp.float32) + b.astype(jnp.float32)).astype(x.dtype)


def rope_angles(positions):
    inv_freq = ROPE_THETA ** (-jnp.arange(0, QK_ROPE, 2, dtype=jnp.float32) / QK_ROPE)
    return positions.astype(jnp.float32)[..., None] * inv_freq


def apply_rope(x, ang):
    xf = x.astype(jnp.float32)
    x1, x2 = jnp.split(xf, 2, axis=-1)
    c, s = jnp.cos(ang), jnp.sin(ang)
    return jnp.concatenate([x1 * c - x2 * s, x2 * c + x1 * s], axis=-1).astype(x.dtype)


def chunk_mask(q_idx, k_idx):
    return (k_idx[None, :] // CHUNK) <= (q_idx[:, None] // CHUNK)


def gmlp_mixer(hn, w_in, ln_g, ln_b, w_s, b_s, w_out):
    B, S, _ = hn.shape
    z = jax.nn.gelu(hn @ w_in, approximate=False)
    u, v = jnp.split(z, 2, axis=-1)
    v = layernorm(v, ln_g, ln_b)
    nb = S // GMLP_BLOCK
    vb = v.reshape(B, nb, GMLP_BLOCK, A_GROUPS, A_GROUP_DIM)
    idx = jnp.arange(GMLP_BLOCK)
    ws = jnp.where(chunk_mask(idx, idx)[None], w_s, jnp.zeros_like(w_s))
    sv = jnp.einsum('gij,bnjgc->bnigc', ws, vb) + b_s.T[None, None, :, :, None]
    return (u * sv.reshape(B, S, GATE_DIM)) @ w_out


def shared_latent_kv(h, positions, src_g, w_kv_a, kv_a_g, w_kv_b):
    B, S, _ = h.shape
    hn = rmsnorm(h, src_g)
    ckv = hn @ w_kv_a
    c_kv, k_pe = ckv[..., :KV_LORA], ckv[..., KV_LORA:]
    c_kv = rmsnorm(c_kv, kv_a_g)
    kv = (c_kv @ w_kv_b).reshape(B, S, B_HEADS, QK_NOPE + V_HEAD)
    k_nope, v = kv[..., :QK_NOPE], kv[..., QK_NOPE:]
    k_pe = apply_rope(k_pe, rope_angles(positions))
    return k_nope, k_pe, v


def chunk_causal_mla_attention(q_nope, q_pe, k_nope, k_pe, v):
    B, S, H, _ = q_nope.shape
    nb = S // Q_BLOCK
    qn = q_nope.reshape(B, nb, Q_BLOCK, H, QK_NOPE).transpose(1, 0, 2, 3, 4)
    qp = q_pe.reshape(B, nb, Q_BLOCK, H, QK_ROPE).transpose(1, 0, 2, 3, 4)
    k_idx = jnp.arange(S)
    scale = (QK_NOPE + QK_ROPE) ** -0.5

    def one_block(args):
        qn_b, qp_b, i = args
        s = (jnp.einsum('bqhd,bkhd->bhqk', qn_b, k_nope).astype(jnp.float32)
             + jnp.einsum('bqhd,bkd->bhqk', qp_b, k_pe).astype(jnp.float32)) * scale
        q_idx = i * Q_BLOCK + jnp.arange(Q_BLOCK)
        s = jnp.where(chunk_mask(q_idx, k_idx)[None, None], s, jnp.finfo(jnp.float32).min)
        p = jax.nn.softmax(s, axis=-1).astype(v.dtype)
        return jnp.einsum('bhqk,bkhd->bqhd', p, v)

    out = lax.map(one_block, (qn, qp, jnp.arange(nb)))
    return out.transpose(1, 0, 2, 3, 4).reshape(B, S, H, V_HEAD)


def mla_mixer(hn, kv, positions, w_q_a, q_g, w_q_b, w_o):
    B, S, _ = hn.shape
    k_nope, k_pe, v = kv
    cq = rmsnorm(hn @ w_q_a, q_g)
    q = (cq @ w_q_b).reshape(B, S, B_HEADS, QK_NOPE + QK_ROPE)
    q_nope = q[..., :QK_NOPE]
    q_pe = apply_rope(q[..., QK_NOPE:], rope_angles(positions)[:, :, None, :])
    o = chunk_causal_mla_attention(q_nope, q_pe, k_nope, k_pe, v)
    return o.reshape(B, S, B_HEADS * V_HEAD) @ w_o


def sq_relu_mlp(hn, w1, w2):
    return jnp.square(jax.nn.relu(hn @ w1)) @ w2


def setup_inputs(seed: int = 0) -> dict:
    key = jax.random.key(seed)
    ks = jax.random.split(key, 24)

    def nrm(k, shape, fan_in, mult=1.0):
        return jax.random.normal(k, shape, jnp.float32) * (mult * fan_in ** -0.5)

    def gain(k, shape):
        return 1.0 + 0.05 * jax.random.normal(k, shape, jnp.float32)

    x = jax.random.normal(ks[0], (BATCH, SEQ, D_MODEL), jnp.float32)
    offset = jax.random.randint(ks[1], (BATCH, 1), 0, 4096, dtype=jnp.int32)
    positions = offset + jnp.arange(SEQ, dtype=jnp.int32)[None, :]
    return {
        "x": x,
        "positions": positions,
        "norm_mix_g": gain(ks[2], (DEPTH, D_MODEL)),
        "norm_mlp_g": gain(ks[3], (DEPTH, D_MODEL)),
        "a_w_in": nrm(ks[4], (N_A, D_MODEL, 2 * GATE_DIM), D_MODEL),
        "a_ln_v_g": gain(ks[5], (N_A, GATE_DIM)),
        "a_ln_v_b": 0.02 * jax.random.normal(ks[6], (N_A, GATE_DIM), jnp.float32),
        "a_w_s": nrm(ks[7], (N_A, A_GROUPS, GMLP_BLOCK, GMLP_BLOCK), GMLP_BLOCK, 0.5),
        "a_b_s": 1.0 + 0.1 * jax.random.normal(ks[8], (N_A, A_GROUPS, GMLP_BLOCK), jnp.float32),
        "a_w_out": nrm(ks[9], (N_A, GATE_DIM, D_MODEL), GATE_DIM),
        "b_w_q_a": nrm(ks[10], (N_B, D_MODEL, Q_LORA), D_MODEL),
        "b_q_norm_g": gain(ks[11], (N_B, Q_LORA)),
        "b_w_q_b": nrm(ks[12], (N_B, Q_LORA, B_HEADS * (QK_NOPE + QK_ROPE)), Q_LORA),
        "b_w_o": nrm(ks[13], (N_B, B_HEADS * V_HEAD, D_MODEL), B_HEADS * V_HEAD),
        "kv_src_norm_g": gain(ks[14], (D_MODEL,)),
        "kv_w_a": nrm(ks[15], (D_MODEL, KV_LORA + QK_ROPE), D_MODEL),
        "kv_a_norm_g": gain(ks[16], (KV_LORA,)),
        "kv_w_b": nrm(ks[17], (KV_LORA, B_HEADS * (QK_NOPE + V_HEAD)), KV_LORA),
        "mlp_w1": nrm(ks[18], (DEPTH, D_MODEL, D_FF), D_MODEL),
        "mlp_w2": nrm(ks[19], (DEPTH, D_FF, D_MODEL), D_FF, 0.5),
        "final_norm_g": gain(ks[20], (D_MODEL,)),
    }


def reference(x, positions, norm_mix_g, norm_mlp_g, a_w_in, a_ln_v_g, a_ln_v_b, a_w_s,
              a_b_s, a_w_out, b_w_q_a, b_q_norm_g, b_w_q_b, b_w_o, kv_src_norm_g, kv_w_a,
              kv_a_norm_g, kv_w_b, mlp_w1, mlp_w2, final_norm_g):
    h = x
    kv = None
    for l in range(DEPTH):
        hn = rmsnorm(h, norm_mix_g[l])
        if l < N_A:
            h = h + gmlp_mixer(hn, a_w_in[l], a_ln_v_g[l], a_ln_v_b[l], a_w_s[l],
                               a_b_s[l], a_w_out[l])
        else:
            j = l - N_A
            h = h + mla_mixer(hn, kv, positions, b_w_q_a[j], b_q_norm_g[j], b_w_q_b[j], b_w_o[j])
        h = h + sq_relu_mlp(rmsnorm(h, norm_mlp_g[l]), mlp_w1[l], mlp_w2[l])
        if l == N_A - 1:
            kv = shared_latent_kv(h, positions, kv_src_norm_g, kv_w_a, kv_a_norm_g, kv_w_b)
    return rmsnorm(h, final_norm_g)
```

```python
import functools

import jax
import jax.numpy as jnp
import numpy as np
from jax import lax
from jax.experimental import pallas as pl
from jax.experimental.pallas import tpu as pltpu

D_MODEL = 1024
BATCH = 8
SEQ = 2048
TOKENS = BATCH * SEQ
CHUNK = 64
CHUNK_SHIFT = 6
GMLP_BLOCK = 128
GATE_DIM = 2 * D_MODEL
A_GROUPS = 8
A_GROUP_DIM = GATE_DIM // A_GROUPS
B_HEADS = 8
QK_NOPE = 128
QK_ROPE = 64
V_HEAD = 128
Q_LORA = 384
KV_LORA = 256
ROPE_THETA = 10000.0
D_FF = 4 * D_MODEL
EPS = 1e-6

LANES = 128
QK_PAD = 2 * LANES
VMEM_LIMIT = 56 * 1024 * 1024

TM = 512
NCHUNK = 512
TQ = 256
TK = 256

F32 = jnp.float32
BF16 = jnp.bfloat16
NEG = float(np.finfo(np.float32).min)
SQRT_HALF = float(np.sqrt(0.5))
NT_DIMS = (((1,), (1,)), ((), ()))
TN_DIMS = (((0,), (0,)), ((), ()))


def _rms(x, g):
    ms = jnp.mean(x * x, axis=-1, keepdims=True)
    return x * lax.rsqrt(ms + EPS) * g


def _resident(shape):
    nd = len(shape)
    return pl.BlockSpec(shape, lambda *_: (0,) * nd, pipeline_mode=pl.Buffered(1))


def _params(*sem):
    return pltpu.CompilerParams(dimension_semantics=sem, vmem_limit_bytes=VMEM_LIMIT)


def _gmlp_kernel(h_ref, g_ref, w_in_ref, lng_ref, lnb_ref, ws_ref, bs_ref, w_out_ref,
                 o_ref, u_scr, v_scr, gated_scr, ws_scr):
    h = h_ref[...]
    hn = _rms(h, g_ref[...]).astype(BF16)

    for c in range(2 * GATE_DIM // NCHUNK):
        z = jnp.dot(hn, w_in_ref[:, c * NCHUNK:(c + 1) * NCHUNK], preferred_element_type=F32)
        z = 0.5 * z * (1.0 + lax.erf(z * SQRT_HALF))
        if c < GATE_DIM // NCHUNK:
            u_scr[:, c * NCHUNK:(c + 1) * NCHUNK] = z
        else:
            c2 = c - GATE_DIM // NCHUNK
            v_scr[:, c2 * NCHUNK:(c2 + 1) * NCHUNK] = z

    ii = lax.broadcasted_iota(jnp.int32, (GMLP_BLOCK, GMLP_BLOCK), 0) >> CHUNK_SHIFT
    jj = lax.broadcasted_iota(jnp.int32, (GMLP_BLOCK, GMLP_BLOCK), 1) >> CHUNK_SHIFT
    for g in range(A_GROUPS):
        ws_scr[g] = jnp.where(jj <= ii, ws_ref[g], 0.0).astype(BF16)

    def row_block(r, _):
        rows = pl.ds(pl.multiple_of(r * GMLP_BLOCK, GMLP_BLOCK), GMLP_BLOCK)
        v = v_scr[rows, :]
        mu = jnp.mean(v, axis=-1, keepdims=True)
        var = jnp.mean(jnp.square(v - mu), axis=-1, keepdims=True)
        rstd = lax.rsqrt(var + EPS)
        for g in range(A_GROUPS):
            cols = slice(g * A_GROUP_DIM, (g + 1) * A_GROUP_DIM)
            vg = v_scr[rows, cols]
            vn = ((vg - mu) * rstd * lng_ref[:, cols] + lnb_ref[:, cols]).astype(BF16)
            sv = jnp.dot(ws_scr[g], vn, preferred_element_type=F32) + bs_ref[g]
            gated_scr[rows, cols] = (u_scr[rows, cols] * sv).astype(BF16)
        return 0

    lax.fori_loop(0, TM // GMLP_BLOCK, row_block, 0)

    o_ref[...] = h + jnp.dot(gated_scr[...], w_out_ref[...], preferred_element_type=F32)


def _gmlp(h, g, w_in, lng, lnb, ws, bs, w_out):
    tok = pl.BlockSpec((TM, D_MODEL), lambda t: (t, 0))
    return pl.pallas_call(
        _gmlp_kernel,
        grid=(TOKENS // TM,),
        in_specs=[tok, _resident((1, D_MODEL)), _resident((D_MODEL, 2 * GATE_DIM)),
                  _resident((1, GATE_DIM)), _resident((1, GATE_DIM)),
                  _resident((A_GROUPS, GMLP_BLOCK, GMLP_BLOCK)),
                  _resident((A_GROUPS, GMLP_BLOCK, 1)), _resident((GATE_DIM, D_MODEL))],
        out_specs=tok,
        out_shape=jax.ShapeDtypeStruct((TOKENS, D_MODEL), F32),
        scratch_shapes=[pltpu.VMEM((TM, GATE_DIM), F32), pltpu.VMEM((TM, GATE_DIM), F32),
                        pltpu.VMEM((TM, GATE_DIM), BF16),
                        pltpu.VMEM((A_GROUPS, GMLP_BLOCK, GMLP_BLOCK), BF16)],
        compiler_params=_params("arbitrary"),
        name="gmlp_mixer",
    )(h, g, w_in, lng, lnb, ws, bs, w_out)


def _mlp_kernel(h_ref, g_ref, w1_ref, w2_ref, gf_ref, o_ref, a_scr, *, final_norm):
    h = h_ref[...]
    hn = _rms(h, g_ref[...]).astype(BF16)
    for c in range(D_FF // NCHUNK):
        cols = slice(c * NCHUNK, (c + 1) * NCHUNK)
        a = jnp.maximum(jnp.dot(hn, w1_ref[:, cols], preferred_element_type=F32), 0.0)
        a_scr[:, cols] = (a * a).astype(BF16)
    out = h + jnp.dot(a_scr[...], w2_ref[...], preferred_element_type=F32)
    if final_norm:
        out = _rms(out, gf_ref[...])
    o_ref[...] = out


def _mlp(h, g, w1, w2, gf, final_norm):
    tok = pl.BlockSpec((TM, D_MODEL), lambda t: (t, 0))
    return pl.pallas_call(
        functools.partial(_mlp_kernel, final_norm=final_norm),
        grid=(TOKENS // TM,),
        in_specs=[tok, _resident((1, D_MODEL)), _resident((D_MODEL, D_FF)),
                  _resident((D_FF, D_MODEL)), _resident((1, D_MODEL))],
        out_specs=tok,
        out_shape=jax.ShapeDtypeStruct((TOKENS, D_MODEL), F32),
        scratch_shapes=[pltpu.VMEM((TM, D_FF), BF16)],
        compiler_params=_params("arbitrary"),
        name="mlp_final" if final_norm else "mlp",
    )(h, g, w1, w2, gf)


def _rope(x, cos, sin_signed, lane):
    half = QK_ROPE // 2
    swapped = jnp.where(lane < half, pltpu.roll(x, LANES - half, 1), pltpu.roll(x, half, 1))
    return x * cos + swapped * sin_signed


def _proj_kernel(h_ref, pos_ref, rope_ref, gkv_ref, gq_ref, wkva_ref, gkva_ref, wk_ref, wvt_ref,
                 wqa_ref, gqa_ref, wqb_ref, q_ref, kn_ref, kpe_ref, vt_ref):
    h = h_ref[0]
    y = h * lax.rsqrt(jnp.mean(h * h, axis=-1, keepdims=True) + EPS)

    ang = pos_ref[0].astype(F32) * rope_ref[0:1, :]
    cos = jnp.cos(ang)
    sin_signed = jnp.sin(ang) * rope_ref[1:2, :]
    lane = lax.broadcasted_iota(jnp.int32, (TM, LANES), 1)

    ckv = jnp.dot((y * gkv_ref[...]).astype(BF16), wkva_ref[...], preferred_element_type=F32)
    c_kv = _rms(ckv[:, :KV_LORA], gkva_ref[...]).astype(BF16)
    kpe_ref[0] = _rope(ckv[:, KV_LORA:], cos, sin_signed, lane).astype(BF16)
    kn = jnp.dot(c_kv, wk_ref[...], preferred_element_type=F32).astype(BF16)
    for hd in range(B_HEADS):
        kn_ref[0, hd] = kn[:, hd * QK_NOPE:(hd + 1) * QK_NOPE]
    vt = lax.dot_general(wvt_ref[...], c_kv, NT_DIMS, preferred_element_type=F32).astype(BF16)
    for kb in range(TM // TK):
        vt_ref[0, kb] = vt[:, kb * TK:(kb + 1) * TK]

    scale = (QK_NOPE + QK_ROPE) ** -0.5
    cq = jnp.dot((y * gq_ref[...]).astype(BF16), wqa_ref[...], preferred_element_type=F32)
    cq = _rms(cq, gqa_ref[...]).astype(BF16)
    for hd in range(B_HEADS):
        q = jnp.dot(cq, wqb_ref[:, hd * QK_PAD:(hd + 1) * QK_PAD], preferred_element_type=F32)
        q_ref[0, hd, :, :QK_NOPE] = (q[:, :QK_NOPE] * scale).astype(BF16)
        q_ref[0, hd, :, QK_NOPE:] = (_rope(q[:, QK_NOPE:], cos, sin_signed, lane) * scale).astype(BF16)


def _proj(h, pos, rope_tab, gkv, gq, wkva, gkva, wk, wvt, wqa, gqa, wqb):
    n_t = SEQ // TM
    return pl.pallas_call(
        _proj_kernel,
        grid=(BATCH, n_t),
        in_specs=[pl.BlockSpec((1, TM, D_MODEL), lambda b, t: (b, t, 0)),
                  pl.BlockSpec((1, TM, 1), lambda b, t: (b, t, 0)),
                  _resident((8, LANES)), _resident((1, D_MODEL)), _resident((1, D_MODEL)),
                  _resident((D_MODEL, KV_LORA + LANES)), _resident((1, KV_LORA)),
                  _resident((KV_LORA, B_HEADS * QK_NOPE)), _resident((B_HEADS * V_HEAD, KV_LORA)),
                  _resident((D_MODEL, Q_LORA)), _resident((1, Q_LORA)),
                  _resident((Q_LORA, B_HEADS * QK_PAD))],
        out_specs=[pl.BlockSpec((1, B_HEADS, TM, QK_PAD), lambda b, t: (b, 0, t, 0)),
                   pl.BlockSpec((1, B_HEADS, TM, QK_NOPE), lambda b, t: (b, 0, t, 0)),
                   pl.BlockSpec((1, TM, LANES), lambda b, t: (b, t, 0)),
                   pl.BlockSpec((1, TM // TK, B_HEADS * V_HEAD, TK), lambda b, t: (b, t, 0, 0))],
        out_shape=[jax.ShapeDtypeStruct((BATCH, B_HEADS, SEQ, QK_PAD), BF16),
                   jax.ShapeDtypeStruct((BATCH, B_HEADS, SEQ, QK_NOPE), BF16),
                   jax.ShapeDtypeStruct((BATCH, SEQ, LANES), BF16),
                   jax.ShapeDtypeStruct((BATCH, SEQ // TK, B_HEADS * V_HEAD, TK), BF16)],
        compiler_params=_params("arbitrary", "arbitrary"),
        name="qkv_proj",
    )(h, pos, rope_tab, gkv, gq, wkva, gkva, wk, wvt, wqa, gqa, wqb)


def _attn_kernel(h_ref, q_ref, kn_ref, kpe_ref, vt_ref, wo_ref, o_ref, ot_scr):
    i = pl.program_id(1)
    kc_id = lax.broadcasted_iota(jnp.int32, (TK, TQ), 0) >> CHUNK_SHIFT
    qc_id = lax.broadcasted_iota(jnp.int32, (TK, TQ), 1) >> CHUNK_SHIFT
    diag_ok = kc_id <= qc_id

    for hd in range(B_HEADS):
        q_h = q_ref[0, hd]
        v_rows = slice(hd * V_HEAD, (hd + 1) * V_HEAD)

        def kv_block(j, carry, masked, hd=hd, q_h=q_h, v_rows=v_rows):
            m, l, acc = carry
            rows = pl.ds(pl.multiple_of(j * TK, TK), TK)
            k_cat = jnp.concatenate([kn_ref[0, hd, rows, :], kpe_ref[0, rows, :]], axis=-1)
            s = lax.dot_general(k_cat, q_h, NT_DIMS, preferred_element_type=F32)
            if masked:
                s = jnp.where(diag_ok, s, NEG)
            m_new = jnp.maximum(m, jnp.max(s, axis=0, keepdims=True))
            alpha = jnp.exp(m - m_new)
            p = jnp.exp(s - m_new)
            l = alpha * l + jnp.sum(p, axis=0, keepdims=True)
            pv = jnp.dot(vt_ref[0, j, v_rows, :], p.astype(BF16), preferred_element_type=F32)
            return m_new, l, alpha * acc + pv

        init = (jnp.full((1, TQ), -jnp.inf, F32), jnp.zeros((1, TQ), F32),
                jnp.zeros((V_HEAD, TQ), F32))
        carry = lax.fori_loop(0, i, functools.partial(kv_block, masked=False), init)
        _, l, acc = kv_block(i, carry, True)
        ot_scr[v_rows, :] = (acc * (1.0 / l)).astype(BF16)

    out = lax.dot_general(ot_scr[...], wo_ref[...], TN_DIMS, preferred_element_type=F32)
    o_ref[0] = h_ref[0] + out


def _attn(h, q, kn, kpe, vt, wo):
    return pl.pallas_call(
        _attn_kernel,
        grid=(BATCH, SEQ // TQ),
        in_specs=[pl.BlockSpec((1, TQ, D_MODEL), lambda b, i: (b, i, 0)),
                  pl.BlockSpec((1, B_HEADS, TQ, QK_PAD), lambda b, i: (b, 0, i, 0)),
                  pl.BlockSpec((1, B_HEADS, SEQ, QK_NOPE), lambda b, i: (b, 0, 0, 0)),
                  pl.BlockSpec((1, SEQ, LANES), lambda b, i: (b, 0, 0)),
                  pl.BlockSpec((1, SEQ // TK, B_HEADS * V_HEAD, TK), lambda b, i: (b, 0, 0, 0)),
                  _resident((B_HEADS * V_HEAD, D_MODEL))],
        out_specs=pl.BlockSpec((1, TQ, D_MODEL), lambda b, i: (b, i, 0)),
        out_shape=jax.ShapeDtypeStruct((BATCH, SEQ, D_MODEL), F32),
        scratch_shapes=[pltpu.VMEM((B_HEADS * V_HEAD, TQ), BF16)],
        compiler_params=_params("arbitrary", "arbitrary"),
        name="mla_attention",
    )(h, q, kn, kpe, vt, wo)


def _row(v):
    return v.reshape(1, -1)


def kernel(x, positions, norm_mix_g, norm_mlp_g, a_w_in, a_ln_v_g, a_ln_v_b, a_w_s, a_b_s, a_w_out,
           b_w_q_a, b_q_norm_g, b_w_q_b, b_w_o, kv_src_norm_g, kv_w_a, kv_a_norm_g, kv_w_b,
           mlp_w1, mlp_w2, final_norm_g):
    w_in = a_w_in[0].astype(BF16)
    w_out = a_w_out[0].astype(BF16)
    w1 = mlp_w1.astype(BF16)
    w2 = mlp_w2.astype(BF16)
    wkva = jnp.pad(kv_w_a, ((0, 0), (0, LANES - QK_ROPE))).astype(BF16)
    kvb = kv_w_b.reshape(KV_LORA, B_HEADS, QK_NOPE + V_HEAD)
    wk = kvb[:, :, :QK_NOPE].reshape(KV_LORA, B_HEADS * QK_NOPE).astype(BF16)
    wvt = kvb[:, :, QK_NOPE:].reshape(KV_LORA, B_HEADS * V_HEAD).T.astype(BF16)
    wqa = b_w_q_a[0].astype(BF16)
    wqb = jnp.pad(b_w_q_b[0].reshape(Q_LORA, B_HEADS, QK_NOPE + QK_ROPE),
                  ((0, 0), (0, 0), (0, QK_PAD - QK_NOPE - QK_ROPE)))
    wqb = wqb.reshape(Q_LORA, B_HEADS * QK_PAD).astype(BF16)
    wo = b_w_o[0].astype(BF16)

    half = QK_ROPE // 2
    inv_freq = ROPE_THETA ** (-jnp.arange(0, QK_ROPE, 2, dtype=F32) / QK_ROPE)
    rope_tab = jnp.zeros((8, LANES), F32)
    rope_tab = rope_tab.at[0, :half].set(inv_freq).at[0, half:QK_ROPE].set(inv_freq)
    rope_tab = rope_tab.at[1, :half].set(-1.0).at[1, half:QK_ROPE].set(1.0)

    h = x.reshape(TOKENS, D_MODEL)
    h = _gmlp(h, _row(norm_mix_g[0]), w_in, _row(a_ln_v_g[0]), _row(a_ln_v_b[0]), a_w_s[0],
              a_b_s[0].reshape(A_GROUPS, GMLP_BLOCK, 1), w_out)
    h = _mlp(h, _row(norm_mlp_g[0]), w1[0], w2[0], _row(final_norm_g), False)

    h3 = h.reshape(BATCH, SEQ, D_MODEL)
    q, kn, kpe, vt = _proj(h3, positions.reshape(BATCH, SEQ, 1), rope_tab, _row(kv_src_norm_g),
                           _row(norm_mix_g[1]), wkva, _row(kv_a_norm_g), wk, wvt, wqa,
                           _row(b_q_norm_g[0]), wqb)
    h3 = _attn(h3, q, kn, kpe, vt, wo)

    h = _mlp(h3.reshape(TOKENS, D_MODEL), _row(norm_mlp_g[1]), w1[1], w2[1], _row(final_norm_g), True)
    return h.reshape(BATCH, SEQ, D_MODEL)
```

```python
import functools

import jax
import jax.numpy as jnp
import numpy as np
from jax import lax
from jax.experimental import pallas as pl
from jax.experimental.pallas import tpu as pltpu

D_MODEL = 1024
BATCH = 8
SEQ = 2048
TOKENS = BATCH * SEQ
CHUNK = 64
CHUNK_SHIFT = 6
GMLP_BLOCK = 128
GATE_DIM = 2 * D_MODEL
A_GROUPS = 8
A_GROUP_DIM = GATE_DIM // A_GROUPS
B_HEADS = 8
QK_NOPE = 128
QK_ROPE = 64
V_HEAD = 128
Q_LORA = 384
KV_LORA = 256
ROPE_THETA = 10000.0
D_FF = 4 * D_MODEL
EPS = 1e-6

LANES = 128
QK_PAD = 2 * LANES
VMEM_LIMIT = 56 * 1024 * 1024

TM = 512
NCHUNK = 512
TQ = 256
TK = 256

F32 = jnp.float32
BF16 = jnp.bfloat16
NEG = float(np.finfo(np.float32).min)
SQRT_HALF = float(np.sqrt(0.5))
LOG2_E = float(np.log2(np.e))
NT_DIMS = (((1,), (1,)), ((), ()))
TN_DIMS = (((0,), (0,)), ((), ()))


def _rms(x, g):
    ms = jnp.mean(x * x, axis=-1, keepdims=True)
    return x * lax.rsqrt(ms + EPS) * g


def _resident(shape):
    nd = len(shape)
    return pl.BlockSpec(shape, lambda *_: (0,) * nd, pipeline_mode=pl.Buffered(1))


def _params(*sem):
    return pltpu.CompilerParams(dimension_semantics=sem, vmem_limit_bytes=VMEM_LIMIT)


def _gmlp_kernel(h_ref, g_ref, w_in_ref, lng_ref, lnb_ref, ws_ref, bs_ref, w_out_ref,
                 o_ref, u_scr, v_scr, gated_scr, ws_scr):
    h = h_ref[...]
    hn = _rms(h, g_ref[...]).astype(BF16)

    for c in range(2 * GATE_DIM // NCHUNK):
        z = jnp.dot(hn, w_in_ref[:, c * NCHUNK:(c + 1) * NCHUNK], preferred_element_type=F32)
        z = 0.5 * z * (1.0 + lax.erf(z * SQRT_HALF))
        if c < GATE_DIM // NCHUNK:
            u_scr[:, c * NCHUNK:(c + 1) * NCHUNK] = z
        else:
            c2 = c - GATE_DIM // NCHUNK
            v_scr[:, c2 * NCHUNK:(c2 + 1) * NCHUNK] = z

    ii = lax.broadcasted_iota(jnp.int32, (GMLP_BLOCK, GMLP_BLOCK), 0) >> CHUNK_SHIFT
    jj = lax.broadcasted_iota(jnp.int32, (GMLP_BLOCK, GMLP_BLOCK), 1) >> CHUNK_SHIFT
    for g in range(A_GROUPS):
        ws_scr[g] = jnp.where(jj <= ii, ws_ref[g], 0.0).astype(BF16)

    def row_block(r, _):
        rows = pl.ds(pl.multiple_of(r * GMLP_BLOCK, GMLP_BLOCK), GMLP_BLOCK)
        v = v_scr[rows, :]
        mu = jnp.mean(v, axis=-1, keepdims=True)
        var = jnp.mean(jnp.square(v - mu), axis=-1, keepdims=True)
        rstd = lax.rsqrt(var + EPS)
        for g in range(A_GROUPS):
            cols = slice(g * A_GROUP_DIM, (g + 1) * A_GROUP_DIM)
            vg = v_scr[rows, cols]
            vn = ((vg - mu) * rstd * lng_ref[:, cols] + lnb_ref[:, cols]).astype(BF16)
            sv = jnp.dot(ws_scr[g], vn, preferred_element_type=F32) + bs_ref[g]
            gated_scr[rows, cols] = (u_scr[rows, cols] * sv).astype(BF16)
        return 0

    lax.fori_loop(0, TM // GMLP_BLOCK, row_block, 0)

    o_ref[...] = h + jnp.dot(gated_scr[...], w_out_ref[...], preferred_element_type=F32)


def _gmlp(h, g, w_in, lng, lnb, ws, bs, w_out):
    tok = pl.BlockSpec((TM, D_MODEL), lambda t: (t, 0))
    return pl.pallas_call(
        _gmlp_kernel,
        grid=(TOKENS // TM,),
        in_specs=[tok, _resident((1, D_MODEL)), _resident((D_MODEL, 2 * GATE_DIM)),
                  _resident((1, GATE_DIM)), _resident((1, GATE_DIM)),
                  _resident((A_GROUPS, GMLP_BLOCK, GMLP_BLOCK)),
                  _resident((A_GROUPS, GMLP_BLOCK, 1)), _resident((GATE_DIM, D_MODEL))],
        out_specs=tok,
        out_shape=jax.ShapeDtypeStruct((TOKENS, D_MODEL), F32),
        scratch_shapes=[pltpu.VMEM((TM, GATE_DIM), F32), pltpu.VMEM((TM, GATE_DIM), F32),
                        pltpu.VMEM((TM, GATE_DIM), BF16),
                        pltpu.VMEM((A_GROUPS, GMLP_BLOCK, GMLP_BLOCK), BF16)],
        compiler_params=_params("arbitrary"),
        name="gmlp_mixer",
    )(h, g, w_in, lng, lnb, ws, bs, w_out)


def _mlp_kernel(h_ref, g_ref, w1_ref, w2_ref, gf_ref, o_ref, a_scr, *, final_norm):
    h = h_ref[...]
    hn = _rms(h, g_ref[...]).astype(BF16)
    for c in range(D_FF // NCHUNK):
        cols = slice(c * NCHUNK, (c + 1) * NCHUNK)
        a = jnp.maximum(jnp.dot(hn, w1_ref[:, cols], preferred_element_type=F32), 0.0)
        a_scr[:, cols] = (a * a).astype(BF16)
    out = h + jnp.dot(a_scr[...], w2_ref[...], preferred_element_type=F32)
    if final_norm:
        out = _rms(out, gf_ref[...])
    o_ref[...] = out


def _mlp(h, g, w1, w2, gf, final_norm):
    tok = pl.BlockSpec((TM, D_MODEL), lambda t: (t, 0))
    return pl.pallas_call(
        functools.partial(_mlp_kernel, final_norm=final_norm),
        grid=(TOKENS // TM,),
        in_specs=[tok, _resident((1, D_MODEL)), _resident((D_MODEL, D_FF)),
                  _resident((D_FF, D_MODEL)), _resident((1, D_MODEL))],
        out_specs=tok,
        out_shape=jax.ShapeDtypeStruct((TOKENS, D_MODEL), F32),
        scratch_shapes=[pltpu.VMEM((TM, D_FF), BF16)],
        compiler_params=_params("arbitrary"),
        name="mlp_final" if final_norm else "mlp",
    )(h, g, w1, w2, gf)


def _rope(x, cos, sin_signed, lane):
    half = QK_ROPE // 2
    swapped = jnp.where(lane < half, pltpu.roll(x, LANES - half, 1), pltpu.roll(x, half, 1))
    return x * cos + swapped * sin_signed


def _proj_kernel(h_ref, pos_ref, rope_ref, gkv_ref, gq_ref, wkva_ref, gkva_ref, wk_ref, wvt_ref,
                 wqa_ref, gqa_ref, wqb_ref, q_ref, kn_ref, kpe_ref, vt_ref):
    h = h_ref[0]
    y = h * lax.rsqrt(jnp.mean(h * h, axis=-1, keepdims=True) + EPS)

    ang = pos_ref[0].astype(F32) * rope_ref[0:1, :]
    cos = jnp.cos(ang)
    sin_signed = jnp.sin(ang) * rope_ref[1:2, :]
    lane = lax.broadcasted_iota(jnp.int32, (TM, LANES), 1)

    ckv = jnp.dot((y * gkv_ref[...]).astype(BF16), wkva_ref[...], preferred_element_type=F32)
    c_kv = _rms(ckv[:, :KV_LORA], gkva_ref[...]).astype(BF16)
    kpe_ref[0] = _rope(ckv[:, KV_LORA:], cos, sin_signed, lane).astype(BF16)
    kn = jnp.dot(c_kv, wk_ref[...], preferred_element_type=F32).astype(BF16)
    for hd in range(B_HEADS):
        kn_ref[0, hd] = kn[:, hd * QK_NOPE:(hd + 1) * QK_NOPE]
    vt = lax.dot_general(wvt_ref[...], c_kv, NT_DIMS, preferred_element_type=F32).astype(BF16)
    for kb in range(TM // TK):
        vt_ref[0, kb] = vt[:, kb * TK:(kb + 1) * TK]

    scale = (QK_NOPE + QK_ROPE) ** -0.5 * LOG2_E
    cq = jnp.dot((y * gq_ref[...]).astype(BF16), wqa_ref[...], preferred_element_type=F32)
    cq = _rms(cq, gqa_ref[...]).astype(BF16)
    for hd in range(B_HEADS):
        q = jnp.dot(cq, wqb_ref[:, hd * QK_PAD:(hd + 1) * QK_PAD], preferred_element_type=F32)
        q_ref[0, hd, :, :QK_NOPE] = (q[:, :QK_NOPE] * scale).astype(BF16)
        q_ref[0, hd, :, QK_NOPE:] = (_rope(q[:, QK_NOPE:], cos, sin_signed, lane) * scale).astype(BF16)


def _proj(h, pos, rope_tab, gkv, gq, wkva, gkva, wk, wvt, wqa, gqa, wqb):
    n_t = SEQ // TM
    return pl.pallas_call(
        _proj_kernel,
        grid=(BATCH, n_t),
        in_specs=[pl.BlockSpec((1, TM, D_MODEL), lambda b, t: (b, t, 0)),
                  pl.BlockSpec((1, TM, 1), lambda b, t: (b, t, 0)),
                  _resident((8, LANES)), _resident((1, D_MODEL)), _resident((1, D_MODEL)),
                  _resident((D_MODEL, KV_LORA + LANES)), _resident((1, KV_LORA)),
                  _resident((KV_LORA, B_HEADS * QK_NOPE)), _resident((B_HEADS * V_HEAD, KV_LORA)),
                  _resident((D_MODEL, Q_LORA)), _resident((1, Q_LORA)),
                  _resident((Q_LORA, B_HEADS * QK_PAD))],
        out_specs=[pl.BlockSpec((1, B_HEADS, TM, QK_PAD), lambda b, t: (b, 0, t, 0)),
                   pl.BlockSpec((1, B_HEADS, TM, QK_NOPE), lambda b, t: (b, 0, t, 0)),
                   pl.BlockSpec((1, TM, LANES), lambda b, t: (b, t, 0)),
                   pl.BlockSpec((1, TM // TK, B_HEADS * V_HEAD, TK), lambda b, t: (b, t, 0, 0))],
        out_shape=[jax.ShapeDtypeStruct((BATCH, B_HEADS, SEQ, QK_PAD), BF16),
                   jax.ShapeDtypeStruct((BATCH, B_HEADS, SEQ, QK_NOPE), BF16),
                   jax.ShapeDtypeStruct((BATCH, SEQ, LANES), BF16),
                   jax.ShapeDtypeStruct((BATCH, SEQ // TK, B_HEADS * V_HEAD, TK), BF16)],
        compiler_params=_params("arbitrary", "arbitrary"),
        name="qkv_proj",
    )(h, pos, rope_tab, gkv, gq, wkva, gkva, wk, wvt, wqa, gqa, wqb)


def _attn_kernel(h_ref, q_ref, kn_ref, kpe_ref, vt_ref, wo_ref, o_ref, m_scr, l_scr, acc_scr, ot_scr):
    i = pl.program_id(1)
    kc_id = lax.broadcasted_iota(jnp.int32, (TK, TQ), 0) >> CHUNK_SHIFT
    qc_id = lax.broadcasted_iota(jnp.int32, (TK, TQ), 1) >> CHUNK_SHIFT
    diag_ok = kc_id <= qc_id

    def scores(j, hd):
        rows = pl.ds(pl.multiple_of(j * TK, TK), TK)
        k_cat = jnp.concatenate([kn_ref[0, hd, rows, :], kpe_ref[0, rows, :]], axis=-1)
        return lax.dot_general(k_cat, q_ref[0, hd], NT_DIMS, preferred_element_type=F32)

    def pv(j, hd, p):
        v_t = vt_ref[0, j, hd * V_HEAD:(hd + 1) * V_HEAD, :]
        return jnp.dot(v_t, p.astype(BF16), preferred_element_type=F32)

    def kv_tile(j, diagonal):
        s_all = [scores(j, hd) for hd in range(B_HEADS)]
        for hd in range(B_HEADS):
            s = s_all[hd]
            if diagonal:
                s = jnp.where(diag_ok, s, NEG)
                m_new = jnp.max(s, axis=0, keepdims=True)
                p = jnp.exp2(s - m_new)
                l_scr[hd] = jnp.sum(p, axis=0, keepdims=True)
                acc_scr[hd] = pv(j, hd, p)
            else:
                m = m_scr[hd]
                m_new = jnp.maximum(m, jnp.max(s, axis=0, keepdims=True))
                alpha = jnp.exp2(m - m_new)
                p = jnp.exp2(s - m_new)
                l_scr[hd] = alpha * l_scr[hd] + jnp.sum(p, axis=0, keepdims=True)
                acc_scr[hd] = alpha * acc_scr[hd] + pv(j, hd, p)
            m_scr[hd] = m_new

    kv_tile(i, True)

    def earlier_tile(j, _):
        kv_tile(j, False)
        return 0

    lax.fori_loop(0, i, earlier_tile, 0)

    for hd in range(B_HEADS):
        ot_scr[hd * V_HEAD:(hd + 1) * V_HEAD, :] = (acc_scr[hd] * (1.0 / l_scr[hd])).astype(BF16)
    out = lax.dot_general(ot_scr[...], wo_ref[...], TN_DIMS, preferred_element_type=F32)
    o_ref[0] = h_ref[0] + out


def _attn(h, q, kn, kpe, vt, wo):
    return pl.pallas_call(
        _attn_kernel,
        grid=(BATCH, SEQ // TQ),
        in_specs=[pl.BlockSpec((1, TQ, D_MODEL), lambda b, i: (b, i, 0)),
                  pl.BlockSpec((1, B_HEADS, TQ, QK_PAD), lambda b, i: (b, 0, i, 0)),
                  pl.BlockSpec((1, B_HEADS, SEQ, QK_NOPE), lambda b, i: (b, 0, 0, 0)),
                  pl.BlockSpec((1, SEQ, LANES), lambda b, i: (b, 0, 0)),
                  pl.BlockSpec((1, SEQ // TK, B_HEADS * V_HEAD, TK), lambda b, i: (b, 0, 0, 0)),
                  _resident((B_HEADS * V_HEAD, D_MODEL))],
        out_specs=pl.BlockSpec((1, TQ, D_MODEL), lambda b, i: (b, i, 0)),
        out_shape=jax.ShapeDtypeStruct((BATCH, SEQ, D_MODEL), F32),
        scratch_shapes=[pltpu.VMEM((B_HEADS, 1, TQ), F32), pltpu.VMEM((B_HEADS, 1, TQ), F32),
                        pltpu.VMEM((B_HEADS, V_HEAD, TQ), F32),
                        pltpu.VMEM((B_HEADS * V_HEAD, TQ), BF16)],
        compiler_params=_params("arbitrary", "arbitrary"),
        name="mla_attention",
    )(h, q, kn, kpe, vt, wo)


def _row(v):
    return v.reshape(1, -1)


def kernel(x, positions, norm_mix_g, norm_mlp_g, a_w_in, a_ln_v_g, a_ln_v_b, a_w_s, a_b_s, a_w_out,
           b_w_q_a, b_q_norm_g, b_w_q_b, b_w_o, kv_src_norm_g, kv_w_a, kv_a_norm_g, kv_w_b,
           mlp_w1, mlp_w2, final_norm_g):
    w_in = a_w_in[0].astype(BF16)
    w_out = a_w_out[0].astype(BF16)
    w1 = mlp_w1.astype(BF16)
    w2 = mlp_w2.astype(BF16)
    wkva = jnp.pad(kv_w_a, ((0, 0), (0, LANES - QK_ROPE))).astype(BF16)
    kvb = kv_w_b.reshape(KV_LORA, B_HEADS, QK_NOPE + V_HEAD)
    wk = kvb[:, :, :QK_NOPE].reshape(KV_LORA, B_HEADS * QK_NOPE).astype(BF16)
    wvt = kvb[:, :, QK_NOPE:].reshape(KV_LORA, B_HEADS * V_HEAD).T.astype(BF16)
    wqa = b_w_q_a[0].astype(BF16)
    wqb = jnp.pad(b_w_q_b[0].reshape(Q_LORA, B_HEADS, QK_NOPE + QK_ROPE),
                  ((0, 0), (0, 0), (0, QK_PAD - QK_NOPE - QK_ROPE)))
    wqb = wqb.reshape(Q_LORA, B_HEADS * QK_PAD).astype(BF16)
    wo = b_w_o[0].astype(BF16)

    half = QK_ROPE // 2
    inv_freq = ROPE_THETA ** (-jnp.arange(0, QK_ROPE, 2, dtype=F32) / QK_ROPE)
    rope_tab = jnp.zeros((8, LANES), F32)
    rope_tab = rope_tab.at[0, :half].set(inv_freq).at[0, half:QK_ROPE].set(inv_freq)
    rope_tab = rope_tab.at[1, :half].set(-1.0).at[1, half:QK_ROPE].set(1.0)

    h = x.reshape(TOKENS, D_MODEL)
    h = _gmlp(h, _row(norm_mix_g[0]), w_in, _row(a_ln_v_g[0]), _row(a_ln_v_b[0]), a_w_s[0],
              a_b_s[0].reshape(A_GROUPS, GMLP_BLOCK, 1), w_out)
    h = _mlp(h, _row(norm_mlp_g[0]), w1[0], w2[0], _row(final_norm_g), False)

    h3 = h.reshape(BATCH, SEQ, D_MODEL)
    q, kn, kpe, vt = _proj(h3, positions.reshape(BATCH, SEQ, 1), rope_tab, _row(kv_src_norm_g),
                           _row(norm_mix_g[1]), wkva, _row(kv_a_norm_g), wk, wvt, wqa,
                           _row(b_q_norm_g[0]), wqb)
    h3 = _attn(h3, q, kn, kpe, vt, wo)

    h = _mlp(h3.reshape(TOKENS, D_MODEL), _row(norm_mlp_g[1]), w1[1], w2[1], _row(final_norm_g), True)
    return h.reshape(BATCH, SEQ, D_MODEL)
```

```python
import functools

import jax
import jax.numpy as jnp
import numpy as np
from jax import lax
from jax.experimental import pallas as pl
from jax.experimental.pallas import tpu as pltpu

D_MODEL = 1024
BATCH = 8
SEQ = 2048
TOKENS = BATCH * SEQ
CHUNK = 64
CHUNK_SHIFT = 6
GMLP_BLOCK = 128
GATE_DIM = 2 * D_MODEL
A_GROUPS = 8
A_GROUP_DIM = GATE_DIM // A_GROUPS
B_HEADS = 8
QK_NOPE = 128
QK_ROPE = 64
V_HEAD = 128
Q_LORA = 384
KV_LORA = 256
ROPE_THETA = 10000.0
D_FF = 4 * D_MODEL
EPS = 1e-6

LANES = 128
QK_PAD = 2 * LANES
VMEM_LIMIT = 56 * 1024 * 1024

TM = 512
NCHUNK = 512
TQ = 256
TK = 256
ONES_ROWS = 16

F32 = jnp.float32
BF16 = jnp.bfloat16
NEG = float(np.finfo(np.float32).min)
SQRT_HALF = float(np.sqrt(0.5))
LOG2_E = float(np.log2(np.e))
NT_DIMS = (((1,), (1,)), ((), ()))
TN_DIMS = (((0,), (0,)), ((), ()))


def _rms(x, g):
    ms = jnp.mean(x * x, axis=-1, keepdims=True)
    return x * lax.rsqrt(ms + EPS) * g


def _resident(shape):
    nd = len(shape)
    return pl.BlockSpec(shape, lambda *_: (0,) * nd, pipeline_mode=pl.Buffered(1))


def _params(*sem):
    return pltpu.CompilerParams(dimension_semantics=sem, vmem_limit_bytes=VMEM_LIMIT)


def _gmlp_kernel(h_ref, g_ref, w_in_ref, lng_ref, lnb_ref, ws_ref, bs_ref, w_out_ref,
                 o_ref, u_scr, v_scr, gated_scr, ws_scr):
    h = h_ref[...]
    hn = _rms(h, g_ref[...]).astype(BF16)

    for c in range(2 * GATE_DIM // NCHUNK):
        z = jnp.dot(hn, w_in_ref[:, c * NCHUNK:(c + 1) * NCHUNK], preferred_element_type=F32)
        z = 0.5 * z * (1.0 + lax.erf(z * SQRT_HALF))
        if c < GATE_DIM // NCHUNK:
            u_scr[:, c * NCHUNK:(c + 1) * NCHUNK] = z
        else:
            c2 = c - GATE_DIM // NCHUNK
            v_scr[:, c2 * NCHUNK:(c2 + 1) * NCHUNK] = z

    ii = lax.broadcasted_iota(jnp.int32, (GMLP_BLOCK, GMLP_BLOCK), 0) >> CHUNK_SHIFT
    jj = lax.broadcasted_iota(jnp.int32, (GMLP_BLOCK, GMLP_BLOCK), 1) >> CHUNK_SHIFT
    for g in range(A_GROUPS):
        ws_scr[g] = jnp.where(jj <= ii, ws_ref[g], 0.0).astype(BF16)

    def row_block(r, _):
        rows = pl.ds(pl.multiple_of(r * GMLP_BLOCK, GMLP_BLOCK), GMLP_BLOCK)
        v = v_scr[rows, :]
        mu = jnp.mean(v, axis=-1, keepdims=True)
        var = jnp.mean(jnp.square(v - mu), axis=-1, keepdims=True)
        rstd = lax.rsqrt(var + EPS)
        for g in range(A_GROUPS):
            cols = slice(g * A_GROUP_DIM, (g + 1) * A_GROUP_DIM)
            vg = v_scr[rows, cols]
            vn = ((vg - mu) * rstd * lng_ref[:, cols] + lnb_ref[:, cols]).astype(BF16)
            sv = jnp.dot(ws_scr[g], vn, preferred_element_type=F32) + bs_ref[g]
            gated_scr[rows, cols] = (u_scr[rows, cols] * sv).astype(BF16)
        return 0

    lax.fori_loop(0, TM // GMLP_BLOCK, row_block, 0)

    o_ref[...] = h + jnp.dot(gated_scr[...], w_out_ref[...], preferred_element_type=F32)


def _gmlp(h, g, w_in, lng, lnb, ws, bs, w_out):
    tok = pl.BlockSpec((TM, D_MODEL), lambda t: (t, 0))
    return pl.pallas_call(
        _gmlp_kernel,
        grid=(TOKENS // TM,),
        in_specs=[tok, _resident((1, D_MODEL)), _resident((D_MODEL, 2 * GATE_DIM)),
                  _resident((1, GATE_DIM)), _resident((1, GATE_DIM)),
                  _resident((A_GROUPS, GMLP_BLOCK, GMLP_BLOCK)),
                  _resident((A_GROUPS, GMLP_BLOCK, 1)), _resident((GATE_DIM, D_MODEL))],
        out_specs=tok,
        out_shape=jax.ShapeDtypeStruct((TOKENS, D_MODEL), F32),
        scratch_shapes=[pltpu.VMEM((TM, GATE_DIM), F32), pltpu.VMEM((TM, GATE_DIM), F32),
                        pltpu.VMEM((TM, GATE_DIM), BF16),
                        pltpu.VMEM((A_GROUPS, GMLP_BLOCK, GMLP_BLOCK), BF16)],
        compiler_params=_params("arbitrary"),
        name="gmlp_mixer",
    )(h, g, w_in, lng, lnb, ws, bs, w_out)


def _mlp_kernel(h_ref, g_ref, w1_ref, w2_ref, gf_ref, o_ref, a_scr, *, final_norm):
    h = h_ref[...]
    hn = _rms(h, g_ref[...]).astype(BF16)
    for c in range(D_FF // NCHUNK):
        cols = slice(c * NCHUNK, (c + 1) * NCHUNK)
        a = jnp.maximum(jnp.dot(hn, w1_ref[:, cols], preferred_element_type=F32), 0.0)
        a_scr[:, cols] = (a * a).astype(BF16)
    out = h + jnp.dot(a_scr[...], w2_ref[...], preferred_element_type=F32)
    if final_norm:
        out = _rms(out, gf_ref[...])
    o_ref[...] = out


def _mlp(h, g, w1, w2, gf, final_norm):
    tok = pl.BlockSpec((TM, D_MODEL), lambda t: (t, 0))
    return pl.pallas_call(
        functools.partial(_mlp_kernel, final_norm=final_norm),
        grid=(TOKENS // TM,),
        in_specs=[tok, _resident((1, D_MODEL)), _resident((D_MODEL, D_FF)),
                  _resident((D_FF, D_MODEL)), _resident((1, D_MODEL))],
        out_specs=tok,
        out_shape=jax.ShapeDtypeStruct((TOKENS, D_MODEL), F32),
        scratch_shapes=[pltpu.VMEM((TM, D_FF), BF16)],
        compiler_params=_params("arbitrary"),
        name="mlp_final" if final_norm else "mlp",
    )(h, g, w1, w2, gf)


def _rope(x, cos, sin_signed, lane):
    half = QK_ROPE // 2
    swapped = jnp.where(lane < half, pltpu.roll(x, LANES - half, 1), pltpu.roll(x, half, 1))
    return x * cos + swapped * sin_signed


def _proj_kernel(h_ref, pos_ref, rope_ref, gkv_ref, gq_ref, wkva_ref, gkva_ref, wk_ref, wvt_ref,
                 wqa_ref, gqa_ref, wqb_ref, q_ref, kn_ref, kpe_ref, vt_ref):
    h = h_ref[0]
    y = h * lax.rsqrt(jnp.mean(h * h, axis=-1, keepdims=True) + EPS)

    ang = pos_ref[0].astype(F32) * rope_ref[0:1, :]
    cos = jnp.cos(ang)
    sin_signed = jnp.sin(ang) * rope_ref[1:2, :]
    lane = lax.broadcasted_iota(jnp.int32, (TM, LANES), 1)

    ckv = jnp.dot((y * gkv_ref[...]).astype(BF16), wkva_ref[...], preferred_element_type=F32)
    c_kv = _rms(ckv[:, :KV_LORA], gkva_ref[...]).astype(BF16)
    kpe_ref[0] = _rope(ckv[:, KV_LORA:], cos, sin_signed, lane).astype(BF16)
    kn = jnp.dot(c_kv, wk_ref[...], preferred_element_type=F32).astype(BF16)
    for hd in range(B_HEADS):
        kn_ref[0, hd] = kn[:, hd * QK_NOPE:(hd + 1) * QK_NOPE]
    vt = lax.dot_general(wvt_ref[...], c_kv, NT_DIMS, preferred_element_type=F32).astype(BF16)
    for kb in range(TM // TK):
        vt_ref[0, kb] = vt[:, kb * TK:(kb + 1) * TK]

    scale = (QK_NOPE + QK_ROPE) ** -0.5 * LOG2_E
    cq = jnp.dot((y * gq_ref[...]).astype(BF16), wqa_ref[...], preferred_element_type=F32)
    cq = _rms(cq, gqa_ref[...]).astype(BF16)
    for hd in range(B_HEADS):
        q = jnp.dot(cq, wqb_ref[:, hd * QK_PAD:(hd + 1) * QK_PAD], preferred_element_type=F32)
        q_ref[0, hd, :, :QK_NOPE] = (q[:, :QK_NOPE] * scale).astype(BF16)
        q_ref[0, hd, :, QK_NOPE:] = (_rope(q[:, QK_NOPE:], cos, sin_signed, lane) * scale).astype(BF16)


def _proj(h, pos, rope_tab, gkv, gq, wkva, gkva, wk, wvt, wqa, gqa, wqb):
    n_t = SEQ // TM
    return pl.pallas_call(
        _proj_kernel,
        grid=(BATCH, n_t),
        in_specs=[pl.BlockSpec((1, TM, D_MODEL), lambda b, t: (b, t, 0)),
                  pl.BlockSpec((1, TM, 1), lambda b, t: (b, t, 0)),
                  _resident((8, LANES)), _resident((1, D_MODEL)), _resident((1, D_MODEL)),
                  _resident((D_MODEL, KV_LORA + LANES)), _resident((1, KV_LORA)),
                  _resident((KV_LORA, B_HEADS * QK_NOPE)), _resident((B_HEADS * V_HEAD, KV_LORA)),
                  _resident((D_MODEL, Q_LORA)), _resident((1, Q_LORA)),
                  _resident((Q_LORA, B_HEADS * QK_PAD))],
        out_specs=[pl.BlockSpec((1, B_HEADS, TM, QK_PAD), lambda b, t: (b, 0, t, 0)),
                   pl.BlockSpec((1, B_HEADS, TM, QK_NOPE), lambda b, t: (b, 0, t, 0)),
                   pl.BlockSpec((1, TM, LANES), lambda b, t: (b, t, 0)),
                   pl.BlockSpec((1, TM // TK, B_HEADS * V_HEAD, TK), lambda b, t: (b, t, 0, 0))],
        out_shape=[jax.ShapeDtypeStruct((BATCH, B_HEADS, SEQ, QK_PAD), BF16),
                   jax.ShapeDtypeStruct((BATCH, B_HEADS, SEQ, QK_NOPE), BF16),
                   jax.ShapeDtypeStruct((BATCH, SEQ, LANES), BF16),
                   jax.ShapeDtypeStruct((BATCH, SEQ // TK, B_HEADS * V_HEAD, TK), BF16)],
        compiler_params=_params("arbitrary", "arbitrary"),
        name="qkv_proj",
    )(h, pos, rope_tab, gkv, gq, wkva, gkva, wk, wvt, wqa, gqa, wqb)


def _attn_kernel(h_ref, q_ref, kn_ref, kpe_ref, vt_ref, wo_ref, o_ref, sa_scr, sb_scr, mta_scr, mtb_scr,
                 m_scr, acc_scr, ot_scr):
    i = pl.program_id(1)
    kc_id = lax.broadcasted_iota(jnp.int32, (TK, TQ), 0) >> CHUNK_SHIFT
    qc_id = lax.broadcasted_iota(jnp.int32, (TK, TQ), 1) >> CHUNK_SHIFT
    diag_ok = kc_id <= qc_id
    ones_rows = jnp.ones((ONES_ROWS, TK), BF16)

    def score_stage(j, s_scr, mt_scr, diagonal):
        rows = pl.ds(pl.multiple_of(j * TK, TK), TK)
        k_pe = kpe_ref[0, rows, :]
        for hd in range(B_HEADS):
            k_cat = jnp.concatenate([kn_ref[0, hd, rows, :], k_pe], axis=-1)
            s = lax.dot_general(k_cat, q_ref[0, hd], NT_DIMS, preferred_element_type=F32)
            if diagonal:
                s = jnp.where(diag_ok, s, NEG)
            s_scr[hd] = s
            mt_scr[hd] = jnp.max(s, axis=0, keepdims=True)

    def softmax_stage(j, s_scr, mt_scr, first):
        for hd in range(B_HEADS):
            v_aug = jnp.concatenate([vt_ref[0, j, hd * V_HEAD:(hd + 1) * V_HEAD, :], ones_rows], axis=0)
            if first:
                m_new = mt_scr[hd]
                p = jnp.exp2(s_scr[hd] - m_new)
                acc_scr[hd] = jnp.dot(v_aug, p.astype(BF16), preferred_element_type=F32)
            else:
                m = m_scr[hd]
                m_new = jnp.maximum(m, mt_scr[hd])
                alpha = jnp.exp2(m - m_new)
                p = jnp.exp2(s_scr[hd] - m_new)
                acc_scr[hd] = alpha * acc_scr[hd] + jnp.dot(v_aug, p.astype(BF16),
                                                            preferred_element_type=F32)
            m_scr[hd] = m_new

    score_stage(i, sb_scr, mtb_scr, True)
    score_stage(0, sa_scr, mta_scr, False)
    softmax_stage(i, sb_scr, mtb_scr, True)

    def tile_pair(u, _):
        j = 2 * u
        score_stage(j + 1, sb_scr, mtb_scr, False)
        softmax_stage(j, sa_scr, mta_scr, False)
        score_stage(j + 2, sa_scr, mta_scr, False)
        softmax_stage(j + 1, sb_scr, mtb_scr, False)
        return 0

    lax.fori_loop(0, jnp.maximum(i - 1, 0) >> 1, tile_pair, 0)

    @pl.when((i & 1) == 1)
    def _():
        softmax_stage(i - 1, sa_scr, mta_scr, False)

    @pl.when(jnp.logical_and((i & 1) == 0, i > 0))
    def _():
        score_stage(i - 1, sb_scr, mtb_scr, False)
        softmax_stage(i - 2, sa_scr, mta_scr, False)
        softmax_stage(i - 1, sb_scr, mtb_scr, False)

    for hd in range(B_HEADS):
        inv_l = 1.0 / acc_scr[hd, V_HEAD:V_HEAD + 1, :]
        ot_scr[hd * V_HEAD:(hd + 1) * V_HEAD, :] = (acc_scr[hd, :V_HEAD, :] * inv_l).astype(BF16)
    out = lax.dot_general(ot_scr[...], wo_ref[...], TN_DIMS, preferred_element_type=F32)
    o_ref[0] = h_ref[0] + out


def _attn(h, q, kn, kpe, vt, wo):
    return pl.pallas_call(
        _attn_kernel,
        grid=(BATCH, SEQ // TQ),
        in_specs=[pl.BlockSpec((1, TQ, D_MODEL), lambda b, i: (b, i, 0)),
                  pl.BlockSpec((1, B_HEADS, TQ, QK_PAD), lambda b, i: (b, 0, i, 0)),
                  pl.BlockSpec((1, B_HEADS, SEQ, QK_NOPE), lambda b, i: (b, 0, 0, 0)),
                  pl.BlockSpec((1, SEQ, LANES), lambda b, i: (b, 0, 0)),
                  pl.BlockSpec((1, SEQ // TK, B_HEADS * V_HEAD, TK), lambda b, i: (b, 0, 0, 0)),
                  _resident((B_HEADS * V_HEAD, D_MODEL))],
        out_specs=pl.BlockSpec((1, TQ, D_MODEL), lambda b, i: (b, i, 0)),
        out_shape=jax.ShapeDtypeStruct((BATCH, SEQ, D_MODEL), F32),
        scratch_shapes=[pltpu.VMEM((B_HEADS, TK, TQ), F32), pltpu.VMEM((B_HEADS, TK, TQ), F32),
                        pltpu.VMEM((B_HEADS, 1, TQ), F32), pltpu.VMEM((B_HEADS, 1, TQ), F32),
                        pltpu.VMEM((B_HEADS, 1, TQ), F32),
                        pltpu.VMEM((B_HEADS, V_HEAD + ONES_ROWS, TQ), F32),
                        pltpu.VMEM((B_HEADS * V_HEAD, TQ), BF16)],
        compiler_params=_params("arbitrary", "arbitrary"),
        name="mla_attention",
    )(h, q, kn, kpe, vt, wo)


def _row(v):
    return v.reshape(1, -1)


def kernel(x, positions, norm_mix_g, norm_mlp_g, a_w_in, a_ln_v_g, a_ln_v_b, a_w_s, a_b_s, a_w_out,
           b_w_q_a, b_q_norm_g, b_w_q_b, b_w_o, kv_src_norm_g, kv_w_a, kv_a_norm_g, kv_w_b,
           mlp_w1, mlp_w2, final_norm_g):
    w_in = a_w_in[0].astype(BF16)
    w_out = a_w_out[0].astype(BF16)
    w1 = mlp_w1.astype(BF16)
    w2 = mlp_w2.astype(BF16)
    wkva = jnp.pad(kv_w_a, ((0, 0), (0, LANES - QK_ROPE))).astype(BF16)
    kvb = kv_w_b.reshape(KV_LORA, B_HEADS, QK_NOPE + V_HEAD)
    wk = kvb[:, :, :QK_NOPE].reshape(KV_LORA, B_HEADS * QK_NOPE).astype(BF16)
    wvt = kvb[:, :, QK_NOPE:].reshape(KV_LORA, B_HEADS * V_HEAD).T.astype(BF16)
    wqa = b_w_q_a[0].astype(BF16)
    wqb = jnp.pad(b_w_q_b[0].reshape(Q_LORA, B_HEADS, QK_NOPE + QK_ROPE),
                  ((0, 0), (0, 0), (0, QK_PAD - QK_NOPE - QK_ROPE)))
    wqb = wqb.reshape(Q_LORA, B_HEADS * QK_PAD).astype(BF16)
    wo = b_w_o[0].astype(BF16)

    half = QK_ROPE // 2
    inv_freq = ROPE_THETA ** (-jnp.arange(0, QK_ROPE, 2, dtype=F32) / QK_ROPE)
    rope_tab = jnp.zeros((8, LANES), F32)
    rope_tab = rope_tab.at[0, :half].set(inv_freq).at[0, half:QK_ROPE].set(inv_freq)
    rope_tab = rope_tab.at[1, :half].set(-1.0).at[1, half:QK_ROPE].set(1.0)

    h = x.reshape(TOKENS, D_MODEL)
    h = _gmlp(h, _row(norm_mix_g[0]), w_in, _row(a_ln_v_g[0]), _row(a_ln_v_b[0]), a_w_s[0],
              a_b_s[0].reshape(A_GROUPS, GMLP_BLOCK, 1), w_out)
    h = _mlp(h, _row(norm_mlp_g[0]), w1[0], w2[0], _row(final_norm_g), False)

    h3 = h.reshape(BATCH, SEQ, D_MODEL)
    q, kn, kpe, vt = _proj(h3, positions.reshape(BATCH, SEQ, 1), rope_tab, _row(kv_src_norm_g),
                           _row(norm_mix_g[1]), wkva, _row(kv_a_norm_g), wk, wvt, wqa,
                           _row(b_q_norm_g[0]), wqb)
    h3 = _attn(h3, q, kn, kpe, vt, wo)

    h = _mlp(h3.reshape(TOKENS, D_MODEL), _row(norm_mlp_g[1]), w1[1], w2[1], _row(final_norm_g), True)
    return h.reshape(BATCH, SEQ, D_MODEL)
```

```python
import functools

import jax
import jax.numpy as jnp
import numpy as np
from jax import lax
from jax.experimental import pallas as pl
from jax.experimental.pallas import tpu as pltpu

D_MODEL = 1024
BATCH = 8
SEQ = 2048
TOKENS = BATCH * SEQ
CHUNK = 64
CHUNK_SHIFT = 6
GMLP_BLOCK = 128
GATE_DIM = 2 * D_MODEL
A_GROUPS = 8
A_GROUP_DIM = GATE_DIM // A_GROUPS
B_HEADS = 8
QK_NOPE = 128
QK_ROPE = 64
V_HEAD = 128
Q_LORA = 384
KV_LORA = 256
ROPE_THETA = 10000.0
D_FF = 4 * D_MODEL
EPS = 1e-6

LANES = 128
QK_PAD = 2 * LANES
VMEM_LIMIT = 56 * 1024 * 1024

TM = 512
NCHUNK = 512
TQ = 256
TK = 256
ONES_ROWS = 16

F32 = jnp.float32
BF16 = jnp.bfloat16
NEG = float(np.finfo(np.float32).min)
SQRT_HALF = float(np.sqrt(0.5))
LOG2_E = float(np.log2(np.e))
HALF_PI = float(np.pi / 2)
NT_DIMS = (((1,), (1,)), ((), ()))
TN_DIMS = (((0,), (0,)), ((), ()))


def _rms(x, g):
    ms = jnp.mean(x * x, axis=-1, keepdims=True)
    return x * lax.rsqrt(ms + EPS) * g


def _resident(shape):
    nd = len(shape)
    return pl.BlockSpec(shape, lambda *_: (0,) * nd, pipeline_mode=pl.Buffered(1))


def _params(*sem):
    return pltpu.CompilerParams(dimension_semantics=sem, vmem_limit_bytes=VMEM_LIMIT)


def _gelu(z):
    return 0.5 * z * (1.0 + lax.erf(z * SQRT_HALF))


def _gmlp_kernel(h_ref, g_ref, w_in_ref, lng_ref, lnb_ref, ws_ref, bs_ref, w_out_ref,
                 o_ref, v_scr, vn_scr, gated_scr, ws_scr):
    h = h_ref[...]
    hn = _rms(h, g_ref[...]).astype(BF16)
    n_half = GATE_DIM // NCHUNK
    row_blocks = [slice(r * GMLP_BLOCK, (r + 1) * GMLP_BLOCK) for r in range(TM // GMLP_BLOCK)]

    for c in range(n_half):
        w_cols = slice(GATE_DIM + c * NCHUNK, GATE_DIM + (c + 1) * NCHUNK)
        v_scr[:, c * NCHUNK:(c + 1) * NCHUNK] = _gelu(
            jnp.dot(hn, w_in_ref[:, w_cols], preferred_element_type=F32))

    ii = lax.broadcasted_iota(jnp.int32, (GMLP_BLOCK, GMLP_BLOCK), 0) >> CHUNK_SHIFT
    jj = lax.broadcasted_iota(jnp.int32, (GMLP_BLOCK, GMLP_BLOCK), 1) >> CHUNK_SHIFT
    for g in range(A_GROUPS):
        ws_scr[g] = jnp.where(jj <= ii, ws_ref[g], 0.0).astype(BF16)

    for rows in row_blocks:
        v = v_scr[rows, :]
        mu = jnp.mean(v, axis=-1, keepdims=True)
        var = jnp.mean(jnp.square(v - mu), axis=-1, keepdims=True)
        vn_scr[rows, :] = ((v - mu) * lax.rsqrt(var + EPS) * lng_ref[...] + lnb_ref[...]).astype(BF16)

    for c in range(n_half):
        u = _gelu(jnp.dot(hn, w_in_ref[:, c * NCHUNK:(c + 1) * NCHUNK], preferred_element_type=F32))
        for gl in range(NCHUNK // A_GROUP_DIM):
            g = c * (NCHUNK // A_GROUP_DIM) + gl
            cols = slice(g * A_GROUP_DIM, (g + 1) * A_GROUP_DIM)
            for rows in row_blocks:
                sv = jnp.dot(ws_scr[g], vn_scr[rows, cols], preferred_element_type=F32) + bs_ref[g]
                u_blk = u[rows, gl * A_GROUP_DIM:(gl + 1) * A_GROUP_DIM]
                gated_scr[rows, cols] = (u_blk * sv).astype(BF16)

    o_ref[...] = h + jnp.dot(gated_scr[...], w_out_ref[...], preferred_element_type=F32)


def _gmlp(h, g, w_in, lng, lnb, ws, bs, w_out):
    tok = pl.BlockSpec((TM, D_MODEL), lambda t: (t, 0))
    return pl.pallas_call(
        _gmlp_kernel,
        grid=(TOKENS // TM,),
        in_specs=[tok, _resident((1, D_MODEL)), _resident((D_MODEL, 2 * GATE_DIM)),
                  _resident((1, GATE_DIM)), _resident((1, GATE_DIM)),
                  _resident((A_GROUPS, GMLP_BLOCK, GMLP_BLOCK)),
                  _resident((A_GROUPS, GMLP_BLOCK, 1)), _resident((GATE_DIM, D_MODEL))],
        out_specs=tok,
        out_shape=jax.ShapeDtypeStruct((TOKENS, D_MODEL), F32),
        scratch_shapes=[pltpu.VMEM((TM, GATE_DIM), F32), pltpu.VMEM((TM, GATE_DIM), BF16),
                        pltpu.VMEM((TM, GATE_DIM), BF16),
                        pltpu.VMEM((A_GROUPS, GMLP_BLOCK, GMLP_BLOCK), BF16)],
        compiler_params=_params("arbitrary"),
        name="gmlp_mixer",
    )(h, g, w_in, lng, lnb, ws, bs, w_out)


def _mlp_kernel(h_ref, g_ref, w1_ref, w2_ref, gf_ref, o_ref, a_scr, *, final_norm):
    h = h_ref[...]
    hn = _rms(h, g_ref[...]).astype(BF16)
    for c in range(D_FF // NCHUNK):
        cols = slice(c * NCHUNK, (c + 1) * NCHUNK)
        a = jnp.maximum(jnp.dot(hn, w1_ref[:, cols], preferred_element_type=F32), 0.0)
        a_scr[:, cols] = (a * a).astype(BF16)
    out = h + jnp.dot(a_scr[...], w2_ref[...], preferred_element_type=F32)
    if final_norm:
        out = _rms(out, gf_ref[...])
    o_ref[...] = out


def _mlp(h, g, w1, w2, gf, final_norm):
    tok = pl.BlockSpec((TM, D_MODEL), lambda t: (t, 0))
    return pl.pallas_call(
        functools.partial(_mlp_kernel, final_norm=final_norm),
        grid=(TOKENS // TM,),
        in_specs=[tok, _resident((1, D_MODEL)), _resident((D_MODEL, D_FF)),
                  _resident((D_FF, D_MODEL)), _resident((1, D_MODEL))],
        out_specs=tok,
        out_shape=jax.ShapeDtypeStruct((TOKENS, D_MODEL), F32),
        scratch_shapes=[pltpu.VMEM((TM, D_FF), BF16)],
        compiler_params=_params("arbitrary"),
        name="mlp_final" if final_norm else "mlp",
    )(h, g, w1, w2, gf)


def _rope(x, cos, sin_signed):
    return x * cos + pltpu.roll(x, LANES // 2, 1) * sin_signed


def _proj_kernel(h_ref, pos_ref, rope_ref, gkv_ref, gq_ref, wkva_ref, gkva_ref, wk_ref, wvt_ref,
                 wqa_ref, gqa_ref, wqb_ref, q_ref, kn_ref, kpe_ref, vt_ref):
    h = h_ref[0]
    y = h * lax.rsqrt(jnp.mean(h * h, axis=-1, keepdims=True) + EPS)

    sin_cos = jnp.sin(pos_ref[0].astype(F32) * rope_ref[0:1, :] + rope_ref[1:2, :])
    cos = pltpu.roll(sin_cos, LANES - QK_ROPE // 2, 1)
    sin_signed = sin_cos * rope_ref[2:3, :]

    ckv = jnp.dot((y * gkv_ref[...]).astype(BF16), wkva_ref[...], preferred_element_type=F32)
    c_kv = _rms(ckv[:, :KV_LORA], gkva_ref[...]).astype(BF16)
    kpe_ref[0] = _rope(ckv[:, KV_LORA:], cos, sin_signed).astype(BF16)
    kn = jnp.dot(c_kv, wk_ref[...], preferred_element_type=F32).astype(BF16)
    for hd in range(B_HEADS):
        kn_ref[0, hd] = kn[:, hd * QK_NOPE:(hd + 1) * QK_NOPE]
    vt = lax.dot_general(wvt_ref[...], c_kv, NT_DIMS, preferred_element_type=F32).astype(BF16)
    for kb in range(TM // TK):
        vt_ref[0, kb] = vt[:, kb * TK:(kb + 1) * TK]

    scale = (QK_NOPE + QK_ROPE) ** -0.5 * LOG2_E
    cq = jnp.dot((y * gq_ref[...]).astype(BF16), wqa_ref[...], preferred_element_type=F32)
    cq = _rms(cq, gqa_ref[...] * scale).astype(BF16)
    for hd in range(B_HEADS):
        q = jnp.dot(cq, wqb_ref[:, hd * QK_PAD:(hd + 1) * QK_PAD], preferred_element_type=F32)
        q_ref[0, hd, :, :QK_NOPE] = q[:, :QK_NOPE].astype(BF16)
        q_ref[0, hd, :, QK_NOPE:] = _rope(q[:, QK_NOPE:], cos, sin_signed).astype(BF16)


def _proj(h, pos, rope_tab, gkv, gq, wkva, gkva, wk, wvt, wqa, gqa, wqb):
    n_t = SEQ // TM
    return pl.pallas_call(
        _proj_kernel,
        grid=(BATCH, n_t),
        in_specs=[pl.BlockSpec((1, TM, D_MODEL), lambda b, t: (b, t, 0)),
                  pl.BlockSpec((1, TM, 1), lambda b, t: (b, t, 0)),
                  _resident((8, LANES)), _resident((1, D_MODEL)), _resident((1, D_MODEL)),
                  _resident((D_MODEL, KV_LORA + LANES)), _resident((1, KV_LORA)),
                  _resident((KV_LORA, B_HEADS * QK_NOPE)), _resident((B_HEADS * V_HEAD, KV_LORA)),
                  _resident((D_MODEL, Q_LORA)), _resident((1, Q_LORA)),
                  _resident((Q_LORA, B_HEADS * QK_PAD))],
        out_specs=[pl.BlockSpec((1, B_HEADS, TM, QK_PAD), lambda b, t: (b, 0, t, 0)),
                   pl.BlockSpec((1, B_HEADS, TM, QK_NOPE), lambda b, t: (b, 0, t, 0)),
                   pl.BlockSpec((1, TM, LANES), lambda b, t: (b, t, 0)),
                   pl.BlockSpec((1, TM // TK, B_HEADS * V_HEAD, TK), lambda b, t: (b, t, 0, 0))],
        out_shape=[jax.ShapeDtypeStruct((BATCH, B_HEADS, SEQ, QK_PAD), BF16),
                   jax.ShapeDtypeStruct((BATCH, B_HEADS, SEQ, QK_NOPE), BF16),
                   jax.ShapeDtypeStruct((BATCH, SEQ, LANES), BF16),
                   jax.ShapeDtypeStruct((BATCH, SEQ // TK, B_HEADS * V_HEAD, TK), BF16)],
        compiler_params=_params("arbitrary", "arbitrary"),
        name="qkv_proj",
    )(h, pos, rope_tab, gkv, gq, wkva, gkva, wk, wvt, wqa, gqa, wqb)


def _attn_kernel(h_ref, q_ref, kn_ref, kpe_ref, vt_ref, wo_ref, o_ref, sa_scr, sb_scr, mta_scr, mtb_scr,
                 m_scr, acc_scr, ot_scr):
    i = pl.program_id(1)
    kc_id = lax.broadcasted_iota(jnp.int32, (TK, TQ), 0) >> CHUNK_SHIFT
    qc_id = lax.broadcasted_iota(jnp.int32, (TK, TQ), 1) >> CHUNK_SHIFT
    diag_ok = kc_id <= qc_id
    ones_rows = jnp.ones((ONES_ROWS, TK), BF16)

    def score_stage(j, s_scr, mt_scr, diagonal):
        rows = pl.ds(pl.multiple_of(j * TK, TK), TK)
        k_pe = kpe_ref[0, rows, :]
        for hd in range(B_HEADS):
            k_cat = jnp.concatenate([kn_ref[0, hd, rows, :], k_pe], axis=-1)
            s = lax.dot_general(k_cat, q_ref[0, hd], NT_DIMS, preferred_element_type=F32)
            if diagonal:
                s = jnp.where(diag_ok, s, NEG)
            s_scr[hd] = s
            mt_scr[hd] = jnp.max(s, axis=0, keepdims=True)

    def softmax_stage(j, s_scr, mt_scr, first):
        for hd in range(B_HEADS):
            v_aug = jnp.concatenate([vt_ref[0, j, hd * V_HEAD:(hd + 1) * V_HEAD, :], ones_rows], axis=0)
            if first:
                m_new = mt_scr[hd]
                p = jnp.exp2(s_scr[hd] - m_new)
                acc_scr[hd] = jnp.dot(v_aug, p.astype(BF16), preferred_element_type=F32)
            else:
                m = m_scr[hd]
                m_new = jnp.maximum(m, mt_scr[hd])
                alpha = jnp.exp2(m - m_new)
                p = jnp.exp2(s_scr[hd] - m_new)
                acc_scr[hd] = alpha * acc_scr[hd] + jnp.dot(v_aug, p.astype(BF16),
                                                            preferred_element_type=F32)
            m_scr[hd] = m_new

    score_stage(i, sb_scr, mtb_scr, True)
    score_stage(0, sa_scr, mta_scr, False)
    softmax_stage(i, sb_scr, mtb_scr, True)

    def tile_pair(u, _):
        j = 2 * u
        score_stage(j + 1, sb_scr, mtb_scr, False)
        softmax_stage(j, sa_scr, mta_scr, False)
        score_stage(j + 2, sa_scr, mta_scr, False)
        softmax_stage(j + 1, sb_scr, mtb_scr, False)
        return 0

    lax.fori_loop(0, jnp.maximum(i - 1, 0) >> 1, tile_pair, 0)

    @pl.when((i & 1) == 1)
    def _():
        softmax_stage(i - 1, sa_scr, mta_scr, False)

    @pl.when(jnp.logical_and((i & 1) == 0, i > 0))
    def _():
        score_stage(i - 1, sb_scr, mtb_scr, False)
        softmax_stage(i - 2, sa_scr, mta_scr, False)
        softmax_stage(i - 1, sb_scr, mtb_scr, False)

    for hd in range(B_HEADS):
        inv_l = 1.0 / acc_scr[hd, V_HEAD:V_HEAD + 1, :]
        ot_scr[hd * V_HEAD:(hd + 1) * V_HEAD, :] = (acc_scr[hd, :V_HEAD, :] * inv_l).astype(BF16)
    out = lax.dot_general(ot_scr[...], wo_ref[...], TN_DIMS, preferred_element_type=F32)
    o_ref[0] = h_ref[0] + out


def _attn(h, q, kn, kpe, vt, wo):
    return pl.pallas_call(
        _attn_kernel,
        grid=(BATCH, SEQ // TQ),
        in_specs=[pl.BlockSpec((1, TQ, D_MODEL), lambda b, i: (b, i, 0)),
                  pl.BlockSpec((1, B_HEADS, TQ, QK_PAD), lambda b, i: (b, 0, i, 0)),
                  pl.BlockSpec((1, B_HEADS, SEQ, QK_NOPE), lambda b, i: (b, 0, 0, 0)),
                  pl.BlockSpec((1, SEQ, LANES), lambda b, i: (b, 0, 0)),
                  pl.BlockSpec((1, SEQ // TK, B_HEADS * V_HEAD, TK), lambda b, i: (b, 0, 0, 0)),
                  _resident((B_HEADS * V_HEAD, D_MODEL))],
        out_specs=pl.BlockSpec((1, TQ, D_MODEL), lambda b, i: (b, i, 0)),
        out_shape=jax.ShapeDtypeStruct((BATCH, SEQ, D_MODEL), F32),
        scratch_shapes=[pltpu.VMEM((B_HEADS, TK, TQ), F32), pltpu.VMEM((B_HEADS, TK, TQ), F32),
                        pltpu.VMEM((B_HEADS, 1, TQ), F32), pltpu.VMEM((B_HEADS, 1, TQ), F32),
                        pltpu.VMEM((B_HEADS, 1, TQ), F32),
                        pltpu.VMEM((B_HEADS, V_HEAD + ONES_ROWS, TQ), F32),
                        pltpu.VMEM((B_HEADS * V_HEAD, TQ), BF16)],
        compiler_params=_params("arbitrary", "arbitrary"),
        name="mla_attention",
    )(h, q, kn, kpe, vt, wo)


def _row(v):
    return v.reshape(1, -1)


def _spread_rope(w):
    half = QK_ROPE // 2
    zeros = jnp.zeros(w.shape[:-1] + (half,), w.dtype)
    return jnp.concatenate([w[..., :half], zeros, w[..., half:], zeros], axis=-1)


def kernel(x, positions, norm_mix_g, norm_mlp_g, a_w_in, a_ln_v_g, a_ln_v_b, a_w_s, a_b_s, a_w_out,
           b_w_q_a, b_q_norm_g, b_w_q_b, b_w_o, kv_src_norm_g, kv_w_a, kv_a_norm_g, kv_w_b,
           mlp_w1, mlp_w2, final_norm_g):
    w_in = a_w_in[0].astype(BF16)
    w_out = a_w_out[0].astype(BF16)
    w1 = mlp_w1.astype(BF16)
    w2 = mlp_w2.astype(BF16)
    wkva = jnp.concatenate([kv_w_a[:, :KV_LORA], _spread_rope(kv_w_a[:, KV_LORA:])], axis=-1).astype(BF16)
    kvb = kv_w_b.reshape(KV_LORA, B_HEADS, QK_NOPE + V_HEAD)
    wk = kvb[:, :, :QK_NOPE].reshape(KV_LORA, B_HEADS * QK_NOPE).astype(BF16)
    wvt = kvb[:, :, QK_NOPE:].reshape(KV_LORA, B_HEADS * V_HEAD).T.astype(BF16)
    wqa = b_w_q_a[0].astype(BF16)
    wqb = b_w_q_b[0].reshape(Q_LORA, B_HEADS, QK_NOPE + QK_ROPE)
    wqb = jnp.concatenate([wqb[:, :, :QK_NOPE], _spread_rope(wqb[:, :, QK_NOPE:])], axis=-1)
    wqb = wqb.reshape(Q_LORA, B_HEADS * QK_PAD).astype(BF16)
    wo = b_w_o[0].astype(BF16)

    inv_freq = ROPE_THETA ** (-jnp.arange(0, QK_ROPE, 2, dtype=F32) / QK_ROPE)
    group = jnp.ones((QK_ROPE // 2,), F32)
    rope_tab = jnp.zeros((8, LANES), F32)
    rope_tab = rope_tab.at[0].set(jnp.tile(inv_freq, 4))
    rope_tab = rope_tab.at[1].set(jnp.concatenate([0.0 * group, HALF_PI * group, 0.0 * group, HALF_PI * group]))
    rope_tab = rope_tab.at[2].set(jnp.concatenate([-group, 0.0 * group, group, 0.0 * group]))

    h = x.reshape(TOKENS, D_MODEL)
    h = _gmlp(h, _row(norm_mix_g[0]), w_in, _row(a_ln_v_g[0]), _row(a_ln_v_b[0]), a_w_s[0],
              a_b_s[0].reshape(A_GROUPS, GMLP_BLOCK, 1), w_out)
    h = _mlp(h, _row(norm_mlp_g[0]), w1[0], w2[0], _row(final_norm_g), False)

    h3 = h.reshape(BATCH, SEQ, D_MODEL)
    q, kn, kpe, vt = _proj(h3, positions.reshape(BATCH, SEQ, 1), rope_tab, _row(kv_src_norm_g),
                           _row(norm_mix_g[1]), wkva, _row(kv_a_norm_g), wk, wvt, wqa,
                           _row(b_q_norm_g[0]), wqb)
    h3 = _attn(h3, q, kn, kpe, vt, wo)

    h = _mlp(h3.reshape(TOKENS, D_MODEL), _row(norm_mlp_g[1]), w1[1], w2[1], _row(final_norm_g), True)
    return h.reshape(BATCH, SEQ, D_MODEL)
```

```python
import functools

import jax
import jax.numpy as jnp
import numpy as np
from jax import lax
from jax.experimental import pallas as pl
from jax.experimental.pallas import tpu as pltpu

D_MODEL = 1024
BATCH = 8
SEQ = 2048
TOKENS = BATCH * SEQ
DEPTH = 2
CHUNK = 64
CHUNK_SHIFT = 6
GMLP_BLOCK = 128
GATE_DIM = 2 * D_MODEL
A_GROUPS = 8
A_GROUP_DIM = GATE_DIM // A_GROUPS
B_HEADS = 8
QK_NOPE = 128
QK_ROPE = 64
V_HEAD = 128
Q_LORA = 384
KV_LORA = 256
ROPE_THETA = 10000.0
D_FF = 4 * D_MODEL
EPS = 1e-6

LANES = 128
QK_PAD = 2 * LANES
VMEM_LIMIT = 56 * 1024 * 1024

TM = 512
NCHUNK = 512
TQ = 256
TK = 256
ONES_ROWS = 16

F32 = jnp.float32
BF16 = jnp.bfloat16
NEG = float(np.finfo(np.float32).min)
SQRT_HALF = float(np.sqrt(0.5))
LOG2_E = float(np.log2(np.e))
HALF_PI = float(np.pi / 2)
NT_DIMS = (((1,), (1,)), ((), ()))
TN_DIMS = (((0,), (0,)), ((), ()))


def _rms(x, g):
    ms = jnp.mean(x * x, axis=-1, keepdims=True)
    return x * lax.rsqrt(ms + EPS) * g


def _resident(shape):
    nd = len(shape)
    return pl.BlockSpec(shape, lambda *_: (0,) * nd, pipeline_mode=pl.Buffered(1))


def _params(*sem):
    return pltpu.CompilerParams(dimension_semantics=sem, vmem_limit_bytes=VMEM_LIMIT)


def _gelu(z):
    return 0.5 * z * (1.0 + lax.erf(z * SQRT_HALF))


def _gmlp_kernel(h_ref, g_ref, w_in_ref, lng_ref, lnb_ref, ws_ref, bs_ref, w_out_ref, w1f_ref, w2f_ref,
                 o_ref, w1b_ref, w2b_ref, v_scr, vn_scr, gated_scr, ws_scr):
    w1b_ref[...] = w1f_ref[...].astype(BF16)
    w2b_ref[...] = w2f_ref[...].astype(BF16)

    h = h_ref[...]
    hn = _rms(h, g_ref[...]).astype(BF16)
    n_half = GATE_DIM // NCHUNK
    row_blocks = [slice(r * GMLP_BLOCK, (r + 1) * GMLP_BLOCK) for r in range(TM // GMLP_BLOCK)]

    for c in range(n_half):
        w_cols = slice(GATE_DIM + c * NCHUNK, GATE_DIM + (c + 1) * NCHUNK)
        v_scr[:, c * NCHUNK:(c + 1) * NCHUNK] = _gelu(
            jnp.dot(hn, w_in_ref[:, w_cols], preferred_element_type=F32))

    ii = lax.broadcasted_iota(jnp.int32, (GMLP_BLOCK, GMLP_BLOCK), 0) >> CHUNK_SHIFT
    jj = lax.broadcasted_iota(jnp.int32, (GMLP_BLOCK, GMLP_BLOCK), 1) >> CHUNK_SHIFT
    for g in range(A_GROUPS):
        ws_scr[g] = jnp.where(jj <= ii, ws_ref[g], 0.0).astype(BF16)

    for rows in row_blocks:
        v = v_scr[rows, :]
        mu = jnp.mean(v, axis=-1, keepdims=True)
        var = jnp.mean(jnp.square(v - mu), axis=-1, keepdims=True)
        vn_scr[rows, :] = ((v - mu) * lax.rsqrt(var + EPS) * lng_ref[...] + lnb_ref[...]).astype(BF16)

    for c in range(n_half):
        u = _gelu(jnp.dot(hn, w_in_ref[:, c * NCHUNK:(c + 1) * NCHUNK], preferred_element_type=F32))
        for gl in range(NCHUNK // A_GROUP_DIM):
            g = c * (NCHUNK // A_GROUP_DIM) + gl
            cols = slice(g * A_GROUP_DIM, (g + 1) * A_GROUP_DIM)
            for rows in row_blocks:
                sv = jnp.dot(ws_scr[g], vn_scr[rows, cols], preferred_element_type=F32) + bs_ref[g]
                u_blk = u[rows, gl * A_GROUP_DIM:(gl + 1) * A_GROUP_DIM]
                gated_scr[rows, cols] = (u_blk * sv).astype(BF16)

    o_ref[...] = h + jnp.dot(gated_scr[...], w_out_ref[...], preferred_element_type=F32)


def _gmlp(h, g, w_in, lng, lnb, ws, bs, w_out, w1f, w2f):
    steps = TOKENS // TM
    tok = pl.BlockSpec((TM, D_MODEL), lambda t: (t, 0))
    w1_slab = pl.BlockSpec((DEPTH, D_MODEL // steps, D_FF), lambda t: (0, t, 0))
    w2_slab = pl.BlockSpec((DEPTH, D_FF // steps, D_MODEL), lambda t: (0, t, 0))
    return pl.pallas_call(
        _gmlp_kernel,
        grid=(steps,),
        in_specs=[tok, _resident((1, D_MODEL)), _resident((D_MODEL, 2 * GATE_DIM)),
                  _resident((1, GATE_DIM)), _resident((1, GATE_DIM)),
                  _resident((A_GROUPS, GMLP_BLOCK, GMLP_BLOCK)),
                  _resident((A_GROUPS, GMLP_BLOCK, 1)), _resident((GATE_DIM, D_MODEL)),
                  w1_slab, w2_slab],
        out_specs=[tok, w1_slab, w2_slab],
        out_shape=[jax.ShapeDtypeStruct((TOKENS, D_MODEL), F32),
                   jax.ShapeDtypeStruct((DEPTH, D_MODEL, D_FF), BF16),
                   jax.ShapeDtypeStruct((DEPTH, D_FF, D_MODEL), BF16)],
        scratch_shapes=[pltpu.VMEM((TM, GATE_DIM), F32), pltpu.VMEM((TM, GATE_DIM), BF16),
                        pltpu.VMEM((TM, GATE_DIM), BF16),
                        pltpu.VMEM((A_GROUPS, GMLP_BLOCK, GMLP_BLOCK), BF16)],
        compiler_params=_params("arbitrary"),
        name="gmlp_mixer",
    )(h, g, w_in, lng, lnb, ws, bs, w_out, w1f, w2f)


def _mlp_residual(h, g_ref, w1_ref, w2_ref, a_scr):
    hn = _rms(h, g_ref[...]).astype(BF16)
    for c in range(D_FF // NCHUNK):
        cols = slice(c * NCHUNK, (c + 1) * NCHUNK)
        a = jnp.maximum(jnp.dot(hn, w1_ref[:, cols], preferred_element_type=F32), 0.0)
        a_scr[:, cols] = (a * a).astype(BF16)
    return h + jnp.dot(a_scr[...], w2_ref[...], preferred_element_type=F32)


def _mlp_rope_kernel(h_ref, g_ref, w1_ref, w2_ref, pos_ref, rope_ref, o_ref, cos_ref, sin_ref, a_scr):
    sin_cos = jnp.sin(pos_ref[...].astype(F32) * rope_ref[0:1, :] + rope_ref[1:2, :])
    cos_ref[...] = pltpu.roll(sin_cos, LANES - QK_ROPE // 2, 1)
    sin_ref[...] = sin_cos * rope_ref[2:3, :]
    o_ref[...] = _mlp_residual(h_ref[...], g_ref, w1_ref, w2_ref, a_scr)


def _mlp_final_kernel(h_ref, g_ref, w1_ref, w2_ref, gf_ref, o_ref, a_scr):
    o_ref[...] = _rms(_mlp_residual(h_ref[...], g_ref, w1_ref, w2_ref, a_scr), gf_ref[...])


def _mlp_specs(layer):
    tok = pl.BlockSpec((TM, D_MODEL), lambda t: (t, 0))
    w1 = pl.BlockSpec((None, D_MODEL, D_FF), lambda t: (layer, 0, 0), pipeline_mode=pl.Buffered(1))
    w2 = pl.BlockSpec((None, D_FF, D_MODEL), lambda t: (layer, 0, 0), pipeline_mode=pl.Buffered(1))
    return tok, [tok, _resident((1, D_MODEL)), w1, w2]


def _mlp_rope(h, g, w1, w2, pos, rope_tab, layer):
    tok, in_specs = _mlp_specs(layer)
    tab = pl.BlockSpec((TM, LANES), lambda t: (t, 0))
    return pl.pallas_call(
        _mlp_rope_kernel,
        grid=(TOKENS // TM,),
        in_specs=in_specs + [pl.BlockSpec((TM, 1), lambda t: (t, 0)), _resident((8, LANES))],
        out_specs=[tok, tab, tab],
        out_shape=[jax.ShapeDtypeStruct((TOKENS, D_MODEL), F32),
                   jax.ShapeDtypeStruct((TOKENS, LANES), F32),
                   jax.ShapeDtypeStruct((TOKENS, LANES), F32)],
        scratch_shapes=[pltpu.VMEM((TM, D_FF), BF16)],
        compiler_params=_params("arbitrary"),
        name="mlp_rope",
    )(h, g, w1, w2, pos, rope_tab)


def _mlp_final(h, g, w1, w2, gf, layer):
    tok, in_specs = _mlp_specs(layer)
    return pl.pallas_call(
        _mlp_final_kernel,
        grid=(TOKENS // TM,),
        in_specs=in_specs + [_resident((1, D_MODEL))],
        out_specs=tok,
        out_shape=jax.ShapeDtypeStruct((TOKENS, D_MODEL), F32),
        scratch_shapes=[pltpu.VMEM((TM, D_FF), BF16)],
        compiler_params=_params("arbitrary"),
        name="mlp_final",
    )(h, g, w1, w2, gf)


def _rope(x, cos, sin_signed):
    return x * cos + pltpu.roll(x, LANES // 2, 1) * sin_signed


def _proj_kernel(h_ref, cos_ref, sin_ref, gkv_ref, gq_ref, wkva_ref, gkva_ref, wk_ref, wvt_ref,
                 wqa_ref, gqa_ref, wqb_ref, q_ref, kn_ref, kpe_ref, vt_ref):
    h = h_ref[0]
    y = h * lax.rsqrt(jnp.mean(h * h, axis=-1, keepdims=True) + EPS)
    cos = cos_ref[0]
    sin_signed = sin_ref[0]

    ckv = jnp.dot((y * gkv_ref[...]).astype(BF16), wkva_ref[...], preferred_element_type=F32)
    c_kv = _rms(ckv[:, :KV_LORA], gkva_ref[...]).astype(BF16)
    kpe_ref[0] = _rope(ckv[:, KV_LORA:], cos, sin_signed).astype(BF16)
    kn = jnp.dot(c_kv, wk_ref[...], preferred_element_type=F32).astype(BF16)
    for hd in range(B_HEADS):
        kn_ref[0, hd] = kn[:, hd * QK_NOPE:(hd + 1) * QK_NOPE]
    vt = lax.dot_general(wvt_ref[...], c_kv, NT_DIMS, preferred_element_type=F32).astype(BF16)
    for kb in range(TM // TK):
        vt_ref[0, kb] = vt[:, kb * TK:(kb + 1) * TK]

    scale = (QK_NOPE + QK_ROPE) ** -0.5 * LOG2_E
    cq = jnp.dot((y * gq_ref[...]).astype(BF16), wqa_ref[...], preferred_element_type=F32)
    cq = _rms(cq, gqa_ref[...] * scale).astype(BF16)
    for hd in range(B_HEADS):
        q = jnp.dot(cq, wqb_ref[:, hd * QK_PAD:(hd + 1) * QK_PAD], preferred_element_type=F32)
        q_ref[0, hd, :, :QK_NOPE] = q[:, :QK_NOPE].astype(BF16)
        q_ref[0, hd, :, QK_NOPE:] = _rope(q[:, QK_NOPE:], cos, sin_signed).astype(BF16)


def _proj(h, cos_tab, sin_tab, gkv, gq, wkva, gkva, wk, wvt, wqa, gqa, wqb):
    n_t = SEQ // TM
    tab = pl.BlockSpec((1, TM, LANES), lambda b, t: (b, t, 0))
    return pl.pallas_call(
        _proj_kernel,
        grid=(BATCH, n_t),
        in_specs=[pl.BlockSpec((1, TM, D_MODEL), lambda b, t: (b, t, 0)), tab, tab,
                  _resident((1, D_MODEL)), _resident((1, D_MODEL)),
                  _resident((D_MODEL, KV_LORA + LANES)), _resident((1, KV_LORA)),
                  _resident((KV_LORA, B_HEADS * QK_NOPE)), _resident((B_HEADS * V_HEAD, KV_LORA)),
                  _resident((D_MODEL, Q_LORA)), _resident((1, Q_LORA)),
                  _resident((Q_LORA, B_HEADS * QK_PAD))],
        out_specs=[pl.BlockSpec((1, B_HEADS, TM, QK_PAD), lambda b, t: (b, 0, t, 0)),
                   pl.BlockSpec((1, B_HEADS, TM, QK_NOPE), lambda b, t: (b, 0, t, 0)),
                   pl.BlockSpec((1, TM, LANES), lambda b, t: (b, t, 0)),
                   pl.BlockSpec((1, TM // TK, B_HEADS * V_HEAD, TK), lambda b, t: (b, t, 0, 0))],
        out_shape=[jax.ShapeDtypeStruct((BATCH, B_HEADS, SEQ, QK_PAD), BF16),
                   jax.ShapeDtypeStruct((BATCH, B_HEADS, SEQ, QK_NOPE), BF16),
                   jax.ShapeDtypeStruct((BATCH, SEQ, LANES), BF16),
                   jax.ShapeDtypeStruct((BATCH, SEQ // TK, B_HEADS * V_HEAD, TK), BF16)],
        compiler_params=_params("arbitrary", "arbitrary"),
        name="qkv_proj",
    )(h, cos_tab, sin_tab, gkv, gq, wkva, gkva, wk, wvt, wqa, gqa, wqb)


def _attn_kernel(h_ref, q_ref, kn_ref, kpe_ref, vt_ref, wo_ref, o_ref, sa_scr, sb_scr, mta_scr, mtb_scr,
                 m_scr, acc_scr, ot_scr):
    i = pl.program_id(1)
    kc_id = lax.broadcasted_iota(jnp.int32, (TK, TQ), 0) >> CHUNK_SHIFT
    qc_id = lax.broadcasted_iota(jnp.int32, (TK, TQ), 1) >> CHUNK_SHIFT
    diag_ok = kc_id <= qc_id
    ones_rows = jnp.ones((ONES_ROWS, TK), BF16)

    def score_stage(j, s_scr, mt_scr, diagonal):
        rows = pl.ds(pl.multiple_of(j * TK, TK), TK)
        k_pe = kpe_ref[0, rows, :]
        for hd in range(B_HEADS):
            k_cat = jnp.concatenate([kn_ref[0, hd, rows, :], k_pe], axis=-1)
            s = lax.dot_general(k_cat, q_ref[0, hd], NT_DIMS, preferred_element_type=F32)
            if diagonal:
                s = jnp.where(diag_ok, s, NEG)
            s_scr[hd] = s
            mt_scr[hd] = jnp.max(s, axis=0, keepdims=True)

    def softmax_stage(j, s_scr, mt_scr, first):
        for hd in range(B_HEADS):
            v_aug = jnp.concatenate([vt_ref[0, j, hd * V_HEAD:(hd + 1) * V_HEAD, :], ones_rows], axis=0)
            if first:
                m_new = mt_scr[hd]
                p = jnp.exp2(s_scr[hd] - m_new)
                acc_scr[hd] = jnp.dot(v_aug, p.astype(BF16), preferred_element_type=F32)
            else:
                m = m_scr[hd]
                m_new = jnp.maximum(m, mt_scr[hd])
                alpha = jnp.exp2(m - m_new)
                p = jnp.exp2(s_scr[hd] - m_new)
                acc_scr[hd] = alpha * acc_scr[hd] + jnp.dot(v_aug, p.astype(BF16),
                                                            preferred_element_type=F32)
            m_scr[hd] = m_new

    score_stage(i, sb_scr, mtb_scr, True)
    score_stage(0, sa_scr, mta_scr, False)
    softmax_stage(i, sb_scr, mtb_scr, True)

    def tile_pair(u, _):
        j = 2 * u
        score_stage(j + 1, sb_scr, mtb_scr, False)
        softmax_stage(j, sa_scr, mta_scr, False)
        score_stage(j + 2, sa_scr, mta_scr, False)
        softmax_stage(j + 1, sb_scr, mtb_scr, False)
        return 0

    lax.fori_loop(0, jnp.maximum(i - 1, 0) >> 1, tile_pair, 0)

    @pl.when((i & 1) == 1)
    def _():
        softmax_stage(i - 1, sa_scr, mta_scr, False)

    @pl.when(jnp.logical_and((i & 1) == 0, i > 0))
    def _():
        score_stage(i - 1, sb_scr, mtb_scr, False)
        softmax_stage(i - 2, sa_scr, mta_scr, False)
        softmax_stage(i - 1, sb_scr, mtb_scr, False)

    for hd in range(B_HEADS):
        inv_l = 1.0 / acc_scr[hd, V_HEAD:V_HEAD + 1, :]
        ot_scr[hd * V_HEAD:(hd + 1) * V_HEAD, :] = (acc_scr[hd, :V_HEAD, :] * inv_l).astype(BF16)
    out = lax.dot_general(ot_scr[...], wo_ref[...], TN_DIMS, preferred_element_type=F32)
    o_ref[0] = h_ref[0] + out


def _attn(h, q, kn, kpe, vt, wo):
    return pl.pallas_call(
        _attn_kernel,
        grid=(BATCH, SEQ // TQ),
        in_specs=[pl.BlockSpec((1, TQ, D_MODEL), lambda b, i: (b, i, 0)),
                  pl.BlockSpec((1, B_HEADS, TQ, QK_PAD), lambda b, i: (b, 0, i, 0)),
                  pl.BlockSpec((1, B_HEADS, SEQ, QK_NOPE), lambda b, i: (b, 0, 0, 0)),
                  pl.BlockSpec((1, SEQ, LANES), lambda b, i: (b, 0, 0)),
                  pl.BlockSpec((1, SEQ // TK, B_HEADS * V_HEAD, TK), lambda b, i: (b, 0, 0, 0)),
                  _resident((B_HEADS * V_HEAD, D_MODEL))],
        out_specs=pl.BlockSpec((1, TQ, D_MODEL), lambda b, i: (b, i, 0)),
        out_shape=jax.ShapeDtypeStruct((BATCH, SEQ, D_MODEL), F32),
        scratch_shapes=[pltpu.VMEM((B_HEADS, TK, TQ), F32), pltpu.VMEM((B_HEADS, TK, TQ), F32),
                        pltpu.VMEM((B_HEADS, 1, TQ), F32), pltpu.VMEM((B_HEADS, 1, TQ), F32),
                        pltpu.VMEM((B_HEADS, 1, TQ), F32),
                        pltpu.VMEM((B_HEADS, V_HEAD + ONES_ROWS, TQ), F32),
                        pltpu.VMEM((B_HEADS * V_HEAD, TQ), BF16)],
        compiler_params=_params("arbitrary", "arbitrary"),
        name="mla_attention",
    )(h, q, kn, kpe, vt, wo)


def _row(v):
    return v.reshape(1, -1)


def _spread_rope(w):
    half = QK_ROPE // 2
    zeros = jnp.zeros(w.shape[:-1] + (half,), w.dtype)
    return jnp.concatenate([w[..., :half], zeros, w[..., half:], zeros], axis=-1)


def kernel(x, positions, norm_mix_g, norm_mlp_g, a_w_in, a_ln_v_g, a_ln_v_b, a_w_s, a_b_s, a_w_out,
           b_w_q_a, b_q_norm_g, b_w_q_b, b_w_o, kv_src_norm_g, kv_w_a, kv_a_norm_g, kv_w_b,
           mlp_w1, mlp_w2, final_norm_g):
    w_in = a_w_in[0].astype(BF16)
    w_out = a_w_out[0].astype(BF16)
    wkva = jnp.concatenate([kv_w_a[:, :KV_LORA], _spread_rope(kv_w_a[:, KV_LORA:])], axis=-1).astype(BF16)
    kvb = kv_w_b.reshape(KV_LORA, B_HEADS, QK_NOPE + V_HEAD)
    wk = kvb[:, :, :QK_NOPE].reshape(KV_LORA, B_HEADS * QK_NOPE).astype(BF16)
    wvt = kvb[:, :, QK_NOPE:].reshape(KV_LORA, B_HEADS * V_HEAD).T.astype(BF16)
    wqa = b_w_q_a[0].astype(BF16)
    wqb = b_w_q_b[0].reshape(Q_LORA, B_HEADS, QK_NOPE + QK_ROPE)
    wqb = jnp.concatenate([wqb[:, :, :QK_NOPE], _spread_rope(wqb[:, :, QK_NOPE:])], axis=-1)
    wqb = wqb.reshape(Q_LORA, B_HEADS * QK_PAD).astype(BF16)
    wo = b_w_o[0].astype(BF16)

    inv_freq = ROPE_THETA ** (-jnp.arange(0, QK_ROPE, 2, dtype=F32) / QK_ROPE)
    group = jnp.ones((QK_ROPE // 2,), F32)
    rope_tab = jnp.zeros((8, LANES), F32)
    rope_tab = rope_tab.at[0].set(jnp.tile(inv_freq, 4))
    rope_tab = rope_tab.at[1].set(jnp.concatenate([0.0 * group, HALF_PI * group, 0.0 * group, HALF_PI * group]))
    rope_tab = rope_tab.at[2].set(jnp.concatenate([-group, 0.0 * group, group, 0.0 * group]))

    h = x.reshape(TOKENS, D_MODEL)
    h, w1, w2 = _gmlp(h, _row(norm_mix_g[0]), w_in, _row(a_ln_v_g[0]), _row(a_ln_v_b[0]), a_w_s[0],
                      a_b_s[0].reshape(A_GROUPS, GMLP_BLOCK, 1), w_out, mlp_w1, mlp_w2)
    h, cos_tab, sin_tab = _mlp_rope(h, _row(norm_mlp_g[0]), w1, w2, positions.reshape(TOKENS, 1),
                                    rope_tab, 0)

    h3 = h.reshape(BATCH, SEQ, D_MODEL)
    q, kn, kpe, vt = _proj(h3, cos_tab.reshape(BATCH, SEQ, LANES), sin_tab.reshape(BATCH, SEQ, LANES),
                           _row(kv_src_norm_g), _row(norm_mix_g[1]), wkva, _row(kv_a_norm_g), wk, wvt,
                           wqa, _row(b_q_norm_g[0]), wqb)
    h3 = _attn(h3, q, kn, kpe, vt, wo)

    h = _mlp_final(h3.reshape(TOKENS, D_MODEL), _row(norm_mlp_g[1]), w1, w2, _row(final_norm_g), 1)
    return h.reshape(BATCH, SEQ, D_MODEL)
```

```python
import functools

import jax
import jax.numpy as jnp
import numpy as np
from jax import lax
from jax.experimental import pallas as pl
from jax.experimental.pallas import tpu as pltpu

D_MODEL = 1024
BATCH = 8
SEQ = 2048
TOKENS = BATCH * SEQ
DEPTH = 2
CHUNK = 64
CHUNK_SHIFT = 6
GMLP_BLOCK = 128
GATE_DIM = 2 * D_MODEL
A_GROUPS = 8
A_GROUP_DIM = GATE_DIM // A_GROUPS
B_HEADS = 8
QK_NOPE = 128
QK_ROPE = 64
V_HEAD = 128
Q_LORA = 384
KV_LORA = 256
ROPE_THETA = 10000.0
D_FF = 4 * D_MODEL
EPS = 1e-6

LANES = 128
QK_PAD = 2 * LANES
VMEM_LIMIT = 56 * 1024 * 1024

TM = 512
TM_MLP = 1024
NCHUNK = 512
TQ = 256
TK = 256
ONES_ROWS = 16

F32 = jnp.float32
BF16 = jnp.bfloat16
NEG = float(np.finfo(np.float32).min)
SQRT_HALF = float(np.sqrt(0.5))
LOG2_E = float(np.log2(np.e))
HALF_PI = float(np.pi / 2)
NT_DIMS = (((1,), (1,)), ((), ()))
TN_DIMS = (((0,), (0,)), ((), ()))


def _rms(x, g):
    ms = jnp.mean(x * x, axis=-1, keepdims=True)
    return x * lax.rsqrt(ms + EPS) * g


def _resident(shape):
    nd = len(shape)
    return pl.BlockSpec(shape, lambda *_: (0,) * nd, pipeline_mode=pl.Buffered(1))


def _params(*sem):
    return pltpu.CompilerParams(dimension_semantics=sem, vmem_limit_bytes=VMEM_LIMIT)


def _gelu(z):
    return 0.5 * z * (1.0 + lax.erf(z * SQRT_HALF))


def _gmlp_kernel(h_ref, g_ref, w_in_ref, lng_ref, lnb_ref, ws_ref, bs_ref, w_out_ref, w1f_ref, w2f_ref,
                 o_ref, w1b_ref, w2b_ref, v_scr, vn_scr, gated_scr, ws_scr):
    w1b_ref[...] = w1f_ref[...].astype(BF16)
    w2b_ref[...] = w2f_ref[...].astype(BF16)

    h = h_ref[...]
    hn = _rms(h, g_ref[...]).astype(BF16)
    n_half = GATE_DIM // NCHUNK
    row_blocks = [slice(r * GMLP_BLOCK, (r + 1) * GMLP_BLOCK) for r in range(TM // GMLP_BLOCK)]

    for c in range(n_half):
        w_cols = slice(GATE_DIM + c * NCHUNK, GATE_DIM + (c + 1) * NCHUNK)
        v_scr[:, c * NCHUNK:(c + 1) * NCHUNK] = _gelu(
            jnp.dot(hn, w_in_ref[:, w_cols], preferred_element_type=F32))

    ii = lax.broadcasted_iota(jnp.int32, (GMLP_BLOCK, GMLP_BLOCK), 0) >> CHUNK_SHIFT
    jj = lax.broadcasted_iota(jnp.int32, (GMLP_BLOCK, GMLP_BLOCK), 1) >> CHUNK_SHIFT
    for g in range(A_GROUPS):
        ws_scr[g] = jnp.where(jj <= ii, ws_ref[g], 0.0).astype(BF16)

    for rows in row_blocks:
        v = v_scr[rows, :]
        mu = jnp.mean(v, axis=-1, keepdims=True)
        var = jnp.mean(jnp.square(v - mu), axis=-1, keepdims=True)
        vn_scr[rows, :] = ((v - mu) * lax.rsqrt(var + EPS) * lng_ref[...] + lnb_ref[...]).astype(BF16)

    for c in range(n_half):
        u = _gelu(jnp.dot(hn, w_in_ref[:, c * NCHUNK:(c + 1) * NCHUNK], preferred_element_type=F32))
        for gl in range(NCHUNK // A_GROUP_DIM):
            g = c * (NCHUNK // A_GROUP_DIM) + gl
            cols = slice(g * A_GROUP_DIM, (g + 1) * A_GROUP_DIM)
            for rows in row_blocks:
                sv = jnp.dot(ws_scr[g], vn_scr[rows, cols], preferred_element_type=F32) + bs_ref[g]
                u_blk = u[rows, gl * A_GROUP_DIM:(gl + 1) * A_GROUP_DIM]
                gated_scr[rows, cols] = (u_blk * sv).astype(BF16)

    o_ref[...] = h + jnp.dot(gated_scr[...], w_out_ref[...], preferred_element_type=F32)


def _gmlp(h, g, w_in, lng, lnb, ws, bs, w_out, w1f, w2f):
    steps = TOKENS // TM
    tok = pl.BlockSpec((TM, D_MODEL), lambda t: (t, 0))
    w1_slab = pl.BlockSpec((DEPTH, D_MODEL // steps, D_FF), lambda t: (0, t, 0))
    w2_slab = pl.BlockSpec((DEPTH, D_FF // steps, D_MODEL), lambda t: (0, t, 0))
    return pl.pallas_call(
        _gmlp_kernel,
        grid=(steps,),
        in_specs=[tok, _resident((1, D_MODEL)), _resident((D_MODEL, 2 * GATE_DIM)),
                  _resident((1, GATE_DIM)), _resident((1, GATE_DIM)),
                  _resident((A_GROUPS, GMLP_BLOCK, GMLP_BLOCK)),
                  _resident((A_GROUPS, GMLP_BLOCK, 1)), _resident((GATE_DIM, D_MODEL)),
                  w1_slab, w2_slab],
        out_specs=[tok, w1_slab, w2_slab],
        out_shape=[jax.ShapeDtypeStruct((TOKENS, D_MODEL), F32),
                   jax.ShapeDtypeStruct((DEPTH, D_MODEL, D_FF), BF16),
                   jax.ShapeDtypeStruct((DEPTH, D_FF, D_MODEL), BF16)],
        scratch_shapes=[pltpu.VMEM((TM, GATE_DIM), F32), pltpu.VMEM((TM, GATE_DIM), BF16),
                        pltpu.VMEM((TM, GATE_DIM), BF16),
                        pltpu.VMEM((A_GROUPS, GMLP_BLOCK, GMLP_BLOCK), BF16)],
        compiler_params=_params("arbitrary"),
        name="gmlp_mixer",
    )(h, g, w_in, lng, lnb, ws, bs, w_out, w1f, w2f)


def _mlp_residual(h, g_ref, w1_ref, w2_ref, a_scr, anchors=None):
    hn = _rms(h, g_ref[...]).astype(BF16)
    n_chunks = D_FF // NCHUNK
    slab = h.shape[0] // n_chunks
    for c in range(n_chunks):
        cols = slice(c * NCHUNK, (c + 1) * NCHUNK)
        a = jnp.maximum(jnp.dot(hn, w1_ref[:, cols], preferred_element_type=F32), 0.0)
        if anchors is None:
            a_scr[:, cols] = (a * a).astype(BF16)
        else:
            for r in range(n_chunks):
                rows = slice(r * slab, (r + 1) * slab)
                blk = a[rows] + anchors[c] if r == c else a[rows]
                a_scr[rows, cols] = (blk * blk).astype(BF16)
    return h + jnp.dot(a_scr[...], w2_ref[...], preferred_element_type=F32)


def _mlp_rope_kernel(h_ref, g_ref, w1_ref, w2_ref, pos_ref, rope_ref, o_ref, cos_ref, sin_ref, a_scr):
    n_slabs = D_FF // NCHUNK
    slab = TM_MLP // n_slabs
    anchors = []
    for c in range(n_slabs):
        rows = slice(c * slab, (c + 1) * slab)
        sin_cos = jnp.sin(pos_ref[rows, :].astype(F32) * rope_ref[0:1, :] + rope_ref[1:2, :])
        cos_ref[rows, :] = pltpu.roll(sin_cos, LANES - QK_ROPE // 2, 1)
        sin_ref[rows, :] = sin_cos * rope_ref[2:3, :]
        anchors.append(sin_cos[:, 0:1] * 0.0)
    o_ref[...] = _mlp_residual(h_ref[...], g_ref, w1_ref, w2_ref, a_scr, anchors)


def _mlp_final_kernel(h_ref, g_ref, w1_ref, w2_ref, gf_ref, o_ref, a_scr):
    o_ref[...] = _rms(_mlp_residual(h_ref[...], g_ref, w1_ref, w2_ref, a_scr), gf_ref[...])


def _mlp_specs(layer):
    tok = pl.BlockSpec((TM_MLP, D_MODEL), lambda t: (t, 0))
    w1 = pl.BlockSpec((None, D_MODEL, D_FF), lambda t: (layer, 0, 0), pipeline_mode=pl.Buffered(1))
    w2 = pl.BlockSpec((None, D_FF, D_MODEL), lambda t: (layer, 0, 0), pipeline_mode=pl.Buffered(1))
    return tok, [tok, _resident((1, D_MODEL)), w1, w2]


def _mlp_rope(h, g, w1, w2, pos, rope_tab, layer):
    tok, in_specs = _mlp_specs(layer)
    tab = pl.BlockSpec((TM_MLP, LANES), lambda t: (t, 0))
    return pl.pallas_call(
        _mlp_rope_kernel,
        grid=(TOKENS // TM_MLP,),
        in_specs=in_specs + [pl.BlockSpec((TM_MLP, 1), lambda t: (t, 0)), _resident((8, LANES))],
        out_specs=[tok, tab, tab],
        out_shape=[jax.ShapeDtypeStruct((TOKENS, D_MODEL), F32),
                   jax.ShapeDtypeStruct((TOKENS, LANES), F32),
                   jax.ShapeDtypeStruct((TOKENS, LANES), F32)],
        scratch_shapes=[pltpu.VMEM((TM_MLP, D_FF), BF16)],
        compiler_params=_params("arbitrary"),
        name="mlp_rope",
    )(h, g, w1, w2, pos, rope_tab)


def _mlp_final(h, g, w1, w2, gf, layer):
    tok, in_specs = _mlp_specs(layer)
    return pl.pallas_call(
        _mlp_final_kernel,
        grid=(TOKENS // TM_MLP,),
        in_specs=in_specs + [_resident((1, D_MODEL))],
        out_specs=tok,
        out_shape=jax.ShapeDtypeStruct((TOKENS, D_MODEL), F32),
        scratch_shapes=[pltpu.VMEM((TM_MLP, D_FF), BF16)],
        compiler_params=_params("arbitrary"),
        name="mlp_final",
    )(h, g, w1, w2, gf)


def _rope(x, cos, sin_signed):
    return x * cos + pltpu.roll(x, LANES // 2, 1) * sin_signed


def _proj_kernel(h_ref, cos_ref, sin_ref, gkv_ref, gq_ref, wkva_ref, gkva_ref, wk_ref, wvt_ref,
                 wqa_ref, gqa_ref, wqb_ref, q_ref, kn_ref, kpe_ref, vt_ref):
    h = h_ref[0]
    y = h * lax.rsqrt(jnp.mean(h * h, axis=-1, keepdims=True) + EPS)
    cos = cos_ref[0]
    sin_signed = sin_ref[0]

    ckv = jnp.dot((y * gkv_ref[...]).astype(BF16), wkva_ref[...], preferred_element_type=F32)
    c_kv = _rms(ckv[:, :KV_LORA], gkva_ref[...]).astype(BF16)
    kpe_ref[0] = _rope(ckv[:, KV_LORA:], cos, sin_signed).astype(BF16)
    kn = jnp.dot(c_kv, wk_ref[...], preferred_element_type=F32).astype(BF16)
    for hd in range(B_HEADS):
        kn_ref[0, hd] = kn[:, hd * QK_NOPE:(hd + 1) * QK_NOPE]
    vt = lax.dot_general(wvt_ref[...], c_kv, NT_DIMS, preferred_element_type=F32).astype(BF16)
    for kb in range(TM // TK):
        vt_ref[0, kb] = vt[:, kb * TK:(kb + 1) * TK]

    scale = (QK_NOPE + QK_ROPE) ** -0.5 * LOG2_E
    cq = jnp.dot((y * gq_ref[...]).astype(BF16), wqa_ref[...], preferred_element_type=F32)
    cq = _rms(cq, gqa_ref[...] * scale).astype(BF16)
    for hd in range(B_HEADS):
        q = jnp.dot(cq, wqb_ref[:, hd * QK_PAD:(hd + 1) * QK_PAD], preferred_element_type=F32)
        q_ref[0, hd, :, :QK_NOPE] = q[:, :QK_NOPE].astype(BF16)
        q_ref[0, hd, :, QK_NOPE:] = _rope(q[:, QK_NOPE:], cos, sin_signed).astype(BF16)


def _proj(h, cos_tab, sin_tab, gkv, gq, wkva, gkva, wk, wvt, wqa, gqa, wqb):
    n_t = SEQ // TM
    tab = pl.BlockSpec((1, TM, LANES), lambda b, t: (b, t, 0))
    return pl.pallas_call(
        _proj_kernel,
        grid=(BATCH, n_t),
        in_specs=[pl.BlockSpec((1, TM, D_MODEL), lambda b, t: (b, t, 0)), tab, tab,
                  _resident((1, D_MODEL)), _resident((1, D_MODEL)),
                  _resident((D_MODEL, KV_LORA + LANES)), _resident((1, KV_LORA)),
                  _resident((KV_LORA, B_HEADS * QK_NOPE)), _resident((B_HEADS * V_HEAD, KV_LORA)),
                  _resident((D_MODEL, Q_LORA)), _resident((1, Q_LORA)),
                  _resident((Q_LORA, B_HEADS * QK_PAD))],
        out_specs=[pl.BlockSpec((1, B_HEADS, TM, QK_PAD), lambda b, t: (b, 0, t, 0)),
                   pl.BlockSpec((1, B_HEADS, TM, QK_NOPE), lambda b, t: (b, 0, t, 0)),
                   pl.BlockSpec((1, TM, LANES), lambda b, t: (b, t, 0)),
                   pl.BlockSpec((1, TM // TK, B_HEADS * V_HEAD, TK), lambda b, t: (b, t, 0, 0))],
        out_shape=[jax.ShapeDtypeStruct((BATCH, B_HEADS, SEQ, QK_PAD), BF16),
                   jax.ShapeDtypeStruct((BATCH, B_HEADS, SEQ, QK_NOPE), BF16),
                   jax.ShapeDtypeStruct((BATCH, SEQ, LANES), BF16),
                   jax.ShapeDtypeStruct((BATCH, SEQ // TK, B_HEADS * V_HEAD, TK), BF16)],
        compiler_params=_params("arbitrary", "arbitrary"),
        name="qkv_proj",
    )(h, cos_tab, sin_tab, gkv, gq, wkva, gkva, wk, wvt, wqa, gqa, wqb)


def _attn_kernel(h_ref, q_ref, kn_ref, kpe_ref, vt_ref, wo_ref, o_ref, sa_scr, sb_scr, mta_scr, mtb_scr,
                 m_scr, acc_scr, ot_scr):
    i = pl.program_id(1)
    kc_id = lax.broadcasted_iota(jnp.int32, (TK, TQ), 0) >> CHUNK_SHIFT
    qc_id = lax.broadcasted_iota(jnp.int32, (TK, TQ), 1) >> CHUNK_SHIFT
    diag_ok = kc_id <= qc_id
    ones_rows = jnp.ones((ONES_ROWS, TK), BF16)

    def score_stage(j, s_scr, mt_scr, diagonal):
        rows = pl.ds(pl.multiple_of(j * TK, TK), TK)
        k_pe = kpe_ref[0, rows, :]
        for hd in range(B_HEADS):
            k_cat = jnp.concatenate([kn_ref[0, hd, rows, :], k_pe], axis=-1)
            s = lax.dot_general(k_cat, q_ref[0, hd], NT_DIMS, preferred_element_type=F32)
            if diagonal:
                s = jnp.where(diag_ok, s, NEG)
            s_scr[hd] = s
            mt_scr[hd] = jnp.max(s, axis=0, keepdims=True)

    def softmax_stage(j, s_scr, mt_scr, first):
        for hd in range(B_HEADS):
            v_aug = jnp.concatenate([vt_ref[0, j, hd * V_HEAD:(hd + 1) * V_HEAD, :], ones_rows], axis=0)
            if first:
                m_new = mt_scr[hd]
                p = jnp.exp2(s_scr[hd] - m_new)
                acc_scr[hd] = jnp.dot(v_aug, p.astype(BF16), preferred_element_type=F32)
            else:
                m = m_scr[hd]
                m_new = jnp.maximum(m, mt_scr[hd])
                alpha = jnp.exp2(m - m_new)
                p = jnp.exp2(s_scr[hd] - m_new)
                acc_scr[hd] = alpha * acc_scr[hd] + jnp.dot(v_aug, p.astype(BF16),
                                                            preferred_element_type=F32)
            m_scr[hd] = m_new

    score_stage(i, sb_scr, mtb_scr, True)
    score_stage(0, sa_scr, mta_scr, False)
    softmax_stage(i, sb_scr, mtb_scr, True)

    def tile_pair(u, _):
        j = 2 * u
        score_stage(j + 1, sb_scr, mtb_scr, False)
        softmax_stage(j, sa_scr, mta_scr, False)
        score_stage(j + 2, sa_scr, mta_scr, False)
        softmax_stage(j + 1, sb_scr, mtb_scr, False)
        return 0

    lax.fori_loop(0, jnp.maximum(i - 1, 0) >> 1, tile_pair, 0)

    @pl.when((i & 1) == 1)
    def _():
        softmax_stage(i - 1, sa_scr, mta_scr, False)

    @pl.when(jnp.logical_and((i & 1) == 0, i > 0))
    def _():
        score_stage(i - 1, sb_scr, mtb_scr, False)
        softmax_stage(i - 2, sa_scr, mta_scr, False)
        softmax_stage(i - 1, sb_scr, mtb_scr, False)

    for hd in range(B_HEADS):
        inv_l = 1.0 / acc_scr[hd, V_HEAD:V_HEAD + 1, :]
        ot_scr[hd * V_HEAD:(hd + 1) * V_HEAD, :] = (acc_scr[hd, :V_HEAD, :] * inv_l).astype(BF16)
    out = lax.dot_general(ot_scr[...], wo_ref[...], TN_DIMS, preferred_element_type=F32)
    o_ref[0] = h_ref[0] + out


def _attn(h, q, kn, kpe, vt, wo):
    return pl.pallas_call(
        _attn_kernel,
        grid=(BATCH, SEQ // TQ),
        in_specs=[pl.BlockSpec((1, TQ, D_MODEL), lambda b, i: (b, i, 0)),
                  pl.BlockSpec((1, B_HEADS, TQ, QK_PAD), lambda b, i: (b, 0, i, 0)),
                  pl.BlockSpec((1, B_HEADS, SEQ, QK_NOPE), lambda b, i: (b, 0, 0, 0)),
                  pl.BlockSpec((1, SEQ, LANES), lambda b, i: (b, 0, 0)),
                  pl.BlockSpec((1, SEQ // TK, B_HEADS * V_HEAD, TK), lambda b, i: (b, 0, 0, 0)),
                  _resident((B_HEADS * V_HEAD, D_MODEL))],
        out_specs=pl.BlockSpec((1, TQ, D_MODEL), lambda b, i: (b, i, 0)),
        out_shape=jax.ShapeDtypeStruct((BATCH, SEQ, D_MODEL), F32),
        scratch_shapes=[pltpu.VMEM((B_HEADS, TK, TQ), F32), pltpu.VMEM((B_HEADS, TK, TQ), F32),
                        pltpu.VMEM((B_HEADS, 1, TQ), F32), pltpu.VMEM((B_HEADS, 1, TQ), F32),
                        pltpu.VMEM((B_HEADS, 1, TQ), F32),
                        pltpu.VMEM((B_HEADS, V_HEAD + ONES_ROWS, TQ), F32),
                        pltpu.VMEM((B_HEADS * V_HEAD, TQ), BF16)],
        compiler_params=_params("arbitrary", "arbitrary"),
        name="mla_attention",
    )(h, q, kn, kpe, vt, wo)


def _row(v):
    return v.reshape(1, -1)


def _spread_rope(w):
    half = QK_ROPE // 2
    zeros = jnp.zeros(w.shape[:-1] + (half,), w.dtype)
    return jnp.concatenate([w[..., :half], zeros, w[..., half:], zeros], axis=-1)


def kernel(x, positions, norm_mix_g, norm_mlp_g, a_w_in, a_ln_v_g, a_ln_v_b, a_w_s, a_b_s, a_w_out,
           b_w_q_a, b_q_norm_g, b_w_q_b, b_w_o, kv_src_norm_g, kv_w_a, kv_a_norm_g, kv_w_b,
           mlp_w1, mlp_w2, final_norm_g):
    w_in = a_w_in[0].astype(BF16)
    w_out = a_w_out[0].astype(BF16)
    wkva = jnp.concatenate([kv_w_a[:, :KV_LORA], _spread_rope(kv_w_a[:, KV_LORA:])], axis=-1).astype(BF16)
    kvb = kv_w_b.reshape(KV_LORA, B_HEADS, QK_NOPE + V_HEAD)
    wk = kvb[:, :, :QK_NOPE].reshape(KV_LORA, B_HEADS * QK_NOPE).astype(BF16)
    wvt = kvb[:, :, QK_NOPE:].reshape(KV_LORA, B_HEADS * V_HEAD).T.astype(BF16)
    wqa = b_w_q_a[0].astype(BF16)
    wqb = b_w_q_b[0].reshape(Q_LORA, B_HEADS, QK_NOPE + QK_ROPE)
    wqb = jnp.concatenate([wqb[:, :, :QK_NOPE], _spread_rope(wqb[:, :, QK_NOPE:])], axis=-1)
    wqb = wqb.reshape(Q_LORA, B_HEADS * QK_PAD).astype(BF16)
    wo = b_w_o[0].astype(BF16)

    inv_freq = ROPE_THETA ** (-jnp.arange(0, QK_ROPE, 2, dtype=F32) / QK_ROPE)
    group = jnp.ones((QK_ROPE // 2,), F32)
    rope_tab = jnp.zeros((8, LANES), F32)
    rope_tab = rope_tab.at[0].set(jnp.tile(inv_freq, 4))
    rope_tab = rope_tab.at[1].set(jnp.concatenate([0.0 * group, HALF_PI * group, 0.0 * group, HALF_PI * group]))
    rope_tab = rope_tab.at[2].set(jnp.concatenate([-group, 0.0 * group, group, 0.0 * group]))

    h = x.reshape(TOKENS, D_MODEL)
    h, w1, w2 = _gmlp(h, _row(norm_mix_g[0]), w_in, _row(a_ln_v_g[0]), _row(a_ln_v_b[0]), a_w_s[0],
                      a_b_s[0].reshape(A_GROUPS, GMLP_BLOCK, 1), w_out, mlp_w1, mlp_w2)
    h, cos_tab, sin_tab = _mlp_rope(h, _row(norm_mlp_g[0]), w1, w2, positions.reshape(TOKENS, 1),
                                    rope_tab, 0)

    h3 = h.reshape(BATCH, SEQ, D_MODEL)
    q, kn, kpe, vt = _proj(h3, cos_tab.reshape(BATCH, SEQ, LANES), sin_tab.reshape(BATCH, SEQ, LANES),
                           _row(kv_src_norm_g), _row(norm_mix_g[1]), wkva, _row(kv_a_norm_g), wk, wvt,
                           wqa, _row(b_q_norm_g[0]), wqb)
    h3 = _attn(h3, q, kn, kpe, vt, wo)

    h = _mlp_final(h3.reshape(TOKENS, D_MODEL), _row(norm_mlp_g[1]), w1, w2, _row(final_norm_g), 1)
    return h.reshape(BATCH, SEQ, D_MODEL)
```

```python
import functools

import jax
import jax.numpy as jnp
import numpy as np
from jax import lax
from jax.experimental import pallas as pl
from jax.experimental.pallas import tpu as pltpu

D_MODEL = 1024
BATCH = 8
SEQ = 2048
TOKENS = BATCH * SEQ
DEPTH = 2
CHUNK = 64
CHUNK_SHIFT = 6
GMLP_BLOCK = 128
GATE_DIM = 2 * D_MODEL
A_GROUPS = 8
A_GROUP_DIM = GATE_DIM // A_GROUPS
B_HEADS = 8
QK_NOPE = 128
QK_ROPE = 64
V_HEAD = 128
Q_LORA = 384
KV_LORA = 256
ROPE_THETA = 10000.0
D_FF = 4 * D_MODEL
EPS = 1e-6

LANES = 128
QK_PAD = 2 * LANES
VMEM_LIMIT = 56 * 1024 * 1024

TM = 512
TM_MLP = 1024
NCHUNK = 512
TQ = 256
TK = 256
ONES_ROWS = 16

F32 = jnp.float32
BF16 = jnp.bfloat16
NEG = float(np.finfo(np.float32).min)
SQRT_HALF = float(np.sqrt(0.5))
LOG2_E = float(np.log2(np.e))
HALF_PI = float(np.pi / 2)
NT_DIMS = (((1,), (1,)), ((), ()))
TN_DIMS = (((0,), (0,)), ((), ()))


def _rms(x, g):
    ms = jnp.mean(x * x, axis=-1, keepdims=True)
    return x * lax.rsqrt(ms + EPS) * g


def _resident(shape):
    nd = len(shape)
    return pl.BlockSpec(shape, lambda *_: (0,) * nd, pipeline_mode=pl.Buffered(1))


def _params(*sem):
    return pltpu.CompilerParams(dimension_semantics=sem, vmem_limit_bytes=VMEM_LIMIT)


def _gelu(z):
    return 0.5 * z * (1.0 + lax.erf(z * SQRT_HALF))


def _gmlp_kernel(h_ref, g_ref, w_in_ref, lng_ref, lnb_ref, ws_ref, bs_ref, w_out_ref, w1f_ref, w2f_ref,
                 o_ref, w1b_ref, w2b_ref, v_scr, vn_scr, gated_scr, ws_scr):
    w1b_ref[...] = w1f_ref[...].astype(BF16)
    w2b_ref[...] = w2f_ref[...].astype(BF16)

    h = h_ref[...]
    hn = _rms(h, g_ref[...]).astype(BF16)
    n_half = GATE_DIM // NCHUNK
    row_blocks = [slice(r * GMLP_BLOCK, (r + 1) * GMLP_BLOCK) for r in range(TM // GMLP_BLOCK)]

    for c in range(n_half):
        w_cols = slice(GATE_DIM + c * NCHUNK, GATE_DIM + (c + 1) * NCHUNK)
        v_scr[:, c * NCHUNK:(c + 1) * NCHUNK] = _gelu(
            jnp.dot(hn, w_in_ref[:, w_cols], preferred_element_type=F32))

    ii = lax.broadcasted_iota(jnp.int32, (GMLP_BLOCK, GMLP_BLOCK), 0) >> CHUNK_SHIFT
    jj = lax.broadcasted_iota(jnp.int32, (GMLP_BLOCK, GMLP_BLOCK), 1) >> CHUNK_SHIFT
    for g in range(A_GROUPS):
        ws_scr[g] = jnp.where(jj <= ii, ws_ref[g], 0.0).astype(BF16)

    for rows in row_blocks:
        v = v_scr[rows, :]
        mu = jnp.mean(v, axis=-1, keepdims=True)
        var = jnp.mean(jnp.square(v - mu), axis=-1, keepdims=True)
        vn_scr[rows, :] = ((v - mu) * lax.rsqrt(var + EPS) * lng_ref[...] + lnb_ref[...]).astype(BF16)

    for c in range(n_half):
        u = _gelu(jnp.dot(hn, w_in_ref[:, c * NCHUNK:(c + 1) * NCHUNK], preferred_element_type=F32))
        for gl in range(NCHUNK // A_GROUP_DIM):
            g = c * (NCHUNK // A_GROUP_DIM) + gl
            cols = slice(g * A_GROUP_DIM, (g + 1) * A_GROUP_DIM)
            for rows in row_blocks:
                sv = jnp.dot(ws_scr[g], vn_scr[rows, cols], preferred_element_type=F32) + bs_ref[g]
                u_blk = u[rows, gl * A_GROUP_DIM:(gl + 1) * A_GROUP_DIM]
                gated_scr[rows, cols] = (u_blk * sv).astype(BF16)

    o_ref[...] = h + jnp.dot(gated_scr[...], w_out_ref[...], preferred_element_type=F32)


def _gmlp(h, g, w_in, lng, lnb, ws, bs, w_out, w1f, w2f):
    steps = TOKENS // TM
    tok = pl.BlockSpec((TM, D_MODEL), lambda t: (t, 0))
    w1_slab = pl.BlockSpec((DEPTH, D_MODEL // steps, D_FF), lambda t: (0, t, 0))
    w2_slab = pl.BlockSpec((DEPTH, D_FF // steps, D_MODEL), lambda t: (0, t, 0))
    return pl.pallas_call(
        _gmlp_kernel,
        grid=(steps,),
        in_specs=[tok, _resident((1, D_MODEL)), _resident((D_MODEL, 2 * GATE_DIM)),
                  _resident((1, GATE_DIM)), _resident((1, GATE_DIM)),
                  _resident((A_GROUPS, GMLP_BLOCK, GMLP_BLOCK)),
                  _resident((A_GROUPS, GMLP_BLOCK, 1)), _resident((GATE_DIM, D_MODEL)),
                  w1_slab, w2_slab],
        out_specs=[tok, w1_slab, w2_slab],
        out_shape=[jax.ShapeDtypeStruct((TOKENS, D_MODEL), F32),
                   jax.ShapeDtypeStruct((DEPTH, D_MODEL, D_FF), BF16),
                   jax.ShapeDtypeStruct((DEPTH, D_FF, D_MODEL), BF16)],
        scratch_shapes=[pltpu.VMEM((TM, GATE_DIM), F32), pltpu.VMEM((TM, GATE_DIM), BF16),
                        pltpu.VMEM((TM, GATE_DIM), BF16),
                        pltpu.VMEM((A_GROUPS, GMLP_BLOCK, GMLP_BLOCK), BF16)],
        compiler_params=_params("arbitrary"),
        name="gmlp_mixer",
    )(h, g, w_in, lng, lnb, ws, bs, w_out, w1f, w2f)


def _mlp_residual(h, g_ref, w1_ref, w2_ref, a_scr, anchors=None):
    hn = _rms(h, g_ref[...]).astype(BF16)
    n_chunks = D_FF // NCHUNK
    slab = h.shape[0] // n_chunks
    for c in range(n_chunks):
        cols = slice(c * NCHUNK, (c + 1) * NCHUNK)
        a = jnp.maximum(jnp.dot(hn, w1_ref[:, cols], preferred_element_type=F32), 0.0)
        if anchors is None:
            a_scr[:, cols] = (a * a).astype(BF16)
        else:
            for r in range(n_chunks):
                rows = slice(r * slab, (r + 1) * slab)
                blk = a[rows] + anchors[c] if r == c else a[rows]
                a_scr[rows, cols] = (blk * blk).astype(BF16)
    return h + jnp.dot(a_scr[...], w2_ref[...], preferred_element_type=F32)


def _mlp_rope_kernel(h_ref, g_ref, w1_ref, w2_ref, pos_ref, rope_ref, o_ref, cos_ref, sin_ref, a_scr):
    n_slabs = D_FF // NCHUNK
    slab = TM_MLP // n_slabs
    anchors = []
    for c in range(n_slabs):
        rows = slice(c * slab, (c + 1) * slab)
        sin_cos = jnp.sin(pos_ref[rows, :].astype(F32) * rope_ref[0:1, :] + rope_ref[1:2, :])
        cos_ref[rows, :] = pltpu.roll(sin_cos, LANES - QK_ROPE // 2, 1)
        sin_ref[rows, :] = sin_cos * rope_ref[2:3, :]
        anchors.append(sin_cos[:, 0:1] * 0.0)
    o_ref[...] = _mlp_residual(h_ref[...], g_ref, w1_ref, w2_ref, a_scr, anchors)


def _mlp_final_kernel(h_ref, g_ref, w1_ref, w2_ref, gf_ref, o_ref, a_scr):
    o_ref[...] = _rms(_mlp_residual(h_ref[...], g_ref, w1_ref, w2_ref, a_scr), gf_ref[...])


def _mlp_specs(layer):
    tok = pl.BlockSpec((TM_MLP, D_MODEL), lambda t: (t, 0))
    w1 = pl.BlockSpec((None, D_MODEL, D_FF), lambda t: (layer, 0, 0), pipeline_mode=pl.Buffered(1))
    w2 = pl.BlockSpec((None, D_FF, D_MODEL), lambda t: (layer, 0, 0), pipeline_mode=pl.Buffered(1))
    return tok, [tok, _resident((1, D_MODEL)), w1, w2]


def _mlp_rope(h, g, w1, w2, pos, rope_tab, layer):
    tok, in_specs = _mlp_specs(layer)
    tab = pl.BlockSpec((TM_MLP, LANES), lambda t: (t, 0))
    return pl.pallas_call(
        _mlp_rope_kernel,
        grid=(TOKENS // TM_MLP,),
        in_specs=in_specs + [pl.BlockSpec((TM_MLP, 1), lambda t: (t, 0)), _resident((8, LANES))],
        out_specs=[tok, tab, tab],
        out_shape=[jax.ShapeDtypeStruct((TOKENS, D_MODEL), F32),
                   jax.ShapeDtypeStruct((TOKENS, LANES), F32),
                   jax.ShapeDtypeStruct((TOKENS, LANES), F32)],
        scratch_shapes=[pltpu.VMEM((TM_MLP, D_FF), BF16)],
        compiler_params=_params("arbitrary"),
        name="mlp_rope",
    )(h, g, w1, w2, pos, rope_tab)


def _mlp_final(h, g, w1, w2, gf, layer):
    tok, in_specs = _mlp_specs(layer)
    return pl.pallas_call(
        _mlp_final_kernel,
        grid=(TOKENS // TM_MLP,),
        in_specs=in_specs + [_resident((1, D_MODEL))],
        out_specs=tok,
        out_shape=jax.ShapeDtypeStruct((TOKENS, D_MODEL), F32),
        scratch_shapes=[pltpu.VMEM((TM_MLP, D_FF), BF16)],
        compiler_params=_params("arbitrary"),
        name="mlp_final",
    )(h, g, w1, w2, gf)


def _rope(x, cos, sin_signed):
    return x * cos + pltpu.roll(x, LANES // 2, 1) * sin_signed


def _proj_kernel(h_ref, cos_ref, sin_ref, gkv_ref, gq_ref, wkva_ref, gkva_ref, wk_ref, wvt_ref,
                 wqa_ref, gqa_ref, wqb_ref, q_ref, kn_ref, kpe_ref, vt_ref):
    h = h_ref[0]
    y = h * lax.rsqrt(jnp.mean(h * h, axis=-1, keepdims=True) + EPS)
    cos = cos_ref[0]
    sin_signed = sin_ref[0]

    ckv = jnp.dot((y * gkv_ref[...]).astype(BF16), wkva_ref[...], preferred_element_type=F32)
    c_kv = _rms(ckv[:, :KV_LORA], gkva_ref[...]).astype(BF16)
    kpe_ref[0] = _rope(ckv[:, KV_LORA:], cos, sin_signed).astype(BF16)
    kn = jnp.dot(c_kv, wk_ref[...], preferred_element_type=F32).astype(BF16)
    for hd in range(B_HEADS):
        kn_ref[0, hd] = kn[:, hd * QK_NOPE:(hd + 1) * QK_NOPE]
    vt = lax.dot_general(wvt_ref[...], c_kv, NT_DIMS, preferred_element_type=F32).astype(BF16)
    for kb in range(TM // TK):
        vt_ref[0, kb] = vt[:, kb * TK:(kb + 1) * TK]

    scale = (QK_NOPE + QK_ROPE) ** -0.5 * LOG2_E
    cq = jnp.dot((y * gq_ref[...]).astype(BF16), wqa_ref[...], preferred_element_type=F32)
    cq = _rms(cq, gqa_ref[...] * scale).astype(BF16)
    for hd in range(B_HEADS):
        q = jnp.dot(cq, wqb_ref[:, hd * QK_PAD:(hd + 1) * QK_PAD], preferred_element_type=F32)
        q_ref[0, hd, :, :QK_NOPE] = q[:, :QK_NOPE].astype(BF16)
        q_ref[0, hd, :, QK_NOPE:] = _rope(q[:, QK_NOPE:], cos, sin_signed).astype(BF16)


def _proj(h, cos_tab, sin_tab, gkv, gq, wkva, gkva, wk, wvt, wqa, gqa, wqb):
    n_t = SEQ // TM
    tab = pl.BlockSpec((1, TM, LANES), lambda b, t: (b, t, 0))
    return pl.pallas_call(
        _proj_kernel,
        grid=(BATCH, n_t),
        in_specs=[pl.BlockSpec((1, TM, D_MODEL), lambda b, t: (b, t, 0)), tab, tab,
                  _resident((1, D_MODEL)), _resident((1, D_MODEL)),
                  _resident((D_MODEL, KV_LORA + LANES)), _resident((1, KV_LORA)),
                  _resident((KV_LORA, B_HEADS * QK_NOPE)), _resident((B_HEADS * V_HEAD, KV_LORA)),
                  _resident((D_MODEL, Q_LORA)), _resident((1, Q_LORA)),
                  _resident((Q_LORA, B_HEADS * QK_PAD))],
        out_specs=[pl.BlockSpec((1, B_HEADS, TM, QK_PAD), lambda b, t: (b, 0, t, 0)),
                   pl.BlockSpec((1, B_HEADS, TM, QK_NOPE), lambda b, t: (b, 0, t, 0)),
                   pl.BlockSpec((1, TM, LANES), lambda b, t: (b, t, 0)),
                   pl.BlockSpec((1, TM // TK, B_HEADS * V_HEAD, TK), lambda b, t: (b, t, 0, 0))],
        out_shape=[jax.ShapeDtypeStruct((BATCH, B_HEADS, SEQ, QK_PAD), BF16),
                   jax.ShapeDtypeStruct((BATCH, B_HEADS, SEQ, QK_NOPE), BF16),
                   jax.ShapeDtypeStruct((BATCH, SEQ, LANES), BF16),
                   jax.ShapeDtypeStruct((BATCH, SEQ // TK, B_HEADS * V_HEAD, TK), BF16)],
        compiler_params=_params("arbitrary", "arbitrary"),
        name="qkv_proj",
    )(h, cos_tab, sin_tab, gkv, gq, wkva, gkva, wk, wvt, wqa, gqa, wqb)


def _attn_kernel(h_ref, q_ref, kn_ref, kpe_ref, vt_ref, wo_ref, o_ref, sa_scr, sb_scr, mta_scr, mtb_scr,
                 m_scr, acc_scr, ot_scr):
    k = pl.program_id(1)
    kc_id = lax.broadcasted_iota(jnp.int32, (TK, TQ), 0) >> CHUNK_SHIFT
    qc_id = lax.broadcasted_iota(jnp.int32, (TK, TQ), 1) >> CHUNK_SHIFT
    diag_ok = kc_id <= qc_id
    ones_rows = jnp.ones((ONES_ROWS, TK), BF16)

    def score_stage(half, j, s_scr, mt_scr, diagonal):
        rows = pl.ds(pl.multiple_of(j * TK, TK), TK)
        k_pe = kpe_ref[0, rows, :]
        for hd in range(B_HEADS):
            k_cat = jnp.concatenate([kn_ref[0, hd, rows, :], k_pe], axis=-1)
            q_h = q_ref[0, hd, half * TQ:(half + 1) * TQ, :]
            s = lax.dot_general(k_cat, q_h, NT_DIMS, preferred_element_type=F32)
            if diagonal:
                s = jnp.where(diag_ok, s, NEG)
            s_scr[hd] = s
            mt_scr[hd] = jnp.max(s, axis=0, keepdims=True)

    def softmax_stage(half, j, s_scr, mt_scr, first):
        for hd in range(B_HEADS):
            v_aug = jnp.concatenate([vt_ref[0, j, hd * V_HEAD:(hd + 1) * V_HEAD, :], ones_rows], axis=0)
            if first:
                m_new = mt_scr[hd]
                p = jnp.exp2(s_scr[hd] - m_new)
                acc_scr[half, hd] = jnp.dot(v_aug, p.astype(BF16), preferred_element_type=F32)
            else:
                m = m_scr[half, hd]
                m_new = jnp.maximum(m, mt_scr[hd])
                alpha = jnp.exp2(m - m_new)
                p = jnp.exp2(s_scr[hd] - m_new)
                acc_scr[half, hd] = alpha * acc_scr[half, hd] + jnp.dot(v_aug, p.astype(BF16),
                                                                        preferred_element_type=F32)
            m_scr[half, hd] = m_new

    def first_tiles(half, i):
        score_stage(half, i, sb_scr, mtb_scr, True)
        score_stage(half, 0, sa_scr, mta_scr, False)
        softmax_stage(half, i, sb_scr, mtb_scr, True)

    def tile_pairs(half, n_pairs):
        def tile_pair(u, _):
            j = 2 * u
            score_stage(half, j + 1, sb_scr, mtb_scr, False)
            softmax_stage(half, j, sa_scr, mta_scr, False)
            score_stage(half, j + 2, sa_scr, mta_scr, False)
            softmax_stage(half, j + 1, sb_scr, mtb_scr, False)
            return 0

        lax.fori_loop(0, n_pairs, tile_pair, 0)

    def project(half):
        q_rows = slice(half * TQ, (half + 1) * TQ)
        for hd in range(B_HEADS):
            inv_l = 1.0 / acc_scr[half, hd, V_HEAD:V_HEAD + 1, :]
            ot_scr[half, hd * V_HEAD:(hd + 1) * V_HEAD, :] = (acc_scr[half, hd, :V_HEAD, :] * inv_l).astype(BF16)
        out = lax.dot_general(ot_scr[half], wo_ref[...], TN_DIMS, preferred_element_type=F32)
        o_ref[0, q_rows, :] = h_ref[0, q_rows, :] + out

    i_a = 2 * k
    first_tiles(0, i_a)
    tile_pairs(0, jnp.maximum(k - 1, 0))

    @pl.when(k > 0)
    def _():
        score_stage(0, i_a - 1, sb_scr, mtb_scr, False)
        softmax_stage(0, i_a - 2, sa_scr, mta_scr, False)
        softmax_stage(0, i_a - 1, sb_scr, mtb_scr, False)

    i_b = 2 * k + 1
    project(0)
    first_tiles(1, i_b)
    tile_pairs(1, k)
    softmax_stage(1, i_b - 1, sa_scr, mta_scr, False)
    project(1)


def _attn(h, q, kn, kpe, vt, wo):
    tq2 = 2 * TQ
    return pl.pallas_call(
        _attn_kernel,
        grid=(BATCH, SEQ // tq2),
        in_specs=[pl.BlockSpec((1, tq2, D_MODEL), lambda b, k: (b, k, 0)),
                  pl.BlockSpec((1, B_HEADS, tq2, QK_PAD), lambda b, k: (b, 0, k, 0)),
                  pl.BlockSpec((1, B_HEADS, SEQ, QK_NOPE), lambda b, k: (b, 0, 0, 0)),
                  pl.BlockSpec((1, SEQ, LANES), lambda b, k: (b, 0, 0)),
                  pl.BlockSpec((1, SEQ // TK, B_HEADS * V_HEAD, TK), lambda b, k: (b, 0, 0, 0)),
                  _resident((B_HEADS * V_HEAD, D_MODEL))],
        out_specs=pl.BlockSpec((1, tq2, D_MODEL), lambda b, k: (b, k, 0)),
        out_shape=jax.ShapeDtypeStruct((BATCH, SEQ, D_MODEL), F32),
        scratch_shapes=[pltpu.VMEM((B_HEADS, TK, TQ), F32), pltpu.VMEM((B_HEADS, TK, TQ), F32),
                        pltpu.VMEM((B_HEADS, 1, TQ), F32), pltpu.VMEM((B_HEADS, 1, TQ), F32),
                        pltpu.VMEM((2, B_HEADS, 1, TQ), F32),
                        pltpu.VMEM((2, B_HEADS, V_HEAD + ONES_ROWS, TQ), F32),
                        pltpu.VMEM((2, B_HEADS * V_HEAD, TQ), BF16)],
        compiler_params=_params("arbitrary", "arbitrary"),
        name="mla_attention",
    )(h, q, kn, kpe, vt, wo)


def _row(v):
    return v.reshape(1, -1)


def _spread_rope(w):
    half = QK_ROPE // 2
    zeros = jnp.zeros(w.shape[:-1] + (half,), w.dtype)
    return jnp.concatenate([w[..., :half], zeros, w[..., half:], zeros], axis=-1)


def kernel(x, positions, norm_mix_g, norm_mlp_g, a_w_in, a_ln_v_g, a_ln_v_b, a_w_s, a_b_s, a_w_out,
           b_w_q_a, b_q_norm_g, b_w_q_b, b_w_o, kv_src_norm_g, kv_w_a, kv_a_norm_g, kv_w_b,
           mlp_w1, mlp_w2, final_norm_g):
    w_in = a_w_in[0].astype(BF16)
    w_out = a_w_out[0].astype(BF16)
    wkva = jnp.concatenate([kv_w_a[:, :KV_LORA], _spread_rope(kv_w_a[:, KV_LORA:])], axis=-1).astype(BF16)
    kvb = kv_w_b.reshape(KV_LORA, B_HEADS, QK_NOPE + V_HEAD)
    wk = kvb[:, :, :QK_NOPE].reshape(KV_LORA, B_HEADS * QK_NOPE).astype(BF16)
    wvt = kvb[:, :, QK_NOPE:].reshape(KV_LORA, B_HEADS * V_HEAD).T.astype(BF16)
    wqa = b_w_q_a[0].astype(BF16)
    wqb = b_w_q_b[0].reshape(Q_LORA, B_HEADS, QK_NOPE + QK_ROPE)
    wqb = jnp.concatenate([wqb[:, :, :QK_NOPE], _spread_rope(wqb[:, :, QK_NOPE:])], axis=-1)
    wqb = wqb.reshape(Q_LORA, B_HEADS * QK_PAD).astype(BF16)
    wo = b_w_o[0].astype(BF16)

    inv_freq = ROPE_THETA ** (-jnp.arange(0, QK_ROPE, 2, dtype=F32) / QK_ROPE)
    group = jnp.ones((QK_ROPE // 2,), F32)
    rope_tab = jnp.zeros((8, LANES), F32)
    rope_tab = rope_tab.at[0].set(jnp.tile(inv_freq, 4))
    rope_tab = rope_tab.at[1].set(jnp.concatenate([0.0 * group, HALF_PI * group, 0.0 * group, HALF_PI * group]))
    rope_tab = rope_tab.at[2].set(jnp.concatenate([-group, 0.0 * group, group, 0.0 * group]))

    h = x.reshape(TOKENS, D_MODEL)
    h, w1, w2 = _gmlp(h, _row(norm_mix_g[0]), w_in, _row(a_ln_v_g[0]), _row(a_ln_v_b[0]), a_w_s[0],
                      a_b_s[0].reshape(A_GROUPS, GMLP_BLOCK, 1), w_out, mlp_w1, mlp_w2)
    h, cos_tab, sin_tab = _mlp_rope(h, _row(norm_mlp_g[0]), w1, w2, positions.reshape(TOKENS, 1),
                                    rope_tab, 0)

    h3 = h.reshape(BATCH, SEQ, D_MODEL)
    q, kn, kpe, vt = _proj(h3, cos_tab.reshape(BATCH, SEQ, LANES), sin_tab.reshape(BATCH, SEQ, LANES),
                           _row(kv_src_norm_g), _row(norm_mix_g[1]), wkva, _row(kv_a_norm_g), wk, wvt,
                           wqa, _row(b_q_norm_g[0]), wqb)
    h3 = _attn(h3, q, kn, kpe, vt, wo)

    h = _mlp_final(h3.reshape(TOKENS, D_MODEL), _row(norm_mlp_g[1]), w1, w2, _row(final_norm_g), 1)
    return h.reshape(BATCH, SEQ, D_MODEL)
```

```python
import functools

import jax
import jax.numpy as jnp
import numpy as np
from jax import lax
from jax.experimental import pallas as pl
from jax.experimental.pallas import tpu as pltpu

D_MODEL = 1024
BATCH = 8
SEQ = 2048
TOKENS = BATCH * SEQ
DEPTH = 2
CHUNK = 64
CHUNK_SHIFT = 6
GMLP_BLOCK = 128
GATE_DIM = 2 * D_MODEL
A_GROUPS = 8
A_GROUP_DIM = GATE_DIM // A_GROUPS
B_HEADS = 8
QK_NOPE = 128
QK_ROPE = 64
V_HEAD = 128
Q_LORA = 384
KV_LORA = 256
ROPE_THETA = 10000.0
D_FF = 4 * D_MODEL
EPS = 1e-6

LANES = 128
QK_PAD = 2 * LANES
VMEM_LIMIT = 56 * 1024 * 1024

TM = 512
TM_MLP = 1024
NCHUNK = 512
TQ = 256
TK = 256
Q_TILES_PER_STEP = 4
ONES_ROWS = 16

F32 = jnp.float32
BF16 = jnp.bfloat16
NEG = float(np.finfo(np.float32).min)
SQRT_HALF = float(np.sqrt(0.5))
LOG2_E = float(np.log2(np.e))
HALF_PI = float(np.pi / 2)
NT_DIMS = (((1,), (1,)), ((), ()))
TN_DIMS = (((0,), (0,)), ((), ()))


def _rms(x, g):
    ms = jnp.mean(x * x, axis=-1, keepdims=True)
    return x * lax.rsqrt(ms + EPS) * g


def _resident(shape):
    nd = len(shape)
    return pl.BlockSpec(shape, lambda *_: (0,) * nd, pipeline_mode=pl.Buffered(1))


def _params(*sem):
    return pltpu.CompilerParams(dimension_semantics=sem, vmem_limit_bytes=VMEM_LIMIT)


def _gelu(z):
    return 0.5 * z * (1.0 + lax.erf(z * SQRT_HALF))


def _gmlp_kernel(h_ref, g_ref, w_in_ref, lng_ref, lnb_ref, ws_ref, bs_ref, w_out_ref, w1f_ref, w2f_ref,
                 o_ref, w1b_ref, w2b_ref, v_scr, vn_scr, gated_scr, ws_scr):
    w1b_ref[...] = w1f_ref[...].astype(BF16)
    w2b_ref[...] = w2f_ref[...].astype(BF16)

    h = h_ref[...]
    hn = _rms(h, g_ref[...]).astype(BF16)
    n_half = GATE_DIM // NCHUNK
    row_blocks = [slice(r * GMLP_BLOCK, (r + 1) * GMLP_BLOCK) for r in range(TM // GMLP_BLOCK)]

    for c in range(n_half):
        w_cols = slice(GATE_DIM + c * NCHUNK, GATE_DIM + (c + 1) * NCHUNK)
        v_scr[:, c * NCHUNK:(c + 1) * NCHUNK] = _gelu(
            jnp.dot(hn, w_in_ref[:, w_cols], preferred_element_type=F32))

    ii = lax.broadcasted_iota(jnp.int32, (GMLP_BLOCK, GMLP_BLOCK), 0) >> CHUNK_SHIFT
    jj = lax.broadcasted_iota(jnp.int32, (GMLP_BLOCK, GMLP_BLOCK), 1) >> CHUNK_SHIFT
    for g in range(A_GROUPS):
        ws_scr[g] = jnp.where(jj <= ii, ws_ref[g], 0.0).astype(BF16)

    for rows in row_blocks:
        v = v_scr[rows, :]
        mu = jnp.mean(v, axis=-1, keepdims=True)
        var = jnp.mean(jnp.square(v - mu), axis=-1, keepdims=True)
        vn_scr[rows, :] = ((v - mu) * lax.rsqrt(var + EPS) * lng_ref[...] + lnb_ref[...]).astype(BF16)

    for c in range(n_half):
        u = _gelu(jnp.dot(hn, w_in_ref[:, c * NCHUNK:(c + 1) * NCHUNK], preferred_element_type=F32))
        for gl in range(NCHUNK // A_GROUP_DIM):
            g = c * (NCHUNK // A_GROUP_DIM) + gl
            cols = slice(g * A_GROUP_DIM, (g + 1) * A_GROUP_DIM)
            for rows in row_blocks:
                sv = jnp.dot(ws_scr[g], vn_scr[rows, cols], preferred_element_type=F32) + bs_ref[g]
                u_blk = u[rows, gl * A_GROUP_DIM:(gl + 1) * A_GROUP_DIM]
                gated_scr[rows, cols] = (u_blk * sv).astype(BF16)

    o_ref[...] = h + jnp.dot(gated_scr[...], w_out_ref[...], preferred_element_type=F32)


def _gmlp(h, g, w_in, lng, lnb, ws, bs, w_out, w1f, w2f):
    steps = TOKENS // TM
    tok = pl.BlockSpec((TM, D_MODEL), lambda t: (t, 0))
    w1_slab = pl.BlockSpec((DEPTH, D_MODEL // steps, D_FF), lambda t: (0, t, 0))
    w2_slab = pl.BlockSpec((DEPTH, D_FF // steps, D_MODEL), lambda t: (0, t, 0))
    return pl.pallas_call(
        _gmlp_kernel,
        grid=(steps,),
        in_specs=[tok, _resident((1, D_MODEL)), _resident((D_MODEL, 2 * GATE_DIM)),
                  _resident((1, GATE_DIM)), _resident((1, GATE_DIM)),
                  _resident((A_GROUPS, GMLP_BLOCK, GMLP_BLOCK)),
                  _resident((A_GROUPS, GMLP_BLOCK, 1)), _resident((GATE_DIM, D_MODEL)),
                  w1_slab, w2_slab],
        out_specs=[tok, w1_slab, w2_slab],
        out_shape=[jax.ShapeDtypeStruct((TOKENS, D_MODEL), F32),
                   jax.ShapeDtypeStruct((DEPTH, D_MODEL, D_FF), BF16),
                   jax.ShapeDtypeStruct((DEPTH, D_FF, D_MODEL), BF16)],
        scratch_shapes=[pltpu.VMEM((TM, GATE_DIM), F32), pltpu.VMEM((TM, GATE_DIM), BF16),
                        pltpu.VMEM((TM, GATE_DIM), BF16),
                        pltpu.VMEM((A_GROUPS, GMLP_BLOCK, GMLP_BLOCK), BF16)],
        compiler_params=_params("arbitrary"),
        name="gmlp_mixer",
    )(h, g, w_in, lng, lnb, ws, bs, w_out, w1f, w2f)


def _mlp_residual(h, g_ref, w1_ref, w2_ref, a_scr, anchors=None):
    hn = _rms(h, g_ref[...]).astype(BF16)
    n_chunks = D_FF // NCHUNK
    slab = h.shape[0] // n_chunks
    for c in range(n_chunks):
        cols = slice(c * NCHUNK, (c + 1) * NCHUNK)
        a = jnp.maximum(jnp.dot(hn, w1_ref[:, cols], preferred_element_type=F32), 0.0)
        if anchors is None:
            a_scr[:, cols] = (a * a).astype(BF16)
        else:
            for r in range(n_chunks):
                rows = slice(r * slab, (r + 1) * slab)
                blk = a[rows] + anchors[c] if r == c else a[rows]
                a_scr[rows, cols] = (blk * blk).astype(BF16)
    return h + jnp.dot(a_scr[...], w2_ref[...], preferred_element_type=F32)


def _mlp_rope_kernel(h_ref, g_ref, w1_ref, w2_ref, pos_ref, rope_ref, o_ref, cos_ref, sin_ref, a_scr):
    n_slabs = D_FF // NCHUNK
    slab = TM_MLP // n_slabs
    anchors = []
    for c in range(n_slabs):
        rows = slice(c * slab, (c + 1) * slab)
        sin_cos = jnp.sin(pos_ref[rows, :].astype(F32) * rope_ref[0:1, :] + rope_ref[1:2, :])
        cos_ref[rows, :] = pltpu.roll(sin_cos, LANES - QK_ROPE // 2, 1)
        sin_ref[rows, :] = sin_cos * rope_ref[2:3, :]
        anchors.append(sin_cos[:, 0:1] * 0.0)
    o_ref[...] = _mlp_residual(h_ref[...], g_ref, w1_ref, w2_ref, a_scr, anchors)


def _mlp_final_kernel(h_ref, g_ref, w1_ref, w2_ref, gf_ref, o_ref, a_scr):
    o_ref[...] = _rms(_mlp_residual(h_ref[...], g_ref, w1_ref, w2_ref, a_scr), gf_ref[...])


def _mlp_specs(layer):
    tok = pl.BlockSpec((TM_MLP, D_MODEL), lambda t: (t, 0))
    w1 = pl.BlockSpec((None, D_MODEL, D_FF), lambda t: (layer, 0, 0), pipeline_mode=pl.Buffered(1))
    w2 = pl.BlockSpec((None, D_FF, D_MODEL), lambda t: (layer, 0, 0), pipeline_mode=pl.Buffered(1))
    return tok, [tok, _resident((1, D_MODEL)), w1, w2]


def _mlp_rope(h, g, w1, w2, pos, rope_tab, layer):
    tok, in_specs = _mlp_specs(layer)
    tab = pl.BlockSpec((TM_MLP, LANES), lambda t: (t, 0))
    return pl.pallas_call(
        _mlp_rope_kernel,
        grid=(TOKENS // TM_MLP,),
        in_specs=in_specs + [pl.BlockSpec((TM_MLP, 1), lambda t: (t, 0)), _resident((8, LANES))],
        out_specs=[tok, tab, tab],
        out_shape=[jax.ShapeDtypeStruct((TOKENS, D_MODEL), F32),
                   jax.ShapeDtypeStruct((TOKENS, LANES), F32),
                   jax.ShapeDtypeStruct((TOKENS, LANES), F32)],
        scratch_shapes=[pltpu.VMEM((TM_MLP, D_FF), BF16)],
        compiler_params=_params("arbitrary"),
        name="mlp_rope",
    )(h, g, w1, w2, pos, rope_tab)


def _mlp_final(h, g, w1, w2, gf, layer):
    tok, in_specs = _mlp_specs(layer)
    return pl.pallas_call(
        _mlp_final_kernel,
        grid=(TOKENS // TM_MLP,),
        in_specs=in_specs + [_resident((1, D_MODEL))],
        out_specs=tok,
        out_shape=jax.ShapeDtypeStruct((TOKENS, D_MODEL), F32),
        scratch_shapes=[pltpu.VMEM((TM_MLP, D_FF), BF16)],
        compiler_params=_params("arbitrary"),
        name="mlp_final",
    )(h, g, w1, w2, gf)


def _rope(x, cos, sin_signed):
    return x * cos + pltpu.roll(x, LANES // 2, 1) * sin_signed


def _proj_kernel(h_ref, cos_ref, sin_ref, gkv_ref, gq_ref, wkva_ref, gkva_ref, wk_ref, wvt_ref,
                 wqa_ref, gqa_ref, wqb_ref, q_ref, kn_ref, kpe_ref, vt_ref):
    h = h_ref[0]
    y = h * lax.rsqrt(jnp.mean(h * h, axis=-1, keepdims=True) + EPS)
    cos = cos_ref[0]
    sin_signed = sin_ref[0]

    ckv = jnp.dot((y * gkv_ref[...]).astype(BF16), wkva_ref[...], preferred_element_type=F32)
    c_kv = _rms(ckv[:, :KV_LORA], gkva_ref[...]).astype(BF16)
    kpe_ref[0] = _rope(ckv[:, KV_LORA:], cos, sin_signed).astype(BF16)
    kn = jnp.dot(c_kv, wk_ref[...], preferred_element_type=F32).astype(BF16)
    for hd in range(B_HEADS):
        kn_ref[0, hd] = kn[:, hd * QK_NOPE:(hd + 1) * QK_NOPE]
    vt = lax.dot_general(wvt_ref[...], c_kv, NT_DIMS, preferred_element_type=F32).astype(BF16)
    for kb in range(TM // TK):
        vt_ref[0, kb] = vt[:, kb * TK:(kb + 1) * TK]

    scale = (QK_NOPE + QK_ROPE) ** -0.5 * LOG2_E
    cq = jnp.dot((y * gq_ref[...]).astype(BF16), wqa_ref[...], preferred_element_type=F32)
    cq = _rms(cq, gqa_ref[...] * scale).astype(BF16)
    for hd in range(B_HEADS):
        q = jnp.dot(cq, wqb_ref[:, hd * QK_PAD:(hd + 1) * QK_PAD], preferred_element_type=F32)
        q_ref[0, hd, :, :QK_NOPE] = q[:, :QK_NOPE].astype(BF16)
        q_ref[0, hd, :, QK_NOPE:] = _rope(q[:, QK_NOPE:], cos, sin_signed).astype(BF16)


def _proj(h, cos_tab, sin_tab, gkv, gq, wkva, gkva, wk, wvt, wqa, gqa, wqb):
    n_t = SEQ // TM
    tab = pl.BlockSpec((1, TM, LANES), lambda b, t: (b, t, 0))
    return pl.pallas_call(
        _proj_kernel,
        grid=(BATCH, n_t),
        in_specs=[pl.BlockSpec((1, TM, D_MODEL), lambda b, t: (b, t, 0)), tab, tab,
                  _resident((1, D_MODEL)), _resident((1, D_MODEL)),
                  _resident((D_MODEL, KV_LORA + LANES)), _resident((1, KV_LORA)),
                  _resident((KV_LORA, B_HEADS * QK_NOPE)), _resident((B_HEADS * V_HEAD, KV_LORA)),
                  _resident((D_MODEL, Q_LORA)), _resident((1, Q_LORA)),
                  _resident((Q_LORA, B_HEADS * QK_PAD))],
        out_specs=[pl.BlockSpec((1, B_HEADS, TM, QK_PAD), lambda b, t: (b, 0, t, 0)),
                   pl.BlockSpec((1, B_HEADS, TM, QK_NOPE), lambda b, t: (b, 0, t, 0)),
                   pl.BlockSpec((1, TM, LANES), lambda b, t: (b, t, 0)),
                   pl.BlockSpec((1, TM // TK, B_HEADS * V_HEAD, TK), lambda b, t: (b, t, 0, 0))],
        out_shape=[jax.ShapeDtypeStruct((BATCH, B_HEADS, SEQ, QK_PAD), BF16),
                   jax.ShapeDtypeStruct((BATCH, B_HEADS, SEQ, QK_NOPE), BF16),
                   jax.ShapeDtypeStruct((BATCH, SEQ, LANES), BF16),
                   jax.ShapeDtypeStruct((BATCH, SEQ // TK, B_HEADS * V_HEAD, TK), BF16)],
        compiler_params=_params("arbitrary", "arbitrary"),
        name="qkv_proj",
    )(h, cos_tab, sin_tab, gkv, gq, wkva, gkva, wk, wvt, wqa, gqa, wqb)


def _attn_kernel(h_ref, q_ref, kn_ref, kpe_ref, vt_ref, wo_ref, o_ref, sa_scr, sb_scr, mta_scr, mtb_scr,
                 m_scr, acc_scr, ot_scr):
    k = pl.program_id(1)
    kc_id = lax.broadcasted_iota(jnp.int32, (TK, TQ), 0) >> CHUNK_SHIFT
    qc_id = lax.broadcasted_iota(jnp.int32, (TK, TQ), 1) >> CHUNK_SHIFT
    diag_ok = kc_id <= qc_id
    ones_rows = jnp.ones((ONES_ROWS, TK), BF16)

    def score_stage(sub, j, s_scr, mt_scr, diagonal):
        rows = pl.ds(pl.multiple_of(j * TK, TK), TK)
        k_pe = kpe_ref[0, rows, :]
        for hd in range(B_HEADS):
            k_cat = jnp.concatenate([kn_ref[0, hd, rows, :], k_pe], axis=-1)
            q_h = q_ref[0, hd, sub * TQ:(sub + 1) * TQ, :]
            s = lax.dot_general(k_cat, q_h, NT_DIMS, preferred_element_type=F32)
            if diagonal:
                s = jnp.where(diag_ok, s, NEG)
            s_scr[hd] = s
            mt_scr[hd] = jnp.max(s, axis=0, keepdims=True)

    def softmax_stage(sub, j, s_scr, mt_scr, first):
        slot = sub % 2
        for hd in range(B_HEADS):
            v_aug = jnp.concatenate([vt_ref[0, j, hd * V_HEAD:(hd + 1) * V_HEAD, :], ones_rows], axis=0)
            if first:
                m_new = mt_scr[hd]
                p = jnp.exp2(s_scr[hd] - m_new)
                acc_scr[slot, hd] = jnp.dot(v_aug, p.astype(BF16), preferred_element_type=F32)
            else:
                m = m_scr[slot, hd]
                m_new = jnp.maximum(m, mt_scr[hd])
                alpha = jnp.exp2(m - m_new)
                p = jnp.exp2(s_scr[hd] - m_new)
                acc_scr[slot, hd] = alpha * acc_scr[slot, hd] + jnp.dot(v_aug, p.astype(BF16),
                                                                        preferred_element_type=F32)
            m_scr[slot, hd] = m_new

    def first_tiles(sub, i):
        score_stage(sub, i, sb_scr, mtb_scr, True)
        score_stage(sub, 0, sa_scr, mta_scr, False)
        softmax_stage(sub, i, sb_scr, mtb_scr, True)

    def tile_pairs(sub, n_pairs):
        def tile_pair(u, _):
            j = 2 * u
            score_stage(sub, j + 1, sb_scr, mtb_scr, False)
            softmax_stage(sub, j, sa_scr, mta_scr, False)
            score_stage(sub, j + 2, sa_scr, mta_scr, False)
            softmax_stage(sub, j + 1, sb_scr, mtb_scr, False)
            return 0

        lax.fori_loop(0, n_pairs, tile_pair, 0)

    def last_tiles_even(sub, i):
        score_stage(sub, i - 1, sb_scr, mtb_scr, False)
        softmax_stage(sub, i - 2, sa_scr, mta_scr, False)
        softmax_stage(sub, i - 1, sb_scr, mtb_scr, False)

    def project(sub):
        slot = sub % 2
        q_rows = slice(sub * TQ, (sub + 1) * TQ)
        for hd in range(B_HEADS):
            inv_l = 1.0 / acc_scr[slot, hd, V_HEAD:V_HEAD + 1, :]
            ot_scr[slot, hd * V_HEAD:(hd + 1) * V_HEAD, :] = (acc_scr[slot, hd, :V_HEAD, :] * inv_l).astype(BF16)
        out = lax.dot_general(ot_scr[slot], wo_ref[...], TN_DIMS, preferred_element_type=F32)
        o_ref[0, q_rows, :] = h_ref[0, q_rows, :] + out

    for sub in range(Q_TILES_PER_STEP):
        i = Q_TILES_PER_STEP * k + sub
        if sub > 0:
            project(sub - 1)
        first_tiles(sub, i)
        tile_pairs(sub, jnp.maximum(i - 1, 0) >> 1)
        if sub % 2 == 1:
            softmax_stage(sub, i - 1, sa_scr, mta_scr, False)
        elif sub == 0:
            pl.when(k > 0)(functools.partial(last_tiles_even, sub, i))
        else:
            last_tiles_even(sub, i)
    project(Q_TILES_PER_STEP - 1)


def _attn(h, q, kn, kpe, vt, wo):
    tq2 = Q_TILES_PER_STEP * TQ
    return pl.pallas_call(
        _attn_kernel,
        grid=(BATCH, SEQ // tq2),
        in_specs=[pl.BlockSpec((1, tq2, D_MODEL), lambda b, k: (b, k, 0)),
                  pl.BlockSpec((1, B_HEADS, tq2, QK_PAD), lambda b, k: (b, 0, k, 0)),
                  pl.BlockSpec((1, B_HEADS, SEQ, QK_NOPE), lambda b, k: (b, 0, 0, 0)),
                  pl.BlockSpec((1, SEQ, LANES), lambda b, k: (b, 0, 0)),
                  pl.BlockSpec((1, SEQ // TK, B_HEADS * V_HEAD, TK), lambda b, k: (b, 0, 0, 0)),
                  _resident((B_HEADS * V_HEAD, D_MODEL))],
        out_specs=pl.BlockSpec((1, tq2, D_MODEL), lambda b, k: (b, k, 0)),
        out_shape=jax.ShapeDtypeStruct((BATCH, SEQ, D_MODEL), F32),
        scratch_shapes=[pltpu.VMEM((B_HEADS, TK, TQ), F32), pltpu.VMEM((B_HEADS, TK, TQ), F32),
                        pltpu.VMEM((B_HEADS, 1, TQ), F32), pltpu.VMEM((B_HEADS, 1, TQ), F32),
                        pltpu.VMEM((2, B_HEADS, 1, TQ), F32),
                        pltpu.VMEM((2, B_HEADS, V_HEAD + ONES_ROWS, TQ), F32),
                        pltpu.VMEM((2, B_HEADS * V_HEAD, TQ), BF16)],
        compiler_params=_params("arbitrary", "arbitrary"),
        name="mla_attention",
    )(h, q, kn, kpe, vt, wo)


def _row(v):
    return v.reshape(1, -1)


def _spread_rope(w):
    half = QK_ROPE // 2
    zeros = jnp.zeros(w.shape[:-1] + (half,), w.dtype)
    return jnp.concatenate([w[..., :half], zeros, w[..., half:], zeros], axis=-1)


def kernel(x, positions, norm_mix_g, norm_mlp_g, a_w_in, a_ln_v_g, a_ln_v_b, a_w_s, a_b_s, a_w_out,
           b_w_q_a, b_q_norm_g, b_w_q_b, b_w_o, kv_src_norm_g, kv_w_a, kv_a_norm_g, kv_w_b,
           mlp_w1, mlp_w2, final_norm_g):
    w_in = a_w_in[0].astype(BF16)
    w_out = a_w_out[0].astype(BF16)
    wkva = jnp.concatenate([kv_w_a[:, :KV_LORA], _spread_rope(kv_w_a[:, KV_LORA:])], axis=-1).astype(BF16)
    kvb = kv_w_b.reshape(KV_LORA, B_HEADS, QK_NOPE + V_HEAD)
    wk = kvb[:, :, :QK_NOPE].reshape(KV_LORA, B_HEADS * QK_NOPE).astype(BF16)
    wvt = kvb[:, :, QK_NOPE:].reshape(KV_LORA, B_HEADS * V_HEAD).T.astype(BF16)
    wqa = b_w_q_a[0].astype(BF16)
    wqb = b_w_q_b[0].reshape(Q_LORA, B_HEADS, QK_NOPE + QK_ROPE)
    wqb = jnp.concatenate([wqb[:, :, :QK_NOPE], _spread_rope(wqb[:, :, QK_NOPE:])], axis=-1)
    wqb = wqb.reshape(Q_LORA, B_HEADS * QK_PAD).astype(BF16)
    wo = b_w_o[0].astype(BF16)

    inv_freq = ROPE_THETA ** (-jnp.arange(0, QK_ROPE, 2, dtype=F32) / QK_ROPE)
    group = jnp.ones((QK_ROPE // 2,), F32)
    rope_tab = jnp.zeros((8, LANES), F32)
    rope_tab = rope_tab.at[0].set(jnp.tile(inv_freq, 4))
    rope_tab = rope_tab.at[1].set(jnp.concatenate([0.0 * group, HALF_PI * group, 0.0 * group, HALF_PI * group]))
    rope_tab = rope_tab.at[2].set(jnp.concatenate([-group, 0.0 * group, group, 0.0 * group]))

    h = x.reshape(TOKENS, D_MODEL)
    h, w1, w2 = _gmlp(h, _row(norm_mix_g[0]), w_in, _row(a_ln_v_g[0]), _row(a_ln_v_b[0]), a_w_s[0],
                      a_b_s[0].reshape(A_GROUPS, GMLP_BLOCK, 1), w_out, mlp_w1, mlp_w2)
    h, cos_tab, sin_tab = _mlp_rope(h, _row(norm_mlp_g[0]), w1, w2, positions.reshape(TOKENS, 1),
                                    rope_tab, 0)

    h3 = h.reshape(BATCH, SEQ, D_MODEL)
    q, kn, kpe, vt = _proj(h3, cos_tab.reshape(BATCH, SEQ, LANES), sin_tab.reshape(BATCH, SEQ, LANES),
                           _row(kv_src_norm_g), _row(norm_mix_g[1]), wkva, _row(kv_a_norm_g), wk, wvt,
                           wqa, _row(b_q_norm_g[0]), wqb)
    h3 = _attn(h3, q, kn, kpe, vt, wo)

    h = _mlp_final(h3.reshape(TOKENS, D_MODEL), _row(norm_mlp_g[1]), w1, w2, _row(final_norm_g), 1)
    return h.reshape(BATCH, SEQ, D_MODEL)
```

```python
import functools

import jax
import jax.numpy as jnp
import numpy as np
from jax import lax
from jax.experimental import pallas as pl
from jax.experimental.pallas import tpu as pltpu

D_MODEL = 1024
BATCH = 8
SEQ = 2048
TOKENS = BATCH * SEQ
DEPTH = 2
CHUNK = 64
CHUNK_SHIFT = 6
GMLP_BLOCK = 128
GATE_DIM = 2 * D_MODEL
A_GROUPS = 8
A_GROUP_DIM = GATE_DIM // A_GROUPS
B_HEADS = 8
QK_NOPE = 128
QK_ROPE = 64
V_HEAD = 128
Q_LORA = 384
KV_LORA = 256
ROPE_THETA = 10000.0
D_FF = 4 * D_MODEL
EPS = 1e-6

LANES = 128
QK_PAD = 2 * LANES
VMEM_LIMIT = 56 * 1024 * 1024

TM = 512
TM_MLP = 1024
NCHUNK = 512
TQ = 256
TK = 256
Q_TILES_PER_STEP = 4
ONES_ROWS = 16

F32 = jnp.float32
BF16 = jnp.bfloat16
NEG = float(np.finfo(np.float32).min)
SQRT_HALF = float(np.sqrt(0.5))
LOG2_E = float(np.log2(np.e))
HALF_PI = float(np.pi / 2)
NT_DIMS = (((1,), (1,)), ((), ()))
TN_DIMS = (((0,), (0,)), ((), ()))


def _rms(x, g):
    ms = jnp.mean(x * x, axis=-1, keepdims=True)
    return x * lax.rsqrt(ms + EPS) * g


def _resident(shape):
    nd = len(shape)
    return pl.BlockSpec(shape, lambda *_: (0,) * nd, pipeline_mode=pl.Buffered(1))


def _params(*sem):
    return pltpu.CompilerParams(dimension_semantics=sem, vmem_limit_bytes=VMEM_LIMIT)


def _gelu(z):
    return 0.5 * z * (1.0 + lax.erf(z * SQRT_HALF))


def _gmlp_kernel(h_ref, g_ref, w_in_ref, lng_ref, lnb_ref, ws_ref, bs_ref, w_out_ref, w1f_ref, w2f_ref,
                 o_ref, w1b_ref, w2b_ref, v_scr, vn_scr, gated_scr, ws_scr):
    w1b_ref[...] = w1f_ref[...].astype(BF16)
    w2b_ref[...] = w2f_ref[...].astype(BF16)

    h = h_ref[...]
    hn = _rms(h, g_ref[...]).astype(BF16)
    n_half = GATE_DIM // NCHUNK
    row_blocks = [slice(r * GMLP_BLOCK, (r + 1) * GMLP_BLOCK) for r in range(TM // GMLP_BLOCK)]

    for c in range(n_half):
        w_cols = slice(GATE_DIM + c * NCHUNK, GATE_DIM + (c + 1) * NCHUNK)
        v_scr[:, c * NCHUNK:(c + 1) * NCHUNK] = _gelu(
            jnp.dot(hn, w_in_ref[:, w_cols], preferred_element_type=F32))

    ii = lax.broadcasted_iota(jnp.int32, (GMLP_BLOCK, GMLP_BLOCK), 0) >> CHUNK_SHIFT
    jj = lax.broadcasted_iota(jnp.int32, (GMLP_BLOCK, GMLP_BLOCK), 1) >> CHUNK_SHIFT
    for g in range(A_GROUPS):
        ws_scr[g] = jnp.where(jj <= ii, ws_ref[g], 0.0).astype(BF16)

    for rows in row_blocks:
        v = v_scr[rows, :]
        mu = jnp.mean(v, axis=-1, keepdims=True)
        var = jnp.mean(jnp.square(v - mu), axis=-1, keepdims=True)
        vn_scr[rows, :] = ((v - mu) * lax.rsqrt(var + EPS) * lng_ref[...] + lnb_ref[...]).astype(BF16)

    for c in range(n_half):
        u = _gelu(jnp.dot(hn, w_in_ref[:, c * NCHUNK:(c + 1) * NCHUNK], preferred_element_type=F32))
        for gl in range(NCHUNK // A_GROUP_DIM):
            g = c * (NCHUNK // A_GROUP_DIM) + gl
            cols = slice(g * A_GROUP_DIM, (g + 1) * A_GROUP_DIM)
            for rows in row_blocks:
                sv = jnp.dot(ws_scr[g], vn_scr[rows, cols], preferred_element_type=F32) + bs_ref[g]
                u_blk = u[rows, gl * A_GROUP_DIM:(gl + 1) * A_GROUP_DIM]
                gated_scr[rows, cols] = (u_blk * sv).astype(BF16)

    o_ref[...] = h + jnp.dot(gated_scr[...], w_out_ref[...], preferred_element_type=F32)


def _gmlp(h, g, w_in, lng, lnb, ws, bs, w_out, w1f, w2f):
    steps = TOKENS // TM
    tok = pl.BlockSpec((TM, D_MODEL), lambda t: (t, 0))
    w1_slab = pl.BlockSpec((DEPTH, D_MODEL // steps, D_FF), lambda t: (0, t, 0))
    w2_slab = pl.BlockSpec((DEPTH, D_FF // steps, D_MODEL), lambda t: (0, t, 0))
    return pl.pallas_call(
        _gmlp_kernel,
        grid=(steps,),
        in_specs=[tok, _resident((1, D_MODEL)), _resident((D_MODEL, 2 * GATE_DIM)),
                  _resident((1, GATE_DIM)), _resident((1, GATE_DIM)),
                  _resident((A_GROUPS, GMLP_BLOCK, GMLP_BLOCK)),
                  _resident((A_GROUPS, GMLP_BLOCK, 1)), _resident((GATE_DIM, D_MODEL)),
                  w1_slab, w2_slab],
        out_specs=[tok, w1_slab, w2_slab],
        out_shape=[jax.ShapeDtypeStruct((TOKENS, D_MODEL), F32),
                   jax.ShapeDtypeStruct((DEPTH, D_MODEL, D_FF), BF16),
                   jax.ShapeDtypeStruct((DEPTH, D_FF, D_MODEL), BF16)],
        scratch_shapes=[pltpu.VMEM((TM, GATE_DIM), F32), pltpu.VMEM((TM, GATE_DIM), BF16),
                        pltpu.VMEM((TM, GATE_DIM), BF16),
                        pltpu.VMEM((A_GROUPS, GMLP_BLOCK, GMLP_BLOCK), BF16)],
        compiler_params=_params("arbitrary"),
        name="gmlp_mixer",
    )(h, g, w_in, lng, lnb, ws, bs, w_out, w1f, w2f)


def _mlp_residual(h, g_ref, w1_ref, w2_ref, a_scr, anchors=None):
    hn = _rms(h, g_ref[...]).astype(BF16)
    n_chunks = D_FF // NCHUNK
    slab = h.shape[0] // n_chunks
    for c in range(n_chunks):
        cols = slice(c * NCHUNK, (c + 1) * NCHUNK)
        a = jnp.maximum(jnp.dot(hn, w1_ref[:, cols], preferred_element_type=F32), 0.0)
        if anchors is None:
            a_scr[:, cols] = (a * a).astype(BF16)
        else:
            for r in range(n_chunks):
                rows = slice(r * slab, (r + 1) * slab)
                blk = a[rows] + anchors[c] if r == c else a[rows]
                a_scr[rows, cols] = (blk * blk).astype(BF16)
    return h + jnp.dot(a_scr[...], w2_ref[...], preferred_element_type=F32)


def _mlp_rope_kernel(h_ref, g_ref, w1_ref, w2_ref, pos_ref, rope_ref, o_ref, cos_ref, sin_ref, a_scr):
    n_slabs = D_FF // NCHUNK
    slab = TM_MLP // n_slabs
    anchors = []
    for c in range(n_slabs):
        rows = slice(c * slab, (c + 1) * slab)
        sin_cos = jnp.sin(pos_ref[rows, :].astype(F32) * rope_ref[0:1, :] + rope_ref[1:2, :])
        cos_ref[rows, :] = pltpu.roll(sin_cos, LANES - QK_ROPE // 2, 1)
        sin_ref[rows, :] = sin_cos * rope_ref[2:3, :]
        anchors.append(sin_cos[:, 0:1] * 0.0)
    o_ref[...] = _mlp_residual(h_ref[...], g_ref, w1_ref, w2_ref, a_scr, anchors)


def _mlp_final_kernel(h_ref, g_ref, w1_ref, w2_ref, gf_ref, o_ref, a_scr):
    o_ref[...] = _rms(_mlp_residual(h_ref[...], g_ref, w1_ref, w2_ref, a_scr), gf_ref[...])


def _mlp_specs(layer):
    tok = pl.BlockSpec((TM_MLP, D_MODEL), lambda t: (t, 0))
    w1 = pl.BlockSpec((None, D_MODEL, D_FF), lambda t: (layer, 0, 0), pipeline_mode=pl.Buffered(1))
    w2 = pl.BlockSpec((None, D_FF, D_MODEL), lambda t: (layer, 0, 0), pipeline_mode=pl.Buffered(1))
    return tok, [tok, _resident((1, D_MODEL)), w1, w2]


def _mlp_rope(h, g, w1, w2, pos, rope_tab, layer):
    tok, in_specs = _mlp_specs(layer)
    tab = pl.BlockSpec((TM_MLP, LANES), lambda t: (t, 0))
    return pl.pallas_call(
        _mlp_rope_kernel,
        grid=(TOKENS // TM_MLP,),
        in_specs=in_specs + [pl.BlockSpec((TM_MLP, 1), lambda t: (t, 0)), _resident((8, LANES))],
        out_specs=[tok, tab, tab],
        out_shape=[jax.ShapeDtypeStruct((TOKENS, D_MODEL), F32),
                   jax.ShapeDtypeStruct((TOKENS, LANES), F32),
                   jax.ShapeDtypeStruct((TOKENS, LANES), F32)],
        scratch_shapes=[pltpu.VMEM((TM_MLP, D_FF), BF16)],
        compiler_params=_params("arbitrary"),
        name="mlp_rope",
    )(h, g, w1, w2, pos, rope_tab)


def _mlp_final(h, g, w1, w2, gf, layer):
    tok, in_specs = _mlp_specs(layer)
    return pl.pallas_call(
        _mlp_final_kernel,
        grid=(TOKENS // TM_MLP,),
        in_specs=in_specs + [_resident((1, D_MODEL))],
        out_specs=tok,
        out_shape=jax.ShapeDtypeStruct((TOKENS, D_MODEL), F32),
        scratch_shapes=[pltpu.VMEM((TM_MLP, D_FF), BF16)],
        compiler_params=_params("arbitrary"),
        name="mlp_final",
    )(h, g, w1, w2, gf)


def _rope(x, cos, sin_signed):
    return x * cos + pltpu.roll(x, LANES // 2, 1) * sin_signed


def _proj_kernel(h_ref, cos_ref, sin_ref, gkv_ref, gq_ref, wkva_ref, gkva_ref, wk_ref, wvt_ref,
                 wqa_ref, gqa_ref, wqb_ref, q_ref, kn_ref, kpe_ref, vt_ref):
    h = h_ref[0]
    y = h * lax.rsqrt(jnp.mean(h * h, axis=-1, keepdims=True) + EPS)
    cos = cos_ref[0]
    sin_signed = sin_ref[0]

    ckv = jnp.dot((y * gkv_ref[...]).astype(BF16), wkva_ref[...], preferred_element_type=F32)
    c_kv = _rms(ckv[:, :KV_LORA], gkva_ref[...]).astype(BF16)
    kpe_ref[0] = _rope(ckv[:, KV_LORA:], cos, sin_signed).astype(BF16)
    kn = jnp.dot(c_kv, wk_ref[...], preferred_element_type=F32).astype(BF16)
    for hd in range(B_HEADS):
        kn_ref[0, hd] = kn[:, hd * QK_NOPE:(hd + 1) * QK_NOPE]
    vt = lax.dot_general(wvt_ref[...], c_kv, NT_DIMS, preferred_element_type=F32).astype(BF16)
    for kb in range(TM // TK):
        vt_ref[0, kb] = vt[:, kb * TK:(kb + 1) * TK]

    scale = (QK_NOPE + QK_ROPE) ** -0.5 * LOG2_E
    cq = jnp.dot((y * gq_ref[...]).astype(BF16), wqa_ref[...], preferred_element_type=F32)
    cq = _rms(cq, gqa_ref[...] * scale).astype(BF16)
    for hd in range(B_HEADS):
        q = jnp.dot(cq, wqb_ref[:, hd * QK_PAD:(hd + 1) * QK_PAD], preferred_element_type=F32)
        q_ref[0, hd, :, :QK_NOPE] = q[:, :QK_NOPE].astype(BF16)
        q_ref[0, hd, :, QK_NOPE:] = _rope(q[:, QK_NOPE:], cos, sin_signed).astype(BF16)


def _proj(h, cos_tab, sin_tab, gkv, gq, wkva, gkva, wk, wvt, wqa, gqa, wqb):
    n_t = SEQ // TM
    tab = pl.BlockSpec((1, TM, LANES), lambda b, t: (b, t, 0))
    return pl.pallas_call(
        _proj_kernel,
        grid=(BATCH, n_t),
        in_specs=[pl.BlockSpec((1, TM, D_MODEL), lambda b, t: (b, t, 0)), tab, tab,
                  _resident((1, D_MODEL)), _resident((1, D_MODEL)),
                  _resident((D_MODEL, KV_LORA + LANES)), _resident((1, KV_LORA)),
                  _resident((KV_LORA, B_HEADS * QK_NOPE)), _resident((B_HEADS * V_HEAD, KV_LORA)),
                  _resident((D_MODEL, Q_LORA)), _resident((1, Q_LORA)),
                  _resident((Q_LORA, B_HEADS * QK_PAD))],
        out_specs=[pl.BlockSpec((1, B_HEADS, TM, QK_PAD), lambda b, t: (b, 0, t, 0)),
                   pl.BlockSpec((1, B_HEADS, TM, QK_NOPE), lambda b, t: (b, 0, t, 0)),
                   pl.BlockSpec((1, TM, LANES), lambda b, t: (b, t, 0)),
                   pl.BlockSpec((1, TM // TK, B_HEADS * V_HEAD, TK), lambda b, t: (b, t, 0, 0))],
        out_shape=[jax.ShapeDtypeStruct((BATCH, B_HEADS, SEQ, QK_PAD), BF16),
                   jax.ShapeDtypeStruct((BATCH, B_HEADS, SEQ, QK_NOPE), BF16),
                   jax.ShapeDtypeStruct((BATCH, SEQ, LANES), BF16),
                   jax.ShapeDtypeStruct((BATCH, SEQ // TK, B_HEADS * V_HEAD, TK), BF16)],
        compiler_params=_params("arbitrary", "arbitrary"),
        name="qkv_proj",
    )(h, cos_tab, sin_tab, gkv, gq, wkva, gkva, wk, wvt, wqa, gqa, wqb)


def _attn_kernel(h_ref, q_ref, kn_ref, kpe_ref, vt_ref, wo_ref, o_ref, sa_scr, sb_scr, mta_scr, mtb_scr,
                 m_scr, acc_scr, ot_scr):
    k = pl.program_id(1)
    kc_id = lax.broadcasted_iota(jnp.int32, (TK, TQ), 0) >> CHUNK_SHIFT
    qc_id = lax.broadcasted_iota(jnp.int32, (TK, TQ), 1) >> CHUNK_SHIFT
    diag_ok = kc_id <= qc_id
    ones_rows = jnp.ones((ONES_ROWS, TK), BF16)

    def score_head(hd, sub, j, s_scr, mt_scr, diagonal):
        rows = pl.ds(pl.multiple_of(j * TK, TK), TK)
        k_cat = jnp.concatenate([kn_ref[0, hd, rows, :], kpe_ref[0, rows, :]], axis=-1)
        q_h = q_ref[0, hd, sub * TQ:(sub + 1) * TQ, :]
        s = lax.dot_general(k_cat, q_h, NT_DIMS, preferred_element_type=F32)
        if diagonal:
            s = jnp.where(diag_ok, s, NEG)
        s_scr[hd] = s
        mt_scr[hd] = jnp.max(s, axis=0, keepdims=True)

    def softmax_head(hd, sub, j, s_scr, mt_scr, first):
        slot = sub % 2
        v_aug = jnp.concatenate([vt_ref[0, j, hd * V_HEAD:(hd + 1) * V_HEAD, :], ones_rows], axis=0)
        if first:
            m_new = mt_scr[hd]
            p = jnp.exp2(s_scr[hd] - m_new)
            acc_scr[slot, hd] = jnp.dot(v_aug, p.astype(BF16), preferred_element_type=F32)
        else:
            m = m_scr[slot, hd]
            m_new = jnp.maximum(m, mt_scr[hd])
            alpha = jnp.exp2(m - m_new)
            p = jnp.exp2(s_scr[hd] - m_new)
            acc_scr[slot, hd] = alpha * acc_scr[slot, hd] + jnp.dot(v_aug, p.astype(BF16),
                                                                    preferred_element_type=F32)
        m_scr[slot, hd] = m_new

    def score_stage(*args):
        for hd in range(B_HEADS):
            score_head(hd, *args)

    def softmax_stage(*args):
        for hd in range(B_HEADS):
            softmax_head(hd, *args)

    def paired_stage(score_args, softmax_args):
        for hd in range(B_HEADS):
            score_head(hd, *score_args)
            softmax_head(hd, *softmax_args)

    def first_tiles(sub, i):
        score_stage(sub, i, sb_scr, mtb_scr, True)
        paired_stage((sub, 0, sa_scr, mta_scr, False), (sub, i, sb_scr, mtb_scr, True))

    def tile_pairs(sub, n_pairs):
        def tile_pair(u, _):
            j = 2 * u
            paired_stage((sub, j + 1, sb_scr, mtb_scr, False), (sub, j, sa_scr, mta_scr, False))
            paired_stage((sub, j + 2, sa_scr, mta_scr, False), (sub, j + 1, sb_scr, mtb_scr, False))
            return 0

        lax.fori_loop(0, n_pairs, tile_pair, 0)

    def last_tiles_even(sub, i):
        paired_stage((sub, i - 1, sb_scr, mtb_scr, False), (sub, i - 2, sa_scr, mta_scr, False))
        softmax_stage(sub, i - 1, sb_scr, mtb_scr, False)

    def project(sub):
        slot = sub % 2
        q_rows = slice(sub * TQ, (sub + 1) * TQ)
        for hd in range(B_HEADS):
            inv_l = 1.0 / acc_scr[slot, hd, V_HEAD:V_HEAD + 1, :]
            ot_scr[slot, hd * V_HEAD:(hd + 1) * V_HEAD, :] = (acc_scr[slot, hd, :V_HEAD, :] * inv_l).astype(BF16)
        out = lax.dot_general(ot_scr[slot], wo_ref[...], TN_DIMS, preferred_element_type=F32)
        o_ref[0, q_rows, :] = h_ref[0, q_rows, :] + out

    for sub in range(Q_TILES_PER_STEP):
        i = Q_TILES_PER_STEP * k + sub
        if sub > 0:
            project(sub - 1)
        first_tiles(sub, i)
        tile_pairs(sub, jnp.maximum(i - 1, 0) >> 1)
        if sub % 2 == 1:
            softmax_stage(sub, i - 1, sa_scr, mta_scr, False)
        elif sub == 0:
            pl.when(k > 0)(functools.partial(last_tiles_even, sub, i))
        else:
            last_tiles_even(sub, i)
    project(Q_TILES_PER_STEP - 1)


def _attn(h, q, kn, kpe, vt, wo):
    tq2 = Q_TILES_PER_STEP * TQ
    return pl.pallas_call(
        _attn_kernel,
        grid=(BATCH, SEQ // tq2),
        in_specs=[pl.BlockSpec((1, tq2, D_MODEL), lambda b, k: (b, k, 0)),
                  pl.BlockSpec((1, B_HEADS, tq2, QK_PAD), lambda b, k: (b, 0, k, 0)),
                  pl.BlockSpec((1, B_HEADS, SEQ, QK_NOPE), lambda b, k: (b, 0, 0, 0)),
                  pl.BlockSpec((1, SEQ, LANES), lambda b, k: (b, 0, 0)),
                  pl.BlockSpec((1, SEQ // TK, B_HEADS * V_HEAD, TK), lambda b, k: (b, 0, 0, 0)),
                  _resident((B_HEADS * V_HEAD, D_MODEL))],
        out_specs=pl.BlockSpec((1, tq2, D_MODEL), lambda b, k: (b, k, 0)),
        out_shape=jax.ShapeDtypeStruct((BATCH, SEQ, D_MODEL), F32),
        scratch_shapes=[pltpu.VMEM((B_HEADS, TK, TQ), F32), pltpu.VMEM((B_HEADS, TK, TQ), F32),
                        pltpu.VMEM((B_HEADS, 1, TQ), F32), pltpu.VMEM((B_HEADS, 1, TQ), F32),
                        pltpu.VMEM((2, B_HEADS, 1, TQ), F32),
                        pltpu.VMEM((2, B_HEADS, V_HEAD + ONES_ROWS, TQ), F32),
                        pltpu.VMEM((2, B_HEADS * V_HEAD, TQ), BF16)],
        compiler_params=_params("arbitrary", "arbitrary"),
        name="mla_attention",
    )(h, q, kn, kpe, vt, wo)


def _row(v):
    return v.reshape(1, -1)


def _spread_rope(w):
    half = QK_ROPE // 2
    zeros = jnp.zeros(w.shape[:-1] + (half,), w.dtype)
    return jnp.concatenate([w[..., :half], zeros, w[..., half:], zeros], axis=-1)


def kernel(x, positions, norm_mix_g, norm_mlp_g, a_w_in, a_ln_v_g, a_ln_v_b, a_w_s, a_b_s, a_w_out,
           b_w_q_a, b_q_norm_g, b_w_q_b, b_w_o, kv_src_norm_g, kv_w_a, kv_a_norm_g, kv_w_b,
           mlp_w1, mlp_w2, final_norm_g):
    w_in = a_w_in[0].astype(BF16)
    w_out = a_w_out[0].astype(BF16)
    wkva = jnp.concatenate([kv_w_a[:, :KV_LORA], _spread_rope(kv_w_a[:, KV_LORA:])], axis=-1).astype(BF16)
    kvb = kv_w_b.reshape(KV_LORA, B_HEADS, QK_NOPE + V_HEAD)
    wk = kvb[:, :, :QK_NOPE].reshape(KV_LORA, B_HEADS * QK_NOPE).astype(BF16)
    wvt = kvb[:, :, QK_NOPE:].reshape(KV_LORA, B_HEADS * V_HEAD).T.astype(BF16)
    wqa = b_w_q_a[0].astype(BF16)
    wqb = b_w_q_b[0].reshape(Q_LORA, B_HEADS, QK_NOPE + QK_ROPE)
    wqb = jnp.concatenate([wqb[:, :, :QK_NOPE], _spread_rope(wqb[:, :, QK_NOPE:])], axis=-1)
    wqb = wqb.reshape(Q_LORA, B_HEADS * QK_PAD).astype(BF16)
    wo = b_w_o[0].astype(BF16)

    inv_freq = ROPE_THETA ** (-jnp.arange(0, QK_ROPE, 2, dtype=F32) / QK_ROPE)
    group = jnp.ones((QK_ROPE // 2,), F32)
    rope_tab = jnp.zeros((8, LANES), F32)
    rope_tab = rope_tab.at[0].set(jnp.tile(inv_freq, 4))
    rope_tab = rope_tab.at[1].set(jnp.concatenate([0.0 * group, HALF_PI * group, 0.0 * group, HALF_PI * group]))
    rope_tab = rope_tab.at[2].set(jnp.concatenate([-group, 0.0 * group, group, 0.0 * group]))

    h = x.reshape(TOKENS, D_MODEL)
    h, w1, w2 = _gmlp(h, _row(norm_mix_g[0]), w_in, _row(a_ln_v_g[0]), _row(a_ln_v_b[0]), a_w_s[0],
                      a_b_s[0].reshape(A_GROUPS, GMLP_BLOCK, 1), w_out, mlp_w1, mlp_w2)
    h, cos_tab, sin_tab = _mlp_rope(h, _row(norm_mlp_g[0]), w1, w2, positions.reshape(TOKENS, 1),
                                    rope_tab, 0)

    h3 = h.reshape(BATCH, SEQ, D_MODEL)
    q, kn, kpe, vt = _proj(h3, cos_tab.reshape(BATCH, SEQ, LANES), sin_tab.reshape(BATCH, SEQ, LANES),
                           _row(kv_src_norm_g), _row(norm_mix_g[1]), wkva, _row(kv_a_norm_g), wk, wvt,
                           wqa, _row(b_q_norm_g[0]), wqb)
    h3 = _attn(h3, q, kn, kpe, vt, wo)

    h = _mlp_final(h3.reshape(TOKENS, D_MODEL), _row(norm_mlp_g[1]), w1, w2, _row(final_norm_g), 1)
    return h.reshape(BATCH, SEQ, D_MODEL)
```

```python
import functools

import jax
import jax.numpy as jnp
import numpy as np
from jax import lax
from jax.experimental import pallas as pl
from jax.experimental.pallas import tpu as pltpu

D_MODEL = 1024
BATCH = 8
SEQ = 2048
TOKENS = BATCH * SEQ
DEPTH = 2
CHUNK = 64
CHUNK_SHIFT = 6
GMLP_BLOCK = 128
GATE_DIM = 2 * D_MODEL
A_GROUPS = 8
A_GROUP_DIM = GATE_DIM // A_GROUPS
B_HEADS = 8
QK_NOPE = 128
QK_ROPE = 64
V_HEAD = 128
Q_LORA = 384
KV_LORA = 256
ROPE_THETA = 10000.0
D_FF = 4 * D_MODEL
EPS = 1e-6

LANES = 128
QK_PAD = 2 * LANES
VMEM_LIMIT = 56 * 1024 * 1024

TM = 512
TM_MLP = 1024
NCHUNK = 512
TQ = 256
TK = 256
Q_TILES_PER_STEP = 4
ONES_ROWS = 16

F32 = jnp.float32
BF16 = jnp.bfloat16
NEG = float(np.finfo(np.float32).min)
SQRT_HALF = float(np.sqrt(0.5))
LOG2_E = float(np.log2(np.e))
HALF_PI = float(np.pi / 2)
NT_DIMS = (((1,), (1,)), ((), ()))
TN_DIMS = (((0,), (0,)), ((), ()))


def _rms(x, g):
    ms = jnp.mean(x * x, axis=-1, keepdims=True)
    return x * lax.rsqrt(ms + EPS) * g


def _resident(shape):
    nd = len(shape)
    return pl.BlockSpec(shape, lambda *_: (0,) * nd, pipeline_mode=pl.Buffered(1))


def _params(*sem):
    return pltpu.CompilerParams(dimension_semantics=sem, vmem_limit_bytes=VMEM_LIMIT)


def _gelu(z):
    return 0.5 * z * (1.0 + lax.erf(z * SQRT_HALF))


def _gmlp_kernel(h_ref, g_ref, w_in_ref, lng_ref, lnb_ref, ws_ref, bs_ref, w_out_ref, w1f_ref, w2f_ref,
                 o_ref, w1b_ref, w2b_ref, v_scr, vn_scr, gated_scr, ws_scr):
    w1b_ref[...] = w1f_ref[...].astype(BF16)
    w2b_ref[...] = w2f_ref[...].astype(BF16)

    h = h_ref[...]
    hn = _rms(h, g_ref[...]).astype(BF16)
    n_half = GATE_DIM // NCHUNK
    row_blocks = [slice(r * GMLP_BLOCK, (r + 1) * GMLP_BLOCK) for r in range(TM // GMLP_BLOCK)]

    for c in range(n_half):
        w_cols = slice(GATE_DIM + c * NCHUNK, GATE_DIM + (c + 1) * NCHUNK)
        v_scr[:, c * NCHUNK:(c + 1) * NCHUNK] = _gelu(
            jnp.dot(hn, w_in_ref[:, w_cols], preferred_element_type=F32))

    ii = lax.broadcasted_iota(jnp.int32, (GMLP_BLOCK, GMLP_BLOCK), 0) >> CHUNK_SHIFT
    jj = lax.broadcasted_iota(jnp.int32, (GMLP_BLOCK, GMLP_BLOCK), 1) >> CHUNK_SHIFT
    for g in range(A_GROUPS):
        ws_scr[g] = jnp.where(jj <= ii, ws_ref[g], 0.0).astype(BF16)

    for rows in row_blocks:
        v = v_scr[rows, :]
        mu = jnp.mean(v, axis=-1, keepdims=True)
        var = jnp.mean(jnp.square(v - mu), axis=-1, keepdims=True)
        vn_scr[rows, :] = ((v - mu) * lax.rsqrt(var + EPS) * lng_ref[...] + lnb_ref[...]).astype(BF16)

    for c in range(n_half):
        u = _gelu(jnp.dot(hn, w_in_ref[:, c * NCHUNK:(c + 1) * NCHUNK], preferred_element_type=F32))
        for gl in range(NCHUNK // A_GROUP_DIM):
            g = c * (NCHUNK // A_GROUP_DIM) + gl
            cols = slice(g * A_GROUP_DIM, (g + 1) * A_GROUP_DIM)
            for rows in row_blocks:
                sv = jnp.dot(ws_scr[g], vn_scr[rows, cols], preferred_element_type=F32) + bs_ref[g]
                u_blk = u[rows, gl * A_GROUP_DIM:(gl + 1) * A_GROUP_DIM]
                gated_scr[rows, cols] = (u_blk * sv).astype(BF16)

    o_ref[...] = h + jnp.dot(gated_scr[...], w_out_ref[...], preferred_element_type=F32)


def _gmlp(h, g, w_in, lng, lnb, ws, bs, w_out, w1f, w2f):
    steps = TOKENS // TM
    tok = pl.BlockSpec((TM, D_MODEL), lambda t: (t, 0))
    w1_slab = pl.BlockSpec((DEPTH, D_MODEL // steps, D_FF), lambda t: (0, t, 0))
    w2_slab = pl.BlockSpec((DEPTH, D_FF // steps, D_MODEL), lambda t: (0, t, 0))
    return pl.pallas_call(
        _gmlp_kernel,
        grid=(steps,),
        in_specs=[tok, _resident((1, D_MODEL)), _resident((D_MODEL, 2 * GATE_DIM)),
                  _resident((1, GATE_DIM)), _resident((1, GATE_DIM)),
                  _resident((A_GROUPS, GMLP_BLOCK, GMLP_BLOCK)),
                  _resident((A_GROUPS, GMLP_BLOCK, 1)), _resident((GATE_DIM, D_MODEL)),
                  w1_slab, w2_slab],
        out_specs=[tok, w1_slab, w2_slab],
        out_shape=[jax.ShapeDtypeStruct((TOKENS, D_MODEL), F32),
                   jax.ShapeDtypeStruct((DEPTH, D_MODEL, D_FF), BF16),
                   jax.ShapeDtypeStruct((DEPTH, D_FF, D_MODEL), BF16)],
        scratch_shapes=[pltpu.VMEM((TM, GATE_DIM), F32), pltpu.VMEM((TM, GATE_DIM), BF16),
                        pltpu.VMEM((TM, GATE_DIM), BF16),
                        pltpu.VMEM((A_GROUPS, GMLP_BLOCK, GMLP_BLOCK), BF16)],
        compiler_params=_params("arbitrary"),
        name="gmlp_mixer",
    )(h, g, w_in, lng, lnb, ws, bs, w_out, w1f, w2f)


def _mlp_residual(h, g_ref, w1_ref, w2_ref, a_scr, anchors=None):
    hn = _rms(h, g_ref[...]).astype(BF16)
    n_chunks = D_FF // NCHUNK
    slab = h.shape[0] // n_chunks
    for c in range(n_chunks):
        cols = slice(c * NCHUNK, (c + 1) * NCHUNK)
        a = jnp.maximum(jnp.dot(hn, w1_ref[:, cols], preferred_element_type=F32), 0.0)
        if anchors is None:
            a_scr[:, cols] = (a * a).astype(BF16)
        else:
            for r in range(n_chunks):
                rows = slice(r * slab, (r + 1) * slab)
                blk = a[rows] + anchors[c] if r == c else a[rows]
                a_scr[rows, cols] = (blk * blk).astype(BF16)
    return h + jnp.dot(a_scr[...], w2_ref[...], preferred_element_type=F32)


def _mlp_rope_kernel(h_ref, g_ref, w1_ref, w2_ref, pos_ref, rope_ref, o_ref, cos_ref, sin_ref, a_scr):
    half = QK_ROPE // 2
    n_slabs = D_FF // NCHUNK
    slab = TM_MLP // n_slabs
    pos = pos_ref[0].astype(F32)
    zeros = jnp.zeros((half, slab), F32)
    anchors = []
    for c in range(n_slabs):
        toks = slice(c * slab, (c + 1) * slab)
        sin_cos = jnp.sin(rope_ref[:QK_ROPE, :] * pos[:, toks] + rope_ref[QK_ROPE:, :])
        sin_t, cos_t = sin_cos[:half], sin_cos[half:]
        cos_tab = jnp.concatenate([cos_t, cos_t, cos_t, cos_t], axis=0).T
        sin_tab = jnp.concatenate([-sin_t, zeros, sin_t, zeros], axis=0).T
        cos_ref[toks, :] = cos_tab
        sin_ref[toks, :] = sin_tab
        anchors.append((cos_tab[:, 0:1] + sin_tab[:, 0:1]) * 0.0)
    o_ref[...] = _mlp_residual(h_ref[...], g_ref, w1_ref, w2_ref, a_scr, anchors)


def _mlp_final_kernel(h_ref, g_ref, w1_ref, w2_ref, gf_ref, o_ref, a_scr):
    o_ref[...] = _rms(_mlp_residual(h_ref[...], g_ref, w1_ref, w2_ref, a_scr), gf_ref[...])


def _mlp_specs(layer):
    tok = pl.BlockSpec((TM_MLP, D_MODEL), lambda t: (t, 0))
    w1 = pl.BlockSpec((None, D_MODEL, D_FF), lambda t: (layer, 0, 0), pipeline_mode=pl.Buffered(1))
    w2 = pl.BlockSpec((None, D_FF, D_MODEL), lambda t: (layer, 0, 0), pipeline_mode=pl.Buffered(1))
    return tok, [tok, _resident((1, D_MODEL)), w1, w2]


def _mlp_rope(h, g, w1, w2, pos, rope_tab, layer):
    tok, in_specs = _mlp_specs(layer)
    tab = pl.BlockSpec((TM_MLP, LANES), lambda t: (t, 0))
    return pl.pallas_call(
        _mlp_rope_kernel,
        grid=(TOKENS // TM_MLP,),
        in_specs=in_specs + [pl.BlockSpec((1, 1, TM_MLP), lambda t: (t, 0, 0)),
                             _resident((2 * QK_ROPE, LANES))],
        out_specs=[tok, tab, tab],
        out_shape=[jax.ShapeDtypeStruct((TOKENS, D_MODEL), F32),
                   jax.ShapeDtypeStruct((TOKENS, LANES), F32),
                   jax.ShapeDtypeStruct((TOKENS, LANES), F32)],
        scratch_shapes=[pltpu.VMEM((TM_MLP, D_FF), BF16)],
        compiler_params=_params("arbitrary"),
        name="mlp_rope",
    )(h, g, w1, w2, pos, rope_tab)


def _mlp_final(h, g, w1, w2, gf, layer):
    tok, in_specs = _mlp_specs(layer)
    return pl.pallas_call(
        _mlp_final_kernel,
        grid=(TOKENS // TM_MLP,),
        in_specs=in_specs + [_resident((1, D_MODEL))],
        out_specs=tok,
        out_shape=jax.ShapeDtypeStruct((TOKENS, D_MODEL), F32),
        scratch_shapes=[pltpu.VMEM((TM_MLP, D_FF), BF16)],
        compiler_params=_params("arbitrary"),
        name="mlp_final",
    )(h, g, w1, w2, gf)


def _rope(x, cos, sin_signed):
    return x * cos + pltpu.roll(x, LANES // 2, 1) * sin_signed


def _proj_kernel(h_ref, cos_ref, sin_ref, gkv_ref, gq_ref, wkva_ref, gkva_ref, wk_ref, wvt_ref,
                 wqa_ref, gqa_ref, wqb_ref, q_ref, kn_ref, kpe_ref, vt_ref):
    h = h_ref[0]
    y = h * lax.rsqrt(jnp.mean(h * h, axis=-1, keepdims=True) + EPS)
    cos = cos_ref[0]
    sin_signed = sin_ref[0]

    ckv = jnp.dot((y * gkv_ref[...]).astype(BF16), wkva_ref[...], preferred_element_type=F32)
    c_kv = _rms(ckv[:, :KV_LORA], gkva_ref[...]).astype(BF16)
    kpe_ref[0] = _rope(ckv[:, KV_LORA:], cos, sin_signed).astype(BF16)
    kn = jnp.dot(c_kv, wk_ref[...], preferred_element_type=F32).astype(BF16)
    for hd in range(B_HEADS):
        kn_ref[0, hd] = kn[:, hd * QK_NOPE:(hd + 1) * QK_NOPE]
    vt = lax.dot_general(wvt_ref[...], c_kv, NT_DIMS, preferred_element_type=F32).astype(BF16)
    for kb in range(TM // TK):
        vt_ref[0, kb] = vt[:, kb * TK:(kb + 1) * TK]

    scale = (QK_NOPE + QK_ROPE) ** -0.5 * LOG2_E
    cq = jnp.dot((y * gq_ref[...]).astype(BF16), wqa_ref[...], preferred_element_type=F32)
    cq = _rms(cq, gqa_ref[...] * scale).astype(BF16)
    for hd in range(B_HEADS):
        q = jnp.dot(cq, wqb_ref[:, hd * QK_PAD:(hd + 1) * QK_PAD], preferred_element_type=F32)
        q_ref[0, hd, :, :QK_NOPE] = q[:, :QK_NOPE].astype(BF16)
        q_ref[0, hd, :, QK_NOPE:] = _rope(q[:, QK_NOPE:], cos, sin_signed).astype(BF16)


def _proj(h, cos_tab, sin_tab, gkv, gq, wkva, gkva, wk, wvt, wqa, gqa, wqb):
    n_t = SEQ // TM
    tab = pl.BlockSpec((1, TM, LANES), lambda b, t: (b, t, 0))
    return pl.pallas_call(
        _proj_kernel,
        grid=(BATCH, n_t),
        in_specs=[pl.BlockSpec((1, TM, D_MODEL), lambda b, t: (b, t, 0)), tab, tab,
                  _resident((1, D_MODEL)), _resident((1, D_MODEL)),
                  _resident((D_MODEL, KV_LORA + LANES)), _resident((1, KV_LORA)),
                  _resident((KV_LORA, B_HEADS * QK_NOPE)), _resident((B_HEADS * V_HEAD, KV_LORA)),
                  _resident((D_MODEL, Q_LORA)), _resident((1, Q_LORA)),
                  _resident((Q_LORA, B_HEADS * QK_PAD))],
        out_specs=[pl.BlockSpec((1, B_HEADS, TM, QK_PAD), lambda b, t: (b, 0, t, 0)),
                   pl.BlockSpec((1, B_HEADS, TM, QK_NOPE), lambda b, t: (b, 0, t, 0)),
                   pl.BlockSpec((1, TM, LANES), lambda b, t: (b, t, 0)),
                   pl.BlockSpec((1, TM // TK, B_HEADS * V_HEAD, TK), lambda b, t: (b, t, 0, 0))],
        out_shape=[jax.ShapeDtypeStruct((BATCH, B_HEADS, SEQ, QK_PAD), BF16),
                   jax.ShapeDtypeStruct((BATCH, B_HEADS, SEQ, QK_NOPE), BF16),
                   jax.ShapeDtypeStruct((BATCH, SEQ, LANES), BF16),
                   jax.ShapeDtypeStruct((BATCH, SEQ // TK, B_HEADS * V_HEAD, TK), BF16)],
        compiler_params=_params("arbitrary", "arbitrary"),
        name="qkv_proj",
    )(h, cos_tab, sin_tab, gkv, gq, wkva, gkva, wk, wvt, wqa, gqa, wqb)


def _attn_kernel(h_ref, q_ref, kn_ref, kpe_ref, vt_ref, wo_ref, o_ref, sa_scr, sb_scr, mta_scr, mtb_scr,
                 m_scr, acc_scr, ot_scr):
    k = pl.program_id(1)
    kc_id = lax.broadcasted_iota(jnp.int32, (TK, TQ), 0) >> CHUNK_SHIFT
    qc_id = lax.broadcasted_iota(jnp.int32, (TK, TQ), 1) >> CHUNK_SHIFT
    diag_ok = kc_id <= qc_id
    ones_rows = jnp.ones((ONES_ROWS, TK), BF16)

    def score_head(hd, sub, j, s_scr, mt_scr, diagonal):
        rows = pl.ds(pl.multiple_of(j * TK, TK), TK)
        k_cat = jnp.concatenate([kn_ref[0, hd, rows, :], kpe_ref[0, rows, :]], axis=-1)
        q_h = q_ref[0, hd, sub * TQ:(sub + 1) * TQ, :]
        s = lax.dot_general(k_cat, q_h, NT_DIMS, preferred_element_type=F32)
        if diagonal:
            s = jnp.where(diag_ok, s, NEG)
        s_scr[hd] = s
        mt_scr[hd] = jnp.max(s, axis=0, keepdims=True)

    def softmax_head(hd, sub, j, s_scr, mt_scr, first):
        slot = sub % 2
        v_aug = jnp.concatenate([vt_ref[0, j, hd * V_HEAD:(hd + 1) * V_HEAD, :], ones_rows], axis=0)
        if first:
            m_new = mt_scr[hd]
            p = jnp.exp2(s_scr[hd] - m_new)
            acc_scr[slot, hd] = jnp.dot(v_aug, p.astype(BF16), preferred_element_type=F32)
        else:
            m = m_scr[slot, hd]
            m_new = jnp.maximum(m, mt_scr[hd])
            alpha = jnp.exp2(m - m_new)
            p = jnp.exp2(s_scr[hd] - m_new)
            acc_scr[slot, hd] = alpha * acc_scr[slot, hd] + jnp.dot(v_aug, p.astype(BF16),
                                                                    preferred_element_type=F32)
        m_scr[slot, hd] = m_new

    def score_stage(*args):
        for hd in range(B_HEADS):
            score_head(hd, *args)

    def softmax_stage(*args):
        for hd in range(B_HEADS):
            softmax_head(hd, *args)

    def paired_stage(score_args, softmax_args):
        for hd in range(B_HEADS):
            score_head(hd, *score_args)
            softmax_head(hd, *softmax_args)

    def first_tiles(sub, i):
        score_stage(sub, i, sb_scr, mtb_scr, True)
        paired_stage((sub, 0, sa_scr, mta_scr, False), (sub, i, sb_scr, mtb_scr, True))

    def tile_pairs(sub, n_pairs):
        def tile_pair(u, _):
            j = 2 * u
            paired_stage((sub, j + 1, sb_scr, mtb_scr, False), (sub, j, sa_scr, mta_scr, False))
            paired_stage((sub, j + 2, sa_scr, mta_scr, False), (sub, j + 1, sb_scr, mtb_scr, False))
            return 0

        lax.fori_loop(0, n_pairs, tile_pair, 0)

    def last_tiles_even(sub, i):
        paired_stage((sub, i - 1, sb_scr, mtb_scr, False), (sub, i - 2, sa_scr, mta_scr, False))
        softmax_stage(sub, i - 1, sb_scr, mtb_scr, False)

    def project(sub):
        slot = sub % 2
        q_rows = slice(sub * TQ, (sub + 1) * TQ)
        for hd in range(B_HEADS):
            inv_l = 1.0 / acc_scr[slot, hd, V_HEAD:V_HEAD + 1, :]
            ot_scr[slot, hd * V_HEAD:(hd + 1) * V_HEAD, :] = (acc_scr[slot, hd, :V_HEAD, :] * inv_l).astype(BF16)
        out = lax.dot_general(ot_scr[slot], wo_ref[...], TN_DIMS, preferred_element_type=F32)
        o_ref[0, q_rows, :] = h_ref[0, q_rows, :] + out

    for sub in range(Q_TILES_PER_STEP):
        i = Q_TILES_PER_STEP * k + sub
        if sub > 0:
            project(sub - 1)
        first_tiles(sub, i)
        tile_pairs(sub, jnp.maximum(i - 1, 0) >> 1)
        if sub % 2 == 1:
            softmax_stage(sub, i - 1, sa_scr, mta_scr, False)
        elif sub == 0:
            pl.when(k > 0)(functools.partial(last_tiles_even, sub, i))
        else:
            last_tiles_even(sub, i)
    project(Q_TILES_PER_STEP - 1)


def _attn(h, q, kn, kpe, vt, wo):
    tq2 = Q_TILES_PER_STEP * TQ
    return pl.pallas_call(
        _attn_kernel,
        grid=(BATCH, SEQ // tq2),
        in_specs=[pl.BlockSpec((1, tq2, D_MODEL), lambda b, k: (b, k, 0)),
                  pl.BlockSpec((1, B_HEADS, tq2, QK_PAD), lambda b, k: (b, 0, k, 0)),
                  pl.BlockSpec((1, B_HEADS, SEQ, QK_NOPE), lambda b, k: (b, 0, 0, 0)),
                  pl.BlockSpec((1, SEQ, LANES), lambda b, k: (b, 0, 0)),
                  pl.BlockSpec((1, SEQ // TK, B_HEADS * V_HEAD, TK), lambda b, k: (b, 0, 0, 0)),
                  _resident((B_HEADS * V_HEAD, D_MODEL))],
        out_specs=pl.BlockSpec((1, tq2, D_MODEL), lambda b, k: (b, k, 0)),
        out_shape=jax.ShapeDtypeStruct((BATCH, SEQ, D_MODEL), F32),
        scratch_shapes=[pltpu.VMEM((B_HEADS, TK, TQ), F32), pltpu.VMEM((B_HEADS, TK, TQ), F32),
                        pltpu.VMEM((B_HEADS, 1, TQ), F32), pltpu.VMEM((B_HEADS, 1, TQ), F32),
                        pltpu.VMEM((2, B_HEADS, 1, TQ), F32),
                        pltpu.VMEM((2, B_HEADS, V_HEAD + ONES_ROWS, TQ), F32),
                        pltpu.VMEM((2, B_HEADS * V_HEAD, TQ), BF16)],
        compiler_params=_params("arbitrary", "arbitrary"),
        name="mla_attention",
    )(h, q, kn, kpe, vt, wo)


def _row(v):
    return v.reshape(1, -1)


def _spread_rope(w):
    half = QK_ROPE // 2
    zeros = jnp.zeros(w.shape[:-1] + (half,), w.dtype)
    return jnp.concatenate([w[..., :half], zeros, w[..., half:], zeros], axis=-1)


def kernel(x, positions, norm_mix_g, norm_mlp_g, a_w_in, a_ln_v_g, a_ln_v_b, a_w_s, a_b_s, a_w_out,
           b_w_q_a, b_q_norm_g, b_w_q_b, b_w_o, kv_src_norm_g, kv_w_a, kv_a_norm_g, kv_w_b,
           mlp_w1, mlp_w2, final_norm_g):
    w_in = a_w_in[0].astype(BF16)
    w_out = a_w_out[0].astype(BF16)
    wkva = jnp.concatenate([kv_w_a[:, :KV_LORA], _spread_rope(kv_w_a[:, KV_LORA:])], axis=-1).astype(BF16)
    kvb = kv_w_b.reshape(KV_LORA, B_HEADS, QK_NOPE + V_HEAD)
    wk = kvb[:, :, :QK_NOPE].reshape(KV_LORA, B_HEADS * QK_NOPE).astype(BF16)
    wvt = kvb[:, :, QK_NOPE:].reshape(KV_LORA, B_HEADS * V_HEAD).T.astype(BF16)
    wqa = b_w_q_a[0].astype(BF16)
    wqb = b_w_q_b[0].reshape(Q_LORA, B_HEADS, QK_NOPE + QK_ROPE)
    wqb = jnp.concatenate([wqb[:, :, :QK_NOPE], _spread_rope(wqb[:, :, QK_NOPE:])], axis=-1)
    wqb = wqb.reshape(Q_LORA, B_HEADS * QK_PAD).astype(BF16)
    wo = b_w_o[0].astype(BF16)

    inv_freq = ROPE_THETA ** (-jnp.arange(0, QK_ROPE, 2, dtype=F32) / QK_ROPE)
    phase = jnp.concatenate([jnp.zeros((QK_ROPE // 2,), F32), jnp.full((QK_ROPE // 2,), HALF_PI, F32)])
    rope_tab = jnp.broadcast_to(jnp.concatenate([inv_freq, inv_freq, phase])[:, None], (2 * QK_ROPE, LANES))

    h = x.reshape(TOKENS, D_MODEL)
    h, w1, w2 = _gmlp(h, _row(norm_mix_g[0]), w_in, _row(a_ln_v_g[0]), _row(a_ln_v_b[0]), a_w_s[0],
                      a_b_s[0].reshape(A_GROUPS, GMLP_BLOCK, 1), w_out, mlp_w1, mlp_w2)
    h, cos_tab, sin_tab = _mlp_rope(h, _row(norm_mlp_g[0]), w1, w2,
                                    positions.reshape(TOKENS // TM_MLP, 1, TM_MLP),
                                    rope_tab, 0)

    h3 = h.reshape(BATCH, SEQ, D_MODEL)
    q, kn, kpe, vt = _proj(h3, cos_tab.reshape(BATCH, SEQ, LANES), sin_tab.reshape(BATCH, SEQ, LANES),
                           _row(kv_src_norm_g), _row(norm_mix_g[1]), wkva, _row(kv_a_norm_g), wk, wvt,
                           wqa, _row(b_q_norm_g[0]), wqb)
    h3 = _attn(h3, q, kn, kpe, vt, wo)

    h = _mlp_final(h3.reshape(TOKENS, D_MODEL), _row(norm_mlp_g[1]), w1, w2, _row(final_norm_g), 1)
    return h.reshape(BATCH, SEQ, D_MODEL)
```

```python
import functools

import jax
import jax.numpy as jnp
import numpy as np
from jax import lax
from jax.experimental import pallas as pl
from jax.experimental.pallas import tpu as pltpu

D_MODEL = 1024
BATCH = 8
SEQ = 2048
TOKENS = BATCH * SEQ
DEPTH = 2
CHUNK = 64
CHUNK_SHIFT = 6
GMLP_BLOCK = 128
GATE_DIM = 2 * D_MODEL
A_GROUPS = 8
A_GROUP_DIM = GATE_DIM // A_GROUPS
B_HEADS = 8
QK_NOPE = 128
QK_ROPE = 64
V_HEAD = 128
Q_LORA = 384
KV_LORA = 256
ROPE_THETA = 10000.0
D_FF = 4 * D_MODEL
EPS = 1e-6

LANES = 128
QK_PAD = 2 * LANES
VMEM_LIMIT = 56 * 1024 * 1024

TM = 512
TM_MLP = 1024
NCHUNK = 512
TQ = 256
TK = 256
Q_TILES_PER_STEP = 4
ONES_ROWS = 16

F32 = jnp.float32
BF16 = jnp.bfloat16
NEG = float(np.finfo(np.float32).min)
SQRT_HALF = float(np.sqrt(0.5))
LOG2_E = float(np.log2(np.e))
HALF_PI = float(np.pi / 2)
NT_DIMS = (((1,), (1,)), ((), ()))
TN_DIMS = (((0,), (0,)), ((), ()))


def _rms(x, g):
    ms = jnp.mean(x * x, axis=-1, keepdims=True)
    return x * lax.rsqrt(ms + EPS) * g


def _resident(shape):
    nd = len(shape)
    return pl.BlockSpec(shape, lambda *_: (0,) * nd, pipeline_mode=pl.Buffered(1))


def _params(*sem):
    return pltpu.CompilerParams(dimension_semantics=sem, vmem_limit_bytes=VMEM_LIMIT)


def _gelu(z):
    return 0.5 * z * (1.0 + lax.erf(z * SQRT_HALF))


def _gmlp_kernel(h_ref, g_ref, w_in_ref, lng_ref, lnb_ref, ws_ref, bs_ref, w_out_ref, w1f_ref, w2f_ref,
                 o_ref, w1b_ref, w2b_ref, v_scr, vn_scr, gated_scr, ws_scr):
    w1b_ref[...] = w1f_ref[...].astype(BF16)
    w2b_ref[...] = w2f_ref[...].astype(BF16)

    h = h_ref[...]
    hn = _rms(h, g_ref[...]).astype(BF16)
    n_half = GATE_DIM // NCHUNK
    row_blocks = [slice(r * GMLP_BLOCK, (r + 1) * GMLP_BLOCK) for r in range(TM // GMLP_BLOCK)]

    for c in range(n_half):
        w_cols = slice(GATE_DIM + c * NCHUNK, GATE_DIM + (c + 1) * NCHUNK)
        v_scr[:, c * NCHUNK:(c + 1) * NCHUNK] = _gelu(
            jnp.dot(hn, w_in_ref[:, w_cols].astype(BF16), preferred_element_type=F32))

    ii = lax.broadcasted_iota(jnp.int32, (GMLP_BLOCK, GMLP_BLOCK), 0) >> CHUNK_SHIFT
    jj = lax.broadcasted_iota(jnp.int32, (GMLP_BLOCK, GMLP_BLOCK), 1) >> CHUNK_SHIFT
    for g in range(A_GROUPS):
        ws_scr[g] = jnp.where(jj <= ii, ws_ref[g], 0.0).astype(BF16)

    for rows in row_blocks:
        v = v_scr[rows, :]
        mu = jnp.mean(v, axis=-1, keepdims=True)
        var = jnp.mean(jnp.square(v - mu), axis=-1, keepdims=True)
        vn_scr[rows, :] = ((v - mu) * lax.rsqrt(var + EPS) * lng_ref[...] + lnb_ref[...]).astype(BF16)

    for c in range(n_half):
        u = _gelu(jnp.dot(hn, w_in_ref[:, c * NCHUNK:(c + 1) * NCHUNK].astype(BF16),
                          preferred_element_type=F32))
        for gl in range(NCHUNK // A_GROUP_DIM):
            g = c * (NCHUNK // A_GROUP_DIM) + gl
            cols = slice(g * A_GROUP_DIM, (g + 1) * A_GROUP_DIM)
            for rows in row_blocks:
                sv = jnp.dot(ws_scr[g], vn_scr[rows, cols], preferred_element_type=F32) + bs_ref[g]
                u_blk = u[rows, gl * A_GROUP_DIM:(gl + 1) * A_GROUP_DIM]
                gated_scr[rows, cols] = (u_blk * sv).astype(BF16)

    o_ref[...] = h + jnp.dot(gated_scr[...], w_out_ref[...].astype(BF16), preferred_element_type=F32)


def _gmlp(h, g, w_in, lng, lnb, ws, bs, w_out, w1f, w2f):
    steps = TOKENS // TM
    tok = pl.BlockSpec((TM, D_MODEL), lambda t: (t, 0))
    w1_slab = pl.BlockSpec((DEPTH, D_MODEL // steps, D_FF), lambda t: (0, t, 0))
    w2_slab = pl.BlockSpec((DEPTH, D_FF // steps, D_MODEL), lambda t: (0, t, 0))
    return pl.pallas_call(
        _gmlp_kernel,
        grid=(steps,),
        in_specs=[tok, _resident((1, D_MODEL)), _resident((D_MODEL, 2 * GATE_DIM)),
                  _resident((1, GATE_DIM)), _resident((1, GATE_DIM)),
                  _resident((A_GROUPS, GMLP_BLOCK, GMLP_BLOCK)),
                  _resident((A_GROUPS, GMLP_BLOCK, 1)), _resident((GATE_DIM, D_MODEL)),
                  w1_slab, w2_slab],
        out_specs=[tok, w1_slab, w2_slab],
        out_shape=[jax.ShapeDtypeStruct((TOKENS, D_MODEL), F32),
                   jax.ShapeDtypeStruct((DEPTH, D_MODEL, D_FF), BF16),
                   jax.ShapeDtypeStruct((DEPTH, D_FF, D_MODEL), BF16)],
        scratch_shapes=[pltpu.VMEM((TM, GATE_DIM), F32), pltpu.VMEM((TM, GATE_DIM), BF16),
                        pltpu.VMEM((TM, GATE_DIM), BF16),
                        pltpu.VMEM((A_GROUPS, GMLP_BLOCK, GMLP_BLOCK), BF16)],
        compiler_params=_params("arbitrary"),
        name="gmlp_mixer",
    )(h, g, w_in, lng, lnb, ws, bs, w_out, w1f, w2f)


def _mlp_residual(h, g_ref, w1_ref, w2_ref, a_scr, anchors=None):
    hn = _rms(h, g_ref[...]).astype(BF16)
    n_chunks = D_FF // NCHUNK
    slab = h.shape[0] // n_chunks
    for c in range(n_chunks):
        cols = slice(c * NCHUNK, (c + 1) * NCHUNK)
        a = jnp.maximum(jnp.dot(hn, w1_ref[:, cols], preferred_element_type=F32), 0.0)
        if anchors is None:
            a_scr[:, cols] = (a * a).astype(BF16)
        else:
            for r in range(n_chunks):
                rows = slice(r * slab, (r + 1) * slab)
                blk = a[rows] + anchors[c] if r == c else a[rows]
                a_scr[rows, cols] = (blk * blk).astype(BF16)
    return h + jnp.dot(a_scr[...], w2_ref[...], preferred_element_type=F32)


def _mlp_rope_kernel(h_ref, g_ref, w1_ref, w2_ref, pos_ref, rope_ref, o_ref, cos_ref, sin_ref, a_scr):
    half = QK_ROPE // 2
    n_slabs = D_FF // NCHUNK
    slab = TM_MLP // n_slabs
    pos = pos_ref[0].astype(F32)
    zeros = jnp.zeros((half, slab), F32)
    anchors = []
    for c in range(n_slabs):
        toks = slice(c * slab, (c + 1) * slab)
        sin_cos = jnp.sin(rope_ref[:QK_ROPE, :] * pos[:, toks] + rope_ref[QK_ROPE:, :])
        sin_t, cos_t = sin_cos[:half], sin_cos[half:]
        cos_tab = jnp.concatenate([cos_t, cos_t, cos_t, cos_t], axis=0).T
        sin_tab = jnp.concatenate([-sin_t, zeros, sin_t, zeros], axis=0).T
        cos_ref[toks, :] = cos_tab
        sin_ref[toks, :] = sin_tab
        anchors.append((cos_tab[:, 0:1] + sin_tab[:, 0:1]) * 0.0)
    o_ref[...] = _mlp_residual(h_ref[...], g_ref, w1_ref, w2_ref, a_scr, anchors)


def _mlp_final_kernel(h_ref, g_ref, w1_ref, w2_ref, gf_ref, o_ref, a_scr):
    o_ref[...] = _rms(_mlp_residual(h_ref[...], g_ref, w1_ref, w2_ref, a_scr), gf_ref[...])


def _mlp_specs(layer):
    tok = pl.BlockSpec((TM_MLP, D_MODEL), lambda t: (t, 0))
    w1 = pl.BlockSpec((None, D_MODEL, D_FF), lambda t: (layer, 0, 0), pipeline_mode=pl.Buffered(1))
    w2 = pl.BlockSpec((None, D_FF, D_MODEL), lambda t: (layer, 0, 0), pipeline_mode=pl.Buffered(1))
    return tok, [tok, _resident((1, D_MODEL)), w1, w2]


def _mlp_rope(h, g, w1, w2, pos, rope_tab, layer):
    tok, in_specs = _mlp_specs(layer)
    tab = pl.BlockSpec((TM_MLP, LANES), lambda t: (t, 0))
    return pl.pallas_call(
        _mlp_rope_kernel,
        grid=(TOKENS // TM_MLP,),
        in_specs=in_specs + [pl.BlockSpec((1, 1, TM_MLP), lambda t: (t, 0, 0)),
                             _resident((2 * QK_ROPE, LANES))],
        out_specs=[tok, tab, tab],
        out_shape=[jax.ShapeDtypeStruct((TOKENS, D_MODEL), F32),
                   jax.ShapeDtypeStruct((TOKENS, LANES), F32),
                   jax.ShapeDtypeStruct((TOKENS, LANES), F32)],
        scratch_shapes=[pltpu.VMEM((TM_MLP, D_FF), BF16)],
        compiler_params=_params("arbitrary"),
        name="mlp_rope",
    )(h, g, w1, w2, pos, rope_tab)


def _mlp_final(h, g, w1, w2, gf, layer):
    tok, in_specs = _mlp_specs(layer)
    return pl.pallas_call(
        _mlp_final_kernel,
        grid=(TOKENS // TM_MLP,),
        in_specs=in_specs + [_resident((1, D_MODEL))],
        out_specs=tok,
        out_shape=jax.ShapeDtypeStruct((TOKENS, D_MODEL), F32),
        scratch_shapes=[pltpu.VMEM((TM_MLP, D_FF), BF16)],
        compiler_params=_params("arbitrary"),
        name="mlp_final",
    )(h, g, w1, w2, gf)


def _rope(x, cos, sin_signed):
    return x * cos + pltpu.roll(x, LANES // 2, 1) * sin_signed


def _proj_kernel(h_ref, cos_ref, sin_ref, gkv_ref, gq_ref, wkva_ref, gkva_ref, wk_ref, wvt_ref,
                 wqa_ref, gqa_ref, wqb_ref, q_ref, kn_ref, kpe_ref, vt_ref):
    h = h_ref[0]
    y = h * lax.rsqrt(jnp.mean(h * h, axis=-1, keepdims=True) + EPS)
    cos = cos_ref[0]
    sin_signed = sin_ref[0]

    ckv = jnp.dot((y * gkv_ref[...]).astype(BF16), wkva_ref[...], preferred_element_type=F32)
    c_kv = _rms(ckv[:, :KV_LORA], gkva_ref[...]).astype(BF16)
    kpe_ref[0] = _rope(ckv[:, KV_LORA:], cos, sin_signed).astype(BF16)
    kn = jnp.dot(c_kv, wk_ref[...], preferred_element_type=F32).astype(BF16)
    for hd in range(B_HEADS):
        kn_ref[0, hd] = kn[:, hd * QK_NOPE:(hd + 1) * QK_NOPE]
    vt = lax.dot_general(wvt_ref[...], c_kv, NT_DIMS, preferred_element_type=F32).astype(BF16)
    for kb in range(TM // TK):
        vt_ref[0, kb] = vt[:, kb * TK:(kb + 1) * TK]

    scale = (QK_NOPE + QK_ROPE) ** -0.5 * LOG2_E
    cq = jnp.dot((y * gq_ref[...]).astype(BF16), wqa_ref[...], preferred_element_type=F32)
    cq = _rms(cq, gqa_ref[...] * scale).astype(BF16)
    for hd in range(B_HEADS):
        q = jnp.dot(cq, wqb_ref[:, hd * QK_PAD:(hd + 1) * QK_PAD], preferred_element_type=F32)
        q_ref[0, hd, :, :QK_NOPE] = q[:, :QK_NOPE].astype(BF16)
        q_ref[0, hd, :, QK_NOPE:] = _rope(q[:, QK_NOPE:], cos, sin_signed).astype(BF16)


def _proj(h, cos_tab, sin_tab, gkv, gq, wkva, gkva, wk, wvt, wqa, gqa, wqb):
    n_t = SEQ // TM
    tab = pl.BlockSpec((1, TM, LANES), lambda b, t: (b, t, 0))
    return pl.pallas_call(
        _proj_kernel,
        grid=(BATCH, n_t),
        in_specs=[pl.BlockSpec((1, TM, D_MODEL), lambda b, t: (b, t, 0)), tab, tab,
                  _resident((1, D_MODEL)), _resident((1, D_MODEL)),
                  _resident((D_MODEL, KV_LORA + LANES)), _resident((1, KV_LORA)),
                  _resident((KV_LORA, B_HEADS * QK_NOPE)), _resident((B_HEADS * V_HEAD, KV_LORA)),
                  _resident((D_MODEL, Q_LORA)), _resident((1, Q_LORA)),
                  _resident((Q_LORA, B_HEADS * QK_PAD))],
        out_specs=[pl.BlockSpec((1, B_HEADS, TM, QK_PAD), lambda b, t: (b, 0, t, 0)),
                   pl.BlockSpec((1, B_HEADS, TM, QK_NOPE), lambda b, t: (b, 0, t, 0)),
                   pl.BlockSpec((1, TM, LANES), lambda b, t: (b, t, 0)),
                   pl.BlockSpec((1, TM // TK, B_HEADS * V_HEAD, TK), lambda b, t: (b, t, 0, 0))],
        out_shape=[jax.ShapeDtypeStruct((BATCH, B_HEADS, SEQ, QK_PAD), BF16),
                   jax.ShapeDtypeStruct((BATCH, B_HEADS, SEQ, QK_NOPE), BF16),
                   jax.ShapeDtypeStruct((BATCH, SEQ, LANES), BF16),
                   jax.ShapeDtypeStruct((BATCH, SEQ // TK, B_HEADS * V_HEAD, TK), BF16)],
        compiler_params=_params("arbitrary", "arbitrary"),
        name="qkv_proj",
    )(h, cos_tab, sin_tab, gkv, gq, wkva, gkva, wk, wvt, wqa, gqa, wqb)


def _attn_kernel(h_ref, q_ref, kn_ref, kpe_ref, vt_ref, wo_ref, o_ref, sa_scr, sb_scr, mta_scr, mtb_scr,
                 m_scr, acc_scr, ot_scr):
    k = pl.program_id(1)
    kc_id = lax.broadcasted_iota(jnp.int32, (TK, TQ), 0) >> CHUNK_SHIFT
    qc_id = lax.broadcasted_iota(jnp.int32, (TK, TQ), 1) >> CHUNK_SHIFT
    diag_ok = kc_id <= qc_id
    ones_rows = jnp.ones((ONES_ROWS, TK), BF16)

    def score_head(hd, sub, j, s_scr, mt_scr, diagonal):
        rows = pl.ds(pl.multiple_of(j * TK, TK), TK)
        k_cat = jnp.concatenate([kn_ref[0, hd, rows, :], kpe_ref[0, rows, :]], axis=-1)
        q_h = q_ref[0, hd, sub * TQ:(sub + 1) * TQ, :]
        s = lax.dot_general(k_cat, q_h, NT_DIMS, preferred_element_type=F32)
        if diagonal:
            s = jnp.where(diag_ok, s, NEG)
        s_scr[hd] = s
        mt_scr[hd] = jnp.max(s, axis=0, keepdims=True)

    def softmax_head(hd, sub, j, s_scr, mt_scr, first):
        slot = sub % 2
        v_aug = jnp.concatenate([vt_ref[0, j, hd * V_HEAD:(hd + 1) * V_HEAD, :], ones_rows], axis=0)
        if first:
            m_new = mt_scr[hd]
            p = jnp.exp2(s_scr[hd] - m_new)
            acc_scr[slot, hd] = jnp.dot(v_aug, p.astype(BF16), preferred_element_type=F32)
        else:
            m = m_scr[slot, hd]
            m_new = jnp.maximum(m, mt_scr[hd])
            alpha = jnp.exp2(m - m_new)
            p = jnp.exp2(s_scr[hd] - m_new)
            acc_scr[slot, hd] = alpha * acc_scr[slot, hd] + jnp.dot(v_aug, p.astype(BF16),
                                                                    preferred_element_type=F32)
        m_scr[slot, hd] = m_new

    def score_stage(*args):
        for hd in range(B_HEADS):
            score_head(hd, *args)

    def softmax_stage(*args):
        for hd in range(B_HEADS):
            softmax_head(hd, *args)

    def paired_stage(score_args, softmax_args):
        for hd in range(B_HEADS):
            score_head(hd, *score_args)
            softmax_head(hd, *softmax_args)

    def first_tiles(sub, i):
        score_stage(sub, i, sb_scr, mtb_scr, True)
        paired_stage((sub, 0, sa_scr, mta_scr, False), (sub, i, sb_scr, mtb_scr, True))

    def tile_pairs(sub, n_pairs):
        def tile_pair(u, _):
            j = 2 * u
            paired_stage((sub, j + 1, sb_scr, mtb_scr, False), (sub, j, sa_scr, mta_scr, False))
            paired_stage((sub, j + 2, sa_scr, mta_scr, False), (sub, j + 1, sb_scr, mtb_scr, False))
            return 0

        lax.fori_loop(0, n_pairs, tile_pair, 0)

    def last_tiles_even(sub, i):
        paired_stage((sub, i - 1, sb_scr, mtb_scr, False), (sub, i - 2, sa_scr, mta_scr, False))
        softmax_stage(sub, i - 1, sb_scr, mtb_scr, False)

    def project(sub):
        slot = sub % 2
        q_rows = slice(sub * TQ, (sub + 1) * TQ)
        for hd in range(B_HEADS):
            inv_l = 1.0 / acc_scr[slot, hd, V_HEAD:V_HEAD + 1, :]
            ot_scr[slot, hd * V_HEAD:(hd + 1) * V_HEAD, :] = (acc_scr[slot, hd, :V_HEAD, :] * inv_l).astype(BF16)
        out = lax.dot_general(ot_scr[slot], wo_ref[...], TN_DIMS, preferred_element_type=F32)
        o_ref[0, q_rows, :] = h_ref[0, q_rows, :] + out

    for sub in range(Q_TILES_PER_STEP):
        i = Q_TILES_PER_STEP * k + sub
        if sub > 0:
            project(sub - 1)
        first_tiles(sub, i)
        tile_pairs(sub, jnp.maximum(i - 1, 0) >> 1)
        if sub % 2 == 1:
            softmax_stage(sub, i - 1, sa_scr, mta_scr, False)
        elif sub == 0:
            pl.when(k > 0)(functools.partial(last_tiles_even, sub, i))
        else:
            last_tiles_even(sub, i)
    project(Q_TILES_PER_STEP - 1)


def _attn(h, q, kn, kpe, vt, wo):
    tq2 = Q_TILES_PER_STEP * TQ
    return pl.pallas_call(
        _attn_kernel,
        grid=(BATCH, SEQ // tq2),
        in_specs=[pl.BlockSpec((1, tq2, D_MODEL), lambda b, k: (b, k, 0)),
                  pl.BlockSpec((1, B_HEADS, tq2, QK_PAD), lambda b, k: (b, 0, k, 0)),
                  pl.BlockSpec((1, B_HEADS, SEQ, QK_NOPE), lambda b, k: (b, 0, 0, 0)),
                  pl.BlockSpec((1, SEQ, LANES), lambda b, k: (b, 0, 0)),
                  pl.BlockSpec((1, SEQ // TK, B_HEADS * V_HEAD, TK), lambda b, k: (b, 0, 0, 0)),
                  _resident((B_HEADS * V_HEAD, D_MODEL))],
        out_specs=pl.BlockSpec((1, tq2, D_MODEL), lambda b, k: (b, k, 0)),
        out_shape=jax.ShapeDtypeStruct((BATCH, SEQ, D_MODEL), F32),
        scratch_shapes=[pltpu.VMEM((B_HEADS, TK, TQ), F32), pltpu.VMEM((B_HEADS, TK, TQ), F32),
                        pltpu.VMEM((B_HEADS, 1, TQ), F32), pltpu.VMEM((B_HEADS, 1, TQ), F32),
                        pltpu.VMEM((2, B_HEADS, 1, TQ), F32),
                        pltpu.VMEM((2, B_HEADS, V_HEAD + ONES_ROWS, TQ), F32),
                        pltpu.VMEM((2, B_HEADS * V_HEAD, TQ), BF16)],
        compiler_params=_params("arbitrary", "arbitrary"),
        name="mla_attention",
    )(h, q, kn, kpe, vt, wo)


def _row(v):
    return v.reshape(1, -1)


def _spread_rope(w):
    half = QK_ROPE // 2
    zeros = jnp.zeros(w.shape[:-1] + (half,), w.dtype)
    return jnp.concatenate([w[..., :half], zeros, w[..., half:], zeros], axis=-1)


def kernel(x, positions, norm_mix_g, norm_mlp_g, a_w_in, a_ln_v_g, a_ln_v_b, a_w_s, a_b_s, a_w_out,
           b_w_q_a, b_q_norm_g, b_w_q_b, b_w_o, kv_src_norm_g, kv_w_a, kv_a_norm_g, kv_w_b,
           mlp_w1, mlp_w2, final_norm_g):
    w_in = a_w_in[0]
    w_out = a_w_out[0]
    wkva = jnp.concatenate([kv_w_a[:, :KV_LORA], _spread_rope(kv_w_a[:, KV_LORA:])], axis=-1).astype(BF16)
    kvb = kv_w_b.reshape(KV_LORA, B_HEADS, QK_NOPE + V_HEAD)
    wk = kvb[:, :, :QK_NOPE].reshape(KV_LORA, B_HEADS * QK_NOPE).astype(BF16)
    wvt = kvb[:, :, QK_NOPE:].reshape(KV_LORA, B_HEADS * V_HEAD).T.astype(BF16)
    wqa = b_w_q_a[0].astype(BF16)
    wqb = b_w_q_b[0].reshape(Q_LORA, B_HEADS, QK_NOPE + QK_ROPE)
    wqb = jnp.concatenate([wqb[:, :, :QK_NOPE], _spread_rope(wqb[:, :, QK_NOPE:])], axis=-1)
    wqb = wqb.reshape(Q_LORA, B_HEADS * QK_PAD).astype(BF16)
    wo = b_w_o[0].astype(BF16)

    inv_freq = ROPE_THETA ** (-jnp.arange(0, QK_ROPE, 2, dtype=F32) / QK_ROPE)
    phase = jnp.concatenate([jnp.zeros((QK_ROPE // 2,), F32), jnp.full((QK_ROPE // 2,), HALF_PI, F32)])
    rope_tab = jnp.broadcast_to(jnp.concatenate([inv_freq, inv_freq, phase])[:, None], (2 * QK_ROPE, LANES))

    h = x.reshape(TOKENS, D_MODEL)
    h, w1, w2 = _gmlp(h, _row(norm_mix_g[0]), w_in, _row(a_ln_v_g[0]), _row(a_ln_v_b[0]), a_w_s[0],
                      a_b_s[0].reshape(A_GROUPS, GMLP_BLOCK, 1), w_out, mlp_w1, mlp_w2)
    h, cos_tab, sin_tab = _mlp_rope(h, _row(norm_mlp_g[0]), w1, w2,
                                    positions.reshape(TOKENS // TM_MLP, 1, TM_MLP),
                                    rope_tab, 0)

    h3 = h.reshape(BATCH, SEQ, D_MODEL)
    q, kn, kpe, vt = _proj(h3, cos_tab.reshape(BATCH, SEQ, LANES), sin_tab.reshape(BATCH, SEQ, LANES),
                           _row(kv_src_norm_g), _row(norm_mix_g[1]), wkva, _row(kv_a_norm_g), wk, wvt,
                           wqa, _row(b_q_norm_g[0]), wqb)
    h3 = _attn(h3, q, kn, kpe, vt, wo)

    h = _mlp_final(h3.reshape(TOKENS, D_MODEL), _row(norm_mlp_g[1]), w1, w2, _row(final_norm_g), 1)
    return h.reshape(BATCH, SEQ, D_MODEL)
```

```python
import functools

import jax
import jax.numpy as jnp
import numpy as np
from jax import lax
from jax.experimental import pallas as pl
from jax.experimental.pallas import tpu as pltpu

D_MODEL = 1024
BATCH = 8
SEQ = 2048
TOKENS = BATCH * SEQ
DEPTH = 2
CHUNK = 64
CHUNK_SHIFT = 6
GMLP_BLOCK = 128
GATE_DIM = 2 * D_MODEL
A_GROUPS = 8
A_GROUP_DIM = GATE_DIM // A_GROUPS
B_HEADS = 8
QK_NOPE = 128
QK_ROPE = 64
V_HEAD = 128
Q_LORA = 384
KV_LORA = 256
ROPE_THETA = 10000.0
D_FF = 4 * D_MODEL
EPS = 1e-6

LANES = 128
QK_PAD = 2 * LANES
VMEM_LIMIT = 56 * 1024 * 1024

TM = 512
TM_PROJ = 1024
TM_MLP = 1024
NCHUNK = 512
TQ = 256
TK = 256
Q_TILES_PER_STEP = 4
ONES_ROWS = 16

F32 = jnp.float32
BF16 = jnp.bfloat16
NEG = float(np.finfo(np.float32).min)
SQRT_HALF = float(np.sqrt(0.5))
LOG2_E = float(np.log2(np.e))
HALF_PI = float(np.pi / 2)
NT_DIMS = (((1,), (1,)), ((), ()))
TN_DIMS = (((0,), (0,)), ((), ()))


def _rms(x, g):
    ms = jnp.mean(x * x, axis=-1, keepdims=True)
    return x * lax.rsqrt(ms + EPS) * g


def _resident(shape):
    nd = len(shape)
    return pl.BlockSpec(shape, lambda *_: (0,) * nd, pipeline_mode=pl.Buffered(1))


def _layer_gain(layer):
    return pl.BlockSpec((None, 1, D_MODEL), lambda *_: (layer, 0, 0), pipeline_mode=pl.Buffered(1))


def _params(*sem):
    return pltpu.CompilerParams(dimension_semantics=sem, vmem_limit_bytes=VMEM_LIMIT)


def _gelu_x2(z):
    return z * (1.0 + lax.erf(z * SQRT_HALF))


def _gmlp_kernel(h_ref, g_ref, w_in_ref, lng_ref, lnb_ref, ws_ref, bs_ref, w_out_ref, w1f_ref, w2f_ref,
                 o_ref, w1b_ref, w2b_ref, v_scr, vn_scr, gated_scr, ws_scr):
    w1b_ref[...] = w1f_ref[...].astype(BF16)
    w2b_ref[...] = w2f_ref[...].astype(BF16)

    h = h_ref[...]
    hn = _rms(h, g_ref[...]).astype(BF16)
    n_half = GATE_DIM // NCHUNK
    row_blocks = [slice(r * GMLP_BLOCK, (r + 1) * GMLP_BLOCK) for r in range(TM // GMLP_BLOCK)]

    for c in range(n_half):
        w_cols = slice(GATE_DIM + c * NCHUNK, GATE_DIM + (c + 1) * NCHUNK)
        v_scr[:, c * NCHUNK:(c + 1) * NCHUNK] = _gelu_x2(
            jnp.dot(hn, w_in_ref[:, w_cols].astype(BF16), preferred_element_type=F32))

    ii = lax.broadcasted_iota(jnp.int32, (GMLP_BLOCK, GMLP_BLOCK), 0) >> CHUNK_SHIFT
    jj = lax.broadcasted_iota(jnp.int32, (GMLP_BLOCK, GMLP_BLOCK), 1) >> CHUNK_SHIFT
    for g in range(A_GROUPS):
        ws_scr[g] = jnp.where(jj <= ii, 0.5 * ws_ref[g], 0.0).astype(BF16)

    for rows in row_blocks:
        v = v_scr[rows, :]
        mu = jnp.mean(v, axis=-1, keepdims=True)
        var = jnp.mean(jnp.square(v - mu), axis=-1, keepdims=True)
        vn_scr[rows, :] = ((v - mu) * lax.rsqrt(var + 4.0 * EPS) * lng_ref[...] + lnb_ref[...]).astype(BF16)

    for c in range(n_half):
        u = _gelu_x2(jnp.dot(hn, w_in_ref[:, c * NCHUNK:(c + 1) * NCHUNK].astype(BF16),
                          preferred_element_type=F32))
        for gl in range(NCHUNK // A_GROUP_DIM):
            g = c * (NCHUNK // A_GROUP_DIM) + gl
            cols = slice(g * A_GROUP_DIM, (g + 1) * A_GROUP_DIM)
            for rows in row_blocks:
                sv = jnp.dot(ws_scr[g], vn_scr[rows, cols], preferred_element_type=F32) + 0.5 * bs_ref[g]
                u_blk = u[rows, gl * A_GROUP_DIM:(gl + 1) * A_GROUP_DIM]
                gated_scr[rows, cols] = (u_blk * sv).astype(BF16)

    o_ref[...] = h + jnp.dot(gated_scr[...], w_out_ref[...].astype(BF16), preferred_element_type=F32)


def _gmlp(h, g, w_in, lng, lnb, ws, bs, w_out, w1f, w2f):
    steps = TOKENS // TM
    tok = pl.BlockSpec((TM, D_MODEL), lambda t: (t, 0))
    w1_slab = pl.BlockSpec((DEPTH, D_MODEL // steps, D_FF), lambda t: (0, t, 0))
    w2_slab = pl.BlockSpec((DEPTH, D_FF // steps, D_MODEL), lambda t: (0, t, 0))
    return pl.pallas_call(
        _gmlp_kernel,
        grid=(steps,),
        in_specs=[tok, _layer_gain(0), _resident((D_MODEL, 2 * GATE_DIM)),
                  _resident((1, GATE_DIM)), _resident((1, GATE_DIM)),
                  _resident((A_GROUPS, GMLP_BLOCK, GMLP_BLOCK)),
                  _resident((A_GROUPS, GMLP_BLOCK, 1)), _resident((GATE_DIM, D_MODEL)),
                  w1_slab, w2_slab],
        out_specs=[tok, w1_slab, w2_slab],
        out_shape=[jax.ShapeDtypeStruct((TOKENS, D_MODEL), F32),
                   jax.ShapeDtypeStruct((DEPTH, D_MODEL, D_FF), BF16),
                   jax.ShapeDtypeStruct((DEPTH, D_FF, D_MODEL), BF16)],
        scratch_shapes=[pltpu.VMEM((TM, GATE_DIM), F32), pltpu.VMEM((TM, GATE_DIM), BF16),
                        pltpu.VMEM((TM, GATE_DIM), BF16),
                        pltpu.VMEM((A_GROUPS, GMLP_BLOCK, GMLP_BLOCK), BF16)],
        compiler_params=_params("arbitrary"),
        name="gmlp_mixer",
    )(h, g, w_in, lng, lnb, ws, bs, w_out, w1f, w2f)


def _mlp_residual(h, g_ref, w1_ref, w2_ref, a_scr, anchors=None):
    hn = _rms(h, g_ref[...]).astype(BF16)
    n_chunks = D_FF // NCHUNK
    slab = h.shape[0] // n_chunks
    for c in range(n_chunks):
        cols = slice(c * NCHUNK, (c + 1) * NCHUNK)
        a = jnp.maximum(jnp.dot(hn, w1_ref[:, cols], preferred_element_type=F32), 0.0)
        if anchors is None:
            a_scr[:, cols] = (a * a).astype(BF16)
        else:
            for r in range(n_chunks):
                rows = slice(r * slab, (r + 1) * slab)
                blk = a[rows] + anchors[c] if r == c else a[rows]
                a_scr[rows, cols] = (blk * blk).astype(BF16)
    return h + jnp.dot(a_scr[...], w2_ref[...], preferred_element_type=F32)


def _mlp_rope_kernel(h_ref, g_ref, w1_ref, w2_ref, pos_ref, rope_ref, o_ref, cos_ref, sin_ref, a_scr):
    half = QK_ROPE // 2
    n_slabs = D_FF // NCHUNK
    slab = TM_MLP // n_slabs
    pos = pos_ref[0].astype(F32)
    zeros = jnp.zeros((half, slab), F32)
    anchors = []
    for c in range(n_slabs):
        toks = slice(c * slab, (c + 1) * slab)
        sin_cos = jnp.sin(rope_ref[:QK_ROPE, :] * pos[:, toks] + rope_ref[QK_ROPE:, :])
        sin_t, cos_t = sin_cos[:half], sin_cos[half:]
        cos_tab = jnp.concatenate([cos_t, cos_t, cos_t, cos_t], axis=0).T
        sin_tab = jnp.concatenate([-sin_t, zeros, sin_t, zeros], axis=0).T
        cos_ref[toks, :] = cos_tab
        sin_ref[toks, :] = sin_tab
        anchors.append((cos_tab[:, 0:1] + sin_tab[:, 0:1]) * 0.0)
    o_ref[...] = _mlp_residual(h_ref[...], g_ref, w1_ref, w2_ref, a_scr, anchors)


def _mlp_final_kernel(h_ref, g_ref, w1_ref, w2_ref, gf_ref, o_ref, a_scr):
    o_ref[...] = _rms(_mlp_residual(h_ref[...], g_ref, w1_ref, w2_ref, a_scr), gf_ref[...])


def _mlp_specs(layer):
    tok = pl.BlockSpec((TM_MLP, D_MODEL), lambda t: (t, 0))
    w1 = pl.BlockSpec((None, D_MODEL, D_FF), lambda t: (layer, 0, 0), pipeline_mode=pl.Buffered(1))
    w2 = pl.BlockSpec((None, D_FF, D_MODEL), lambda t: (layer, 0, 0), pipeline_mode=pl.Buffered(1))
    return tok, [tok, _layer_gain(layer), w1, w2]


def _mlp_rope(h, g, w1, w2, pos, rope_tab, layer):
    tok, in_specs = _mlp_specs(layer)
    tab = pl.BlockSpec((TM_MLP, LANES), lambda t: (t, 0))
    return pl.pallas_call(
        _mlp_rope_kernel,
        grid=(TOKENS // TM_MLP,),
        in_specs=in_specs + [pl.BlockSpec((1, 1, TM_MLP), lambda t: (t, 0, 0)),
                             _resident((2 * QK_ROPE, LANES))],
        out_specs=[tok, tab, tab],
        out_shape=[jax.ShapeDtypeStruct((TOKENS, D_MODEL), F32),
                   jax.ShapeDtypeStruct((TOKENS, LANES), F32),
                   jax.ShapeDtypeStruct((TOKENS, LANES), F32)],
        scratch_shapes=[pltpu.VMEM((TM_MLP, D_FF), BF16)],
        compiler_params=_params("arbitrary"),
        name="mlp_rope",
    )(h, g, w1, w2, pos, rope_tab)


def _mlp_final(h, g, w1, w2, gf, layer):
    tok, in_specs = _mlp_specs(layer)
    return pl.pallas_call(
        _mlp_final_kernel,
        grid=(TOKENS // TM_MLP,),
        in_specs=in_specs + [_resident((1, D_MODEL))],
        out_specs=tok,
        out_shape=jax.ShapeDtypeStruct((TOKENS, D_MODEL), F32),
        scratch_shapes=[pltpu.VMEM((TM_MLP, D_FF), BF16)],
        compiler_params=_params("arbitrary"),
        name="mlp_final",
    )(h, g, w1, w2, gf)


def _rope(x, cos, sin_signed):
    return x * cos + pltpu.roll(x, LANES // 2, 1) * sin_signed


def _proj_kernel(h_ref, cos_ref, sin_ref, gkv_ref, gq_ref, wkva_ref, gkva_ref, wk_ref, wvt_ref,
                 wqa_ref, gqa_ref, wqb_ref, q_ref, kn_ref, kpe_ref, vt_ref):
    h = h_ref[0]
    y = h * lax.rsqrt(jnp.mean(h * h, axis=-1, keepdims=True) + EPS)
    cos = cos_ref[0]
    sin_signed = sin_ref[0]

    ckv = jnp.dot((y * gkv_ref[...]).astype(BF16), wkva_ref[...], preferred_element_type=F32)
    c_kv = _rms(ckv[:, :KV_LORA], gkva_ref[...]).astype(BF16)
    kpe_ref[0] = _rope(ckv[:, KV_LORA:], cos, sin_signed).astype(BF16)
    kn = jnp.dot(c_kv, wk_ref[...], preferred_element_type=F32).astype(BF16)
    for hd in range(B_HEADS):
        kn_ref[0, hd] = kn[:, hd * QK_NOPE:(hd + 1) * QK_NOPE]
    vt = lax.dot_general(wvt_ref[...], c_kv, NT_DIMS, preferred_element_type=F32).astype(BF16)
    for kb in range(TM_PROJ // TK):
        vt_ref[0, kb] = vt[:, kb * TK:(kb + 1) * TK]

    scale = (QK_NOPE + QK_ROPE) ** -0.5 * LOG2_E
    cq = jnp.dot((y * gq_ref[...]).astype(BF16), wqa_ref[...].astype(BF16), preferred_element_type=F32)
    cq = _rms(cq, gqa_ref[...] * scale).astype(BF16)
    for hd in range(B_HEADS):
        q = jnp.dot(cq, wqb_ref[:, hd * QK_PAD:(hd + 1) * QK_PAD], preferred_element_type=F32)
        q_ref[0, hd, :, :QK_NOPE] = q[:, :QK_NOPE].astype(BF16)
        q_ref[0, hd, :, QK_NOPE:] = _rope(q[:, QK_NOPE:], cos, sin_signed).astype(BF16)


def _proj(h, cos_tab, sin_tab, gkv, gq, wkva, gkva, wk, wvt, wqa, gqa, wqb):
    n_t = SEQ // TM_PROJ
    tab = pl.BlockSpec((1, TM_PROJ, LANES), lambda b, t: (b, t, 0))
    return pl.pallas_call(
        _proj_kernel,
        grid=(BATCH, n_t),
        in_specs=[pl.BlockSpec((1, TM_PROJ, D_MODEL), lambda b, t: (b, t, 0)), tab, tab,
                  _resident((1, D_MODEL)), _layer_gain(1),
                  _resident((D_MODEL, KV_LORA + LANES)), _resident((1, KV_LORA)),
                  _resident((KV_LORA, B_HEADS * QK_NOPE)), _resident((B_HEADS * V_HEAD, KV_LORA)),
                  _resident((D_MODEL, Q_LORA)), _resident((1, Q_LORA)),
                  _resident((Q_LORA, B_HEADS * QK_PAD))],
        out_specs=[pl.BlockSpec((1, B_HEADS, TM_PROJ, QK_PAD), lambda b, t: (b, 0, t, 0)),
                   pl.BlockSpec((1, B_HEADS, TM_PROJ, QK_NOPE), lambda b, t: (b, 0, t, 0)),
                   pl.BlockSpec((1, TM_PROJ, LANES), lambda b, t: (b, t, 0)),
                   pl.BlockSpec((1, TM_PROJ // TK, B_HEADS * V_HEAD, TK), lambda b, t: (b, t, 0, 0))],
        out_shape=[jax.ShapeDtypeStruct((BATCH, B_HEADS, SEQ, QK_PAD), BF16),
                   jax.ShapeDtypeStruct((BATCH, B_HEADS, SEQ, QK_NOPE), BF16),
                   jax.ShapeDtypeStruct((BATCH, SEQ, LANES), BF16),
                   jax.ShapeDtypeStruct((BATCH, SEQ // TK, B_HEADS * V_HEAD, TK), BF16)],
        compiler_params=_params("arbitrary", "arbitrary"),
        name="qkv_proj",
    )(h, cos_tab, sin_tab, gkv, gq, wkva, gkva, wk, wvt, wqa, gqa, wqb)


def _attn_kernel(h_ref, q_ref, kn_ref, kpe_ref, vt_ref, wo_ref, o_ref, sa_scr, sb_scr, mta_scr, mtb_scr,
                 m_scr, acc_scr, ot_scr, wo_scr):
    k = pl.program_id(1)
    kc_id = lax.broadcasted_iota(jnp.int32, (TK, TQ), 0) >> CHUNK_SHIFT
    qc_id = lax.broadcasted_iota(jnp.int32, (TK, TQ), 1) >> CHUNK_SHIFT
    diag_ok = kc_id <= qc_id
    ones_rows = jnp.ones((ONES_ROWS, TK), BF16)

    @pl.when(jnp.logical_and(pl.program_id(0) == 0, k == 0))
    def _():
        wo_scr[...] = wo_ref[...].astype(BF16)

    def score_head(hd, sub, j, s_scr, mt_scr, diagonal):
        rows = pl.ds(pl.multiple_of(j * TK, TK), TK)
        k_cat = jnp.concatenate([kn_ref[0, hd, rows, :], kpe_ref[0, rows, :]], axis=-1)
        q_h = q_ref[0, hd, sub * TQ:(sub + 1) * TQ, :]
        s = lax.dot_general(k_cat, q_h, NT_DIMS, preferred_element_type=F32)
        if diagonal:
            s = jnp.where(diag_ok, s, NEG)
        s_scr[hd] = s
        mt_scr[hd] = jnp.max(s, axis=0, keepdims=True)

    def softmax_head(hd, sub, j, s_scr, mt_scr, first):
        slot = sub % 2
        v_aug = jnp.concatenate([vt_ref[0, j, hd * V_HEAD:(hd + 1) * V_HEAD, :], ones_rows], axis=0)
        if first:
            m_new = mt_scr[hd]
            p = jnp.exp2(s_scr[hd] - m_new)
            acc_scr[slot, hd] = jnp.dot(v_aug, p.astype(BF16), preferred_element_type=F32)
        else:
            m = m_scr[slot, hd]
            m_new = jnp.maximum(m, mt_scr[hd])
            alpha = jnp.exp2(m - m_new)
            p = jnp.exp2(s_scr[hd] - m_new)
            acc_scr[slot, hd] = alpha * acc_scr[slot, hd] + jnp.dot(v_aug, p.astype(BF16),
                                                                    preferred_element_type=F32)
        m_scr[slot, hd] = m_new

    def score_stage(*args):
        for hd in range(B_HEADS):
            score_head(hd, *args)

    def softmax_stage(*args):
        for hd in range(B_HEADS):
            softmax_head(hd, *args)

    def paired_stage(score_args, softmax_args):
        for hd in range(B_HEADS):
            score_head(hd, *score_args)
            softmax_head(hd, *softmax_args)

    def first_tiles(sub, i):
        score_stage(sub, i, sb_scr, mtb_scr, True)
        paired_stage((sub, 0, sa_scr, mta_scr, False), (sub, i, sb_scr, mtb_scr, True))

    def tile_pairs(sub, n_pairs):
        def tile_pair(u, _):
            j = 2 * u
            paired_stage((sub, j + 1, sb_scr, mtb_scr, False), (sub, j, sa_scr, mta_scr, False))
            paired_stage((sub, j + 2, sa_scr, mta_scr, False), (sub, j + 1, sb_scr, mtb_scr, False))
            return 0

        lax.fori_loop(0, n_pairs, tile_pair, 0)

    def last_tiles_even(sub, i):
        paired_stage((sub, i - 1, sb_scr, mtb_scr, False), (sub, i - 2, sa_scr, mta_scr, False))
        softmax_stage(sub, i - 1, sb_scr, mtb_scr, False)

    def project(sub):
        slot = sub % 2
        q_rows = slice(sub * TQ, (sub + 1) * TQ)
        for hd in range(B_HEADS):
            inv_l = 1.0 / acc_scr[slot, hd, V_HEAD:V_HEAD + 1, :]
            ot_scr[slot, hd * V_HEAD:(hd + 1) * V_HEAD, :] = (acc_scr[slot, hd, :V_HEAD, :] * inv_l).astype(BF16)
        out = lax.dot_general(ot_scr[slot], wo_scr[...], TN_DIMS, preferred_element_type=F32)
        o_ref[0, q_rows, :] = h_ref[0, q_rows, :] + out

    for sub in range(Q_TILES_PER_STEP):
        i = Q_TILES_PER_STEP * k + sub
        if sub > 0:
            project(sub - 1)
        first_tiles(sub, i)
        tile_pairs(sub, jnp.maximum(i - 1, 0) >> 1)
        if sub % 2 == 1:
            softmax_stage(sub, i - 1, sa_scr, mta_scr, False)
        elif sub == 0:
            pl.when(k > 0)(functools.partial(last_tiles_even, sub, i))
        else:
            last_tiles_even(sub, i)
    project(Q_TILES_PER_STEP - 1)


def _attn(h, q, kn, kpe, vt, wo):
    tq2 = Q_TILES_PER_STEP * TQ
    return pl.pallas_call(
        _attn_kernel,
        grid=(BATCH, SEQ // tq2),
        in_specs=[pl.BlockSpec((1, tq2, D_MODEL), lambda b, k: (b, k, 0)),
                  pl.BlockSpec((1, B_HEADS, tq2, QK_PAD), lambda b, k: (b, 0, k, 0)),
                  pl.BlockSpec((1, B_HEADS, SEQ, QK_NOPE), lambda b, k: (b, 0, 0, 0)),
                  pl.BlockSpec((1, SEQ, LANES), lambda b, k: (b, 0, 0)),
                  pl.BlockSpec((1, SEQ // TK, B_HEADS * V_HEAD, TK), lambda b, k: (b, 0, 0, 0)),
                  _resident((B_HEADS * V_HEAD, D_MODEL))],
        out_specs=pl.BlockSpec((1, tq2, D_MODEL), lambda b, k: (b, k, 0)),
        out_shape=jax.ShapeDtypeStruct((BATCH, SEQ, D_MODEL), F32),
        scratch_shapes=[pltpu.VMEM((B_HEADS, TK, TQ), F32), pltpu.VMEM((B_HEADS, TK, TQ), F32),
                        pltpu.VMEM((B_HEADS, 1, TQ), F32), pltpu.VMEM((B_HEADS, 1, TQ), F32),
                        pltpu.VMEM((2, B_HEADS, 1, TQ), F32),
                        pltpu.VMEM((2, B_HEADS, V_HEAD + ONES_ROWS, TQ), F32),
                        pltpu.VMEM((2, B_HEADS * V_HEAD, TQ), BF16),
                        pltpu.VMEM((B_HEADS * V_HEAD, D_MODEL), BF16)],
        compiler_params=_params("arbitrary", "arbitrary"),
        name="mla_attention",
    )(h, q, kn, kpe, vt, wo)


def _row(v):
    return v.reshape(1, -1)


def _spread_rope(w):
    half = QK_ROPE // 2
    zeros = jnp.zeros(w.shape[:-1] + (half,), w.dtype)
    return jnp.concatenate([w[..., :half], zeros, w[..., half:], zeros], axis=-1)


def kernel(x, positions, norm_mix_g, norm_mlp_g, a_w_in, a_ln_v_g, a_ln_v_b, a_w_s, a_b_s, a_w_out,
           b_w_q_a, b_q_norm_g, b_w_q_b, b_w_o, kv_src_norm_g, kv_w_a, kv_a_norm_g, kv_w_b,
           mlp_w1, mlp_w2, final_norm_g):
    w_in = a_w_in[0]
    w_out = a_w_out[0]
    wkva = jnp.concatenate([kv_w_a[:, :KV_LORA], _spread_rope(kv_w_a[:, KV_LORA:])], axis=-1).astype(BF16)
    kvb = kv_w_b.reshape(KV_LORA, B_HEADS, QK_NOPE + V_HEAD)
    wk = kvb[:, :, :QK_NOPE].reshape(KV_LORA, B_HEADS * QK_NOPE).astype(BF16)
    wvt = kvb[:, :, QK_NOPE:].reshape(KV_LORA, B_HEADS * V_HEAD).T.astype(BF16)
    wqa = b_w_q_a[0]
    wqb = b_w_q_b[0].reshape(Q_LORA, B_HEADS, QK_NOPE + QK_ROPE)
    wqb = jnp.concatenate([wqb[:, :, :QK_NOPE], _spread_rope(wqb[:, :, QK_NOPE:])], axis=-1)
    wqb = wqb.reshape(Q_LORA, B_HEADS * QK_PAD).astype(BF16)
    wo = b_w_o[0]

    inv_freq = ROPE_THETA ** (-jnp.arange(0, QK_ROPE, 2, dtype=F32) / QK_ROPE)
    phase = jnp.concatenate([jnp.zeros((QK_ROPE // 2,), F32), jnp.full((QK_ROPE // 2,), HALF_PI, F32)])
    rope_tab = jnp.broadcast_to(jnp.concatenate([inv_freq, inv_freq, phase])[:, None], (2 * QK_ROPE, LANES))

    mix_g = norm_mix_g.reshape(DEPTH, 1, D_MODEL)
    mlp_g = norm_mlp_g.reshape(DEPTH, 1, D_MODEL)
    h = x.reshape(TOKENS, D_MODEL)
    h, w1, w2 = _gmlp(h, mix_g, w_in, _row(a_ln_v_g[0]), _row(a_ln_v_b[0]), a_w_s[0],
                      a_b_s[0].reshape(A_GROUPS, GMLP_BLOCK, 1), w_out, mlp_w1, mlp_w2)
    h, cos_tab, sin_tab = _mlp_rope(h, mlp_g, w1, w2,
                                    positions.reshape(TOKENS // TM_MLP, 1, TM_MLP),
                                    rope_tab, 0)

    h3 = h.reshape(BATCH, SEQ, D_MODEL)
    q, kn, kpe, vt = _proj(h3, cos_tab.reshape(BATCH, SEQ, LANES), sin_tab.reshape(BATCH, SEQ, LANES),
                           _row(kv_src_norm_g), mix_g, wkva, _row(kv_a_norm_g), wk, wvt,
                           wqa, _row(b_q_norm_g[0]), wqb)
    h3 = _attn(h3, q, kn, kpe, vt, wo)

    h = _mlp_final(h3.reshape(TOKENS, D_MODEL), mlp_g, w1, w2, _row(final_norm_g), 1)
    return h.reshape(BATCH, SEQ, D_MODEL)
```

```python
import functools

import jax
import jax.numpy as jnp
import numpy as np
from jax import lax
from jax.experimental import pallas as pl
from jax.experimental.pallas import tpu as pltpu

D_MODEL = 1024
BATCH = 8
SEQ = 2048
TOKENS = BATCH * SEQ
DEPTH = 2
CHUNK = 64
CHUNK_SHIFT = CHUNK.bit_length() - 1
GMLP_BLOCK = 128
GATE_DIM = 2 * D_MODEL
A_GROUPS = 8
A_GROUP_DIM = GATE_DIM // A_GROUPS
B_HEADS = 8
QK_NOPE = 128
QK_ROPE = 64
V_HEAD = 128
Q_LORA = 384
KV_LORA = 256
ROPE_THETA = 10000.0
D_FF = 4 * D_MODEL
EPS = 1e-6

LANES = 128
QK_PAD = 2 * LANES
VMEM_LIMIT = 56 * 1024 * 1024

TM = 512
TM_PROJ = 1024
TM_MLP = 1024
NCHUNK = 512
TQ = 256
TK = 256
Q_TILES_PER_STEP = 4
ONES_ROWS = 16

assert Q_TILES_PER_STEP % 2 == 0 and SEQ % (Q_TILES_PER_STEP * TQ) == 0 and TQ == TK
assert TM_MLP // (D_FF // NCHUNK) == LANES

F32 = jnp.float32
BF16 = jnp.bfloat16
NEG = float(np.finfo(np.float32).min)
SQRT_HALF = float(np.sqrt(0.5))
LOG2_E = float(np.log2(np.e))
HALF_PI = float(np.pi / 2)
NT_DIMS = (((1,), (1,)), ((), ()))
TN_DIMS = (((0,), (0,)), ((), ()))


def _rms(x, g):
    ms = jnp.mean(x * x, axis=-1, keepdims=True)
    return x * lax.rsqrt(ms + EPS) * g


def _resident(shape):
    nd = len(shape)
    return pl.BlockSpec(shape, lambda *_: (0,) * nd, pipeline_mode=pl.Buffered(1))


def _layer_gain(layer):
    return pl.BlockSpec((None, 1, D_MODEL), lambda *_: (layer, 0, 0), pipeline_mode=pl.Buffered(1))


def _params(*sem):
    return pltpu.CompilerParams(dimension_semantics=sem, vmem_limit_bytes=VMEM_LIMIT)


def _gelu_x2(z):
    return z * (1.0 + lax.erf(z * SQRT_HALF))


def _gmlp_kernel(h_ref, g_ref, w_in_ref, lng_ref, lnb_ref, ws_ref, bs_ref, w_out_ref, w1f_ref, w2f_ref,
                 o_ref, w1b_ref, w2b_ref, v_scr, vn_scr, gated_scr, ws_scr):
    w1b_ref[...] = w1f_ref[...].astype(BF16)
    w2b_ref[...] = w2f_ref[...].astype(BF16)

    h = h_ref[...]
    hn = _rms(h, g_ref[...]).astype(BF16)
    n_half = GATE_DIM // NCHUNK
    row_blocks = [slice(r * GMLP_BLOCK, (r + 1) * GMLP_BLOCK) for r in range(TM // GMLP_BLOCK)]

    for c in range(n_half):
        w_cols = slice(GATE_DIM + c * NCHUNK, GATE_DIM + (c + 1) * NCHUNK)
        v_scr[:, c * NCHUNK:(c + 1) * NCHUNK] = _gelu_x2(
            jnp.dot(hn, w_in_ref[:, w_cols].astype(BF16), preferred_element_type=F32))

    ii = lax.broadcasted_iota(jnp.int32, (GMLP_BLOCK, GMLP_BLOCK), 0) >> CHUNK_SHIFT
    jj = lax.broadcasted_iota(jnp.int32, (GMLP_BLOCK, GMLP_BLOCK), 1) >> CHUNK_SHIFT
    for g in range(A_GROUPS):
        ws_scr[g] = jnp.where(jj <= ii, 0.5 * ws_ref[g], 0.0).astype(BF16)

    for rows in row_blocks:
        v = v_scr[rows, :]
        mu = jnp.mean(v, axis=-1, keepdims=True)
        var = jnp.mean(jnp.square(v - mu), axis=-1, keepdims=True)
        vn_scr[rows, :] = ((v - mu) * lax.rsqrt(var + 4.0 * EPS) * lng_ref[...] + lnb_ref[...]).astype(BF16)

    for c in range(n_half):
        u = _gelu_x2(jnp.dot(hn, w_in_ref[:, c * NCHUNK:(c + 1) * NCHUNK].astype(BF16),
                          preferred_element_type=F32))
        for gl in range(NCHUNK // A_GROUP_DIM):
            g = c * (NCHUNK // A_GROUP_DIM) + gl
            cols = slice(g * A_GROUP_DIM, (g + 1) * A_GROUP_DIM)
            for rows in row_blocks:
                sv = jnp.dot(ws_scr[g], vn_scr[rows, cols], preferred_element_type=F32) + 0.5 * bs_ref[g]
                u_blk = u[rows, gl * A_GROUP_DIM:(gl + 1) * A_GROUP_DIM]
                gated_scr[rows, cols] = (u_blk * sv).astype(BF16)

    o_ref[...] = h + jnp.dot(gated_scr[...], w_out_ref[...].astype(BF16), preferred_element_type=F32)


def _gmlp(h, g, w_in, lng, lnb, ws, bs, w_out, w1f, w2f):
    steps = TOKENS // TM
    tok = pl.BlockSpec((TM, D_MODEL), lambda t: (t, 0))
    w1_slab = pl.BlockSpec((DEPTH, D_MODEL // steps, D_FF), lambda t: (0, t, 0))
    w2_slab = pl.BlockSpec((DEPTH, D_FF // steps, D_MODEL), lambda t: (0, t, 0))
    return pl.pallas_call(
        _gmlp_kernel,
        grid=(steps,),
        in_specs=[tok, _layer_gain(0), _resident((D_MODEL, 2 * GATE_DIM)),
                  _resident((1, GATE_DIM)), _resident((1, GATE_DIM)),
                  _resident((A_GROUPS, GMLP_BLOCK, GMLP_BLOCK)),
                  _resident((A_GROUPS, GMLP_BLOCK, 1)), _resident((GATE_DIM, D_MODEL)),
                  w1_slab, w2_slab],
        out_specs=[tok, w1_slab, w2_slab],
        out_shape=[jax.ShapeDtypeStruct((TOKENS, D_MODEL), F32),
                   jax.ShapeDtypeStruct((DEPTH, D_MODEL, D_FF), BF16),
                   jax.ShapeDtypeStruct((DEPTH, D_FF, D_MODEL), BF16)],
        scratch_shapes=[pltpu.VMEM((TM, GATE_DIM), F32), pltpu.VMEM((TM, GATE_DIM), BF16),
                        pltpu.VMEM((TM, GATE_DIM), BF16),
                        pltpu.VMEM((A_GROUPS, GMLP_BLOCK, GMLP_BLOCK), BF16)],
        compiler_params=_params("arbitrary"),
        name="gmlp_mixer",
    )(h, g, w_in, lng, lnb, ws, bs, w_out, w1f, w2f)


def _mlp_residual(h, g_ref, w1_ref, w2_ref, a_scr, anchors=None):
    inv_ms = 1.0 / (jnp.mean(h * h, axis=-1, keepdims=True) + EPS)
    hg = (h * g_ref[...]).astype(BF16)
    n_chunks = D_FF // NCHUNK
    slab = h.shape[0] // n_chunks
    for c in range(n_chunks):
        cols = slice(c * NCHUNK, (c + 1) * NCHUNK)
        a = jnp.maximum(jnp.dot(hg, w1_ref[:, cols], preferred_element_type=F32), 0.0)
        if anchors is None:
            a_scr[:, cols] = (a * a).astype(BF16)
        else:
            for r in range(n_chunks):
                rows = slice(r * slab, (r + 1) * slab)
                blk = a[rows] + anchors[c] if r == c else a[rows]
                a_scr[rows, cols] = (blk * blk).astype(BF16)
    return h + inv_ms * jnp.dot(a_scr[...], w2_ref[...], preferred_element_type=F32)


def _mlp_rope_kernel(h_ref, g_ref, w1_ref, w2_ref, pos_ref, rope_ref, o_ref, cos_ref, sin_ref, a_scr):
    half = QK_ROPE // 2
    n_slabs = D_FF // NCHUNK
    slab = TM_MLP // n_slabs
    pos = pos_ref[0].astype(F32)
    zeros = jnp.zeros((half, slab), F32)
    anchors = []
    for c in range(n_slabs):
        toks = slice(c * slab, (c + 1) * slab)
        sin_cos = jnp.sin(rope_ref[:QK_ROPE, :] * pos[:, toks] + rope_ref[QK_ROPE:, :])
        sin_t, cos_t = sin_cos[:half], sin_cos[half:]
        cos_tab = jnp.concatenate([cos_t, cos_t, cos_t, cos_t], axis=0).T
        sin_tab = jnp.concatenate([-sin_t, zeros, sin_t, zeros], axis=0).T
        cos_ref[toks, :] = cos_tab
        sin_ref[toks, :] = sin_tab
        anchors.append((cos_tab[:, 0:1] + sin_tab[:, 0:1]) * 0.0)
    o_ref[...] = _mlp_residual(h_ref[...], g_ref, w1_ref, w2_ref, a_scr, anchors)


def _mlp_final_kernel(h_ref, g_ref, w1_ref, w2_ref, gf_ref, o_ref, a_scr):
    o_ref[...] = _rms(_mlp_residual(h_ref[...], g_ref, w1_ref, w2_ref, a_scr), gf_ref[...])


def _mlp_specs(layer):
    tok = pl.BlockSpec((TM_MLP, D_MODEL), lambda t: (t, 0))
    w1 = pl.BlockSpec((None, D_MODEL, D_FF), lambda t: (layer, 0, 0), pipeline_mode=pl.Buffered(1))
    w2 = pl.BlockSpec((None, D_FF, D_MODEL), lambda t: (layer, 0, 0), pipeline_mode=pl.Buffered(1))
    return tok, [tok, _layer_gain(layer), w1, w2]


def _mlp_rope(h, g, w1, w2, pos, rope_tab, layer):
    tok, in_specs = _mlp_specs(layer)
    tab = pl.BlockSpec((TM_MLP, LANES), lambda t: (t, 0))
    return pl.pallas_call(
        _mlp_rope_kernel,
        grid=(TOKENS // TM_MLP,),
        in_specs=in_specs + [pl.BlockSpec((1, 1, TM_MLP), lambda t: (t, 0, 0)),
                             _resident((2 * QK_ROPE, LANES))],
        out_specs=[tok, tab, tab],
        out_shape=[jax.ShapeDtypeStruct((TOKENS, D_MODEL), F32),
                   jax.ShapeDtypeStruct((TOKENS, LANES), F32),
                   jax.ShapeDtypeStruct((TOKENS, LANES), F32)],
        scratch_shapes=[pltpu.VMEM((TM_MLP, D_FF), BF16)],
        compiler_params=_params("arbitrary"),
        name="mlp_rope",
    )(h, g, w1, w2, pos, rope_tab)


def _mlp_final(h, g, w1, w2, gf, layer):
    tok, in_specs = _mlp_specs(layer)
    return pl.pallas_call(
        _mlp_final_kernel,
        grid=(TOKENS // TM_MLP,),
        in_specs=in_specs + [_resident((1, D_MODEL))],
        out_specs=tok,
        out_shape=jax.ShapeDtypeStruct((TOKENS, D_MODEL), F32),
        scratch_shapes=[pltpu.VMEM((TM_MLP, D_FF), BF16)],
        compiler_params=_params("arbitrary"),
        name="mlp_final",
    )(h, g, w1, w2, gf)


def _rope(x, cos, sin_signed):
    return x * cos + pltpu.roll(x, LANES // 2, 1) * sin_signed


def _proj_kernel(h_ref, cos_ref, sin_ref, gkv_ref, gq_ref, wkva_ref, gkva_ref, wk_ref, wvt_ref,
                 wqa_ref, gqa_ref, wqb_ref, q_ref, kn_ref, kpe_ref, vt_ref):
    h = h_ref[0]
    cos = cos_ref[0]
    sin_signed = sin_ref[0]

    ms = jnp.mean(h * h, axis=-1, keepdims=True) + EPS
    r = lax.rsqrt(ms)
    eps_latent = EPS * ms

    def latent_rms(x, g):
        return x * lax.rsqrt(jnp.mean(x * x, axis=-1, keepdims=True) + eps_latent) * g

    ckv = jnp.dot((h * gkv_ref[...]).astype(BF16), wkva_ref[...], preferred_element_type=F32)
    c_kv = latent_rms(ckv[:, :KV_LORA], gkva_ref[...]).astype(BF16)
    kpe_ref[0] = (_rope(ckv[:, KV_LORA:], cos, sin_signed) * r).astype(BF16)
    kn = jnp.dot(c_kv, wk_ref[...], preferred_element_type=F32).astype(BF16)
    for hd in range(B_HEADS):
        kn_ref[0, hd] = kn[:, hd * QK_NOPE:(hd + 1) * QK_NOPE]
    vt = lax.dot_general(wvt_ref[...], c_kv, NT_DIMS, preferred_element_type=F32).astype(BF16)
    for kb in range(TM_PROJ // TK):
        vt_ref[0, kb] = vt[:, kb * TK:(kb + 1) * TK]

    scale = (QK_NOPE + QK_ROPE) ** -0.5 * LOG2_E
    cq = jnp.dot((h * gq_ref[...]).astype(BF16), wqa_ref[...].astype(BF16), preferred_element_type=F32)
    cq = latent_rms(cq, gqa_ref[...] * scale).astype(BF16)
    for hd in range(B_HEADS):
        q = jnp.dot(cq, wqb_ref[:, hd * QK_PAD:(hd + 1) * QK_PAD], preferred_element_type=F32)
        q_ref[0, hd, :, :QK_NOPE] = q[:, :QK_NOPE].astype(BF16)
        q_ref[0, hd, :, QK_NOPE:] = _rope(q[:, QK_NOPE:], cos, sin_signed).astype(BF16)


def _proj(h, cos_tab, sin_tab, gkv, gq, wkva, gkva, wk, wvt, wqa, gqa, wqb):
    n_t = SEQ // TM_PROJ
    tab = pl.BlockSpec((1, TM_PROJ, LANES), lambda b, t: (b, t, 0))
    return pl.pallas_call(
        _proj_kernel,
        grid=(BATCH, n_t),
        in_specs=[pl.BlockSpec((1, TM_PROJ, D_MODEL), lambda b, t: (b, t, 0)), tab, tab,
                  _resident((1, D_MODEL)), _layer_gain(1),
                  _resident((D_MODEL, KV_LORA + LANES)), _resident((1, KV_LORA)),
                  _resident((KV_LORA, B_HEADS * QK_NOPE)), _resident((B_HEADS * V_HEAD, KV_LORA)),
                  _resident((D_MODEL, Q_LORA)), _resident((1, Q_LORA)),
                  _resident((Q_LORA, B_HEADS * QK_PAD))],
        out_specs=[pl.BlockSpec((1, B_HEADS, TM_PROJ, QK_PAD), lambda b, t: (b, 0, t, 0)),
                   pl.BlockSpec((1, B_HEADS, TM_PROJ, QK_NOPE), lambda b, t: (b, 0, t, 0)),
                   pl.BlockSpec((1, TM_PROJ, LANES), lambda b, t: (b, t, 0)),
                   pl.BlockSpec((1, TM_PROJ // TK, B_HEADS * V_HEAD, TK), lambda b, t: (b, t, 0, 0))],
        out_shape=[jax.ShapeDtypeStruct((BATCH, B_HEADS, SEQ, QK_PAD), BF16),
                   jax.ShapeDtypeStruct((BATCH, B_HEADS, SEQ, QK_NOPE), BF16),
                   jax.ShapeDtypeStruct((BATCH, SEQ, LANES), BF16),
                   jax.ShapeDtypeStruct((BATCH, SEQ // TK, B_HEADS * V_HEAD, TK), BF16)],
        compiler_params=_params("arbitrary", "arbitrary"),
        name="qkv_proj",
    )(h, cos_tab, sin_tab, gkv, gq, wkva, gkva, wk, wvt, wqa, gqa, wqb)


def _attn_kernel(h_ref, q_ref, kn_ref, kpe_ref, vt_ref, wo_ref, o_ref, sa_scr, sb_scr, mta_scr, mtb_scr,
                 m_scr, acc_scr, ot_scr, wo_scr):
    k = pl.program_id(1)
    kc_id = lax.broadcasted_iota(jnp.int32, (TK, TQ), 0) >> CHUNK_SHIFT
    qc_id = lax.broadcasted_iota(jnp.int32, (TK, TQ), 1) >> CHUNK_SHIFT
    diag_ok = kc_id <= qc_id
    ones_rows = jnp.ones((ONES_ROWS, TK), BF16)

    @pl.when(jnp.logical_and(pl.program_id(0) == 0, k == 0))
    def _():
        wo_scr[...] = wo_ref[...].astype(BF16)

    def score_head(hd, sub, j, s_scr, mt_scr, diagonal):
        rows = pl.ds(pl.multiple_of(j * TK, TK), TK)
        k_cat = jnp.concatenate([kn_ref[0, hd, rows, :], kpe_ref[0, rows, :]], axis=-1)
        q_h = q_ref[0, hd, sub * TQ:(sub + 1) * TQ, :]
        s = lax.dot_general(k_cat, q_h, NT_DIMS, preferred_element_type=F32)
        if diagonal:
            s = jnp.where(diag_ok, s, NEG)
        s_scr[hd] = s
        mt_scr[hd] = jnp.max(s, axis=0, keepdims=True)

    def softmax_head(hd, sub, j, s_scr, mt_scr, first):
        slot = sub % 2
        v_aug = jnp.concatenate([vt_ref[0, j, hd * V_HEAD:(hd + 1) * V_HEAD, :], ones_rows], axis=0)
        if first:
            m_new = mt_scr[hd]
            p = jnp.exp2(s_scr[hd] - m_new)
            acc_scr[slot, hd] = jnp.dot(v_aug, p.astype(BF16), preferred_element_type=F32)
        else:
            m = m_scr[slot, hd]
            m_new = jnp.maximum(m, mt_scr[hd])
            alpha = jnp.exp2(m - m_new)
            p = jnp.exp2(s_scr[hd] - m_new)
            acc_scr[slot, hd] = alpha * acc_scr[slot, hd] + jnp.dot(v_aug, p.astype(BF16),
                                                                    preferred_element_type=F32)
        m_scr[slot, hd] = m_new

    def score_stage(*args):
        for hd in range(B_HEADS):
            score_head(hd, *args)

    def softmax_stage(*args):
        for hd in range(B_HEADS):
            softmax_head(hd, *args)

    def paired_stage(score_args, softmax_args):
        for hd in range(B_HEADS):
            score_head(hd, *score_args)
            softmax_head(hd, *softmax_args)

    def first_tiles(sub, i):
        score_stage(sub, i, sb_scr, mtb_scr, True)
        paired_stage((sub, 0, sa_scr, mta_scr, False), (sub, i, sb_scr, mtb_scr, True))

    def tile_pairs(sub, n_pairs):
        def tile_pair(u, _):
            j = 2 * u
            paired_stage((sub, j + 1, sb_scr, mtb_scr, False), (sub, j, sa_scr, mta_scr, False))
            paired_stage((sub, j + 2, sa_scr, mta_scr, False), (sub, j + 1, sb_scr, mtb_scr, False))
            return 0

        lax.fori_loop(0, n_pairs, tile_pair, 0)

    def last_tiles_even(sub, i):
        paired_stage((sub, i - 1, sb_scr, mtb_scr, False), (sub, i - 2, sa_scr, mta_scr, False))
        softmax_stage(sub, i - 1, sb_scr, mtb_scr, False)

    def project(sub):
        slot = sub % 2
        q_rows = slice(sub * TQ, (sub + 1) * TQ)
        for hd in range(B_HEADS):
            inv_l = 1.0 / acc_scr[slot, hd, V_HEAD:V_HEAD + 1, :]
            ot_scr[slot, hd * V_HEAD:(hd + 1) * V_HEAD, :] = (acc_scr[slot, hd, :V_HEAD, :] * inv_l).astype(BF16)
        out = lax.dot_general(ot_scr[slot], wo_scr[...], TN_DIMS, preferred_element_type=F32)
        o_ref[0, q_rows, :] = h_ref[0, q_rows, :] + out

    for sub in range(Q_TILES_PER_STEP):
        i = Q_TILES_PER_STEP * k + sub
        if sub > 0:
            project(sub - 1)
        first_tiles(sub, i)
        tile_pairs(sub, jnp.maximum(i - 1, 0) >> 1)
        if sub % 2 == 1:
            softmax_stage(sub, i - 1, sa_scr, mta_scr, False)
        elif sub == 0:
            pl.when(k > 0)(functools.partial(last_tiles_even, sub, i))
        else:
            last_tiles_even(sub, i)
    project(Q_TILES_PER_STEP - 1)


def _attn(h, q, kn, kpe, vt, wo):
    tq2 = Q_TILES_PER_STEP * TQ
    return pl.pallas_call(
        _attn_kernel,
        grid=(BATCH, SEQ // tq2),
        in_specs=[pl.BlockSpec((1, tq2, D_MODEL), lambda b, k: (b, k, 0)),
                  pl.BlockSpec((1, B_HEADS, tq2, QK_PAD), lambda b, k: (b, 0, k, 0)),
                  pl.BlockSpec((1, B_HEADS, SEQ, QK_NOPE), lambda b, k: (b, 0, 0, 0)),
                  pl.BlockSpec((1, SEQ, LANES), lambda b, k: (b, 0, 0)),
                  pl.BlockSpec((1, SEQ // TK, B_HEADS * V_HEAD, TK), lambda b, k: (b, 0, 0, 0)),
                  _resident((B_HEADS * V_HEAD, D_MODEL))],
        out_specs=pl.BlockSpec((1, tq2, D_MODEL), lambda b, k: (b, k, 0)),
        out_shape=jax.ShapeDtypeStruct((BATCH, SEQ, D_MODEL), F32),
        scratch_shapes=[pltpu.VMEM((B_HEADS, TK, TQ), F32), pltpu.VMEM((B_HEADS, TK, TQ), F32),
                        pltpu.VMEM((B_HEADS, 1, TQ), F32), pltpu.VMEM((B_HEADS, 1, TQ), F32),
                        pltpu.VMEM((2, B_HEADS, 1, TQ), F32),
                        pltpu.VMEM((2, B_HEADS, V_HEAD + ONES_ROWS, TQ), F32),
                        pltpu.VMEM((2, B_HEADS * V_HEAD, TQ), BF16),
                        pltpu.VMEM((B_HEADS * V_HEAD, D_MODEL), BF16)],
        compiler_params=_params("arbitrary", "arbitrary"),
        name="mla_attention",
    )(h, q, kn, kpe, vt, wo)


def _row(v):
    return v.reshape(1, -1)


def _spread_rope(w):
    half = QK_ROPE // 2
    zeros = jnp.zeros(w.shape[:-1] + (half,), w.dtype)
    return jnp.concatenate([w[..., :half], zeros, w[..., half:], zeros], axis=-1)


def kernel(x, positions, norm_mix_g, norm_mlp_g, a_w_in, a_ln_v_g, a_ln_v_b, a_w_s, a_b_s, a_w_out,
           b_w_q_a, b_q_norm_g, b_w_q_b, b_w_o, kv_src_norm_g, kv_w_a, kv_a_norm_g, kv_w_b,
           mlp_w1, mlp_w2, final_norm_g):
    w_in = a_w_in[0]
    w_out = a_w_out[0]
    wkva = jnp.concatenate([kv_w_a[:, :KV_LORA], _spread_rope(kv_w_a[:, KV_LORA:])], axis=-1).astype(BF16)
    kvb = kv_w_b.reshape(KV_LORA, B_HEADS, QK_NOPE + V_HEAD)
    wk = kvb[:, :, :QK_NOPE].reshape(KV_LORA, B_HEADS * QK_NOPE).astype(BF16)
    wvt = kvb[:, :, QK_NOPE:].reshape(KV_LORA, B_HEADS * V_HEAD).T.astype(BF16)
    wqa = b_w_q_a[0]
    wqb = b_w_q_b[0].reshape(Q_LORA, B_HEADS, QK_NOPE + QK_ROPE)
    wqb = jnp.concatenate([wqb[:, :, :QK_NOPE], _spread_rope(wqb[:, :, QK_NOPE:])], axis=-1)
    wqb = wqb.reshape(Q_LORA, B_HEADS * QK_PAD).astype(BF16)
    wo = b_w_o[0]

    inv_freq = ROPE_THETA ** (-jnp.arange(0, QK_ROPE, 2, dtype=F32) / QK_ROPE)
    phase = jnp.concatenate([jnp.zeros((QK_ROPE // 2,), F32), jnp.full((QK_ROPE // 2,), HALF_PI, F32)])
    rope_tab = jnp.broadcast_to(jnp.concatenate([inv_freq, inv_freq, phase])[:, None], (2 * QK_ROPE, LANES))

    mix_g = norm_mix_g.reshape(DEPTH, 1, D_MODEL)
    mlp_g = norm_mlp_g.reshape(DEPTH, 1, D_MODEL)
    h = x.reshape(TOKENS, D_MODEL)
    h, w1, w2 = _gmlp(h, mix_g, w_in, _row(a_ln_v_g[0]), _row(a_ln_v_b[0]), a_w_s[0],
                      a_b_s[0].reshape(A_GROUPS, GMLP_BLOCK, 1), w_out, mlp_w1, mlp_w2)
    h, cos_tab, sin_tab = _mlp_rope(h, mlp_g, w1, w2,
                                    positions.reshape(TOKENS // TM_MLP, 1, TM_MLP),
                                    rope_tab, 0)

    h3 = h.reshape(BATCH, SEQ, D_MODEL)
    q, kn, kpe, vt = _proj(h3, cos_tab.reshape(BATCH, SEQ, LANES), sin_tab.reshape(BATCH, SEQ, LANES),
                           _row(kv_src_norm_g), mix_g, wkva, _row(kv_a_norm_g), wk, wvt,
                           wqa, _row(b_q_norm_g[0]), wqb)
    h3 = _attn(h3, q, kn, kpe, vt, wo)

    h = _mlp_final(h3.reshape(TOKENS, D_MODEL), mlp_g, w1, w2, _row(final_norm_g), 1)
    return h.reshape(BATCH, SEQ, D_MODEL)
```

```python
import functools

import jax
import jax.numpy as jnp
import numpy as np
from jax import lax
from jax.experimental import pallas as pl
from jax.experimental.pallas import tpu as pltpu

D_MODEL = 1024
BATCH = 8
SEQ = 2048
TOKENS = BATCH * SEQ
DEPTH = 2
CHUNK = 64
CHUNK_SHIFT = CHUNK.bit_length() - 1
GMLP_BLOCK = 128
GATE_DIM = 2 * D_MODEL
A_GROUPS = 8
A_GROUP_DIM = GATE_DIM // A_GROUPS
B_HEADS = 8
QK_NOPE = 128
QK_ROPE = 64
V_HEAD = 128
Q_LORA = 384
KV_LORA = 256
ROPE_THETA = 10000.0
D_FF = 4 * D_MODEL
EPS = 1e-6

LANES = 128
QK_PAD = 2 * LANES
VMEM_LIMIT = 56 * 1024 * 1024

TM = 512
TM_PROJ = 1024
TM_MLP = 1024
NCHUNK = 512
TQ = 256
TK = 256
Q_TILES_PER_STEP = 4
ONES_ROWS = 16

assert Q_TILES_PER_STEP % 2 == 0 and SEQ % (Q_TILES_PER_STEP * TQ) == 0 and TQ == TK
assert TM_MLP // (D_FF // NCHUNK) == LANES

F32 = jnp.float32
BF16 = jnp.bfloat16
NEG = float(np.finfo(np.float32).min)
SQRT_HALF = float(np.sqrt(0.5))
LOG2_E = float(np.log2(np.e))
HALF_PI = float(np.pi / 2)
NT_DIMS = (((1,), (1,)), ((), ()))
TN_DIMS = (((0,), (0,)), ((), ()))


def _rms(x, g):
    ms = jnp.mean(x * x, axis=-1, keepdims=True)
    return x * lax.rsqrt(ms + EPS) * g


def _resident(shape):
    nd = len(shape)
    return pl.BlockSpec(shape, lambda *_: (0,) * nd, pipeline_mode=pl.Buffered(1))


def _layer_gain(layer):
    return pl.BlockSpec((None, 1, D_MODEL), lambda *_: (layer, 0, 0), pipeline_mode=pl.Buffered(1))


def _params(*sem):
    return pltpu.CompilerParams(dimension_semantics=sem, vmem_limit_bytes=VMEM_LIMIT)


def _gelu_x2_unscaled(z0, r_sqrt_half):
    return z0 * (1.0 + lax.erf(z0 * r_sqrt_half))


def _gmlp_kernel(h_ref, g_ref, w_in_ref, lng_ref, lnb_ref, ws_ref, bs_ref, w_out_ref, w1f_ref, w2f_ref,
                 o_ref, w1b_ref, w2b_ref, v_scr, vn_scr, gated_scr, ws_scr):
    w1b_ref[...] = w1f_ref[...].astype(BF16)
    w2b_ref[...] = w2f_ref[...].astype(BF16)

    h = h_ref[...]
    ms = jnp.mean(h * h, axis=-1, keepdims=True) + EPS
    r = lax.rsqrt(ms)
    r_sqrt_half = r * SQRT_HALF
    hn = (h * g_ref[...]).astype(BF16)
    n_half = GATE_DIM // NCHUNK
    row_blocks = [slice(r * GMLP_BLOCK, (r + 1) * GMLP_BLOCK) for r in range(TM // GMLP_BLOCK)]

    for c in range(n_half):
        w_cols = slice(GATE_DIM + c * NCHUNK, GATE_DIM + (c + 1) * NCHUNK)
        v_scr[:, c * NCHUNK:(c + 1) * NCHUNK] = _gelu_x2_unscaled(
            jnp.dot(hn, w_in_ref[:, w_cols].astype(BF16), preferred_element_type=F32), r_sqrt_half)

    ii = lax.broadcasted_iota(jnp.int32, (GMLP_BLOCK, GMLP_BLOCK), 0) >> CHUNK_SHIFT
    jj = lax.broadcasted_iota(jnp.int32, (GMLP_BLOCK, GMLP_BLOCK), 1) >> CHUNK_SHIFT
    for g in range(A_GROUPS):
        ws_scr[g] = jnp.where(jj <= ii, 0.5 * ws_ref[g], 0.0).astype(BF16)

    for rows in row_blocks:
        v = v_scr[rows, :]
        mu = jnp.mean(v, axis=-1, keepdims=True)
        var = jnp.mean(jnp.square(v - mu), axis=-1, keepdims=True)
        eps_rows = 4.0 * EPS * ms[rows]
        vn_scr[rows, :] = ((v - mu) * lax.rsqrt(var + eps_rows) * lng_ref[...] + lnb_ref[...]).astype(BF16)

    for c in range(n_half):
        u = _gelu_x2_unscaled(jnp.dot(hn, w_in_ref[:, c * NCHUNK:(c + 1) * NCHUNK].astype(BF16),
                                      preferred_element_type=F32), r_sqrt_half)
        for gl in range(NCHUNK // A_GROUP_DIM):
            g = c * (NCHUNK // A_GROUP_DIM) + gl
            cols = slice(g * A_GROUP_DIM, (g + 1) * A_GROUP_DIM)
            for rows in row_blocks:
                sv = jnp.dot(ws_scr[g], vn_scr[rows, cols], preferred_element_type=F32) + 0.5 * bs_ref[g]
                u_blk = u[rows, gl * A_GROUP_DIM:(gl + 1) * A_GROUP_DIM]
                gated_scr[rows, cols] = (u_blk * sv).astype(BF16)

    o_ref[...] = h + r * jnp.dot(gated_scr[...], w_out_ref[...].astype(BF16), preferred_element_type=F32)


def _gmlp(h, g, w_in, lng, lnb, ws, bs, w_out, w1f, w2f):
    steps = TOKENS // TM
    tok = pl.BlockSpec((TM, D_MODEL), lambda t: (t, 0))
    w1_slab = pl.BlockSpec((DEPTH, D_MODEL // steps, D_FF), lambda t: (0, t, 0))
    w2_slab = pl.BlockSpec((DEPTH, D_FF // steps, D_MODEL), lambda t: (0, t, 0))
    return pl.pallas_call(
        _gmlp_kernel,
        grid=(steps,),
        in_specs=[tok, _layer_gain(0), _resident((D_MODEL, 2 * GATE_DIM)),
                  _resident((1, GATE_DIM)), _resident((1, GATE_DIM)),
                  _resident((A_GROUPS, GMLP_BLOCK, GMLP_BLOCK)),
                  _resident((A_GROUPS, GMLP_BLOCK, 1)), _resident((GATE_DIM, D_MODEL)),
                  w1_slab, w2_slab],
        out_specs=[tok, w1_slab, w2_slab],
        out_shape=[jax.ShapeDtypeStruct((TOKENS, D_MODEL), F32),
                   jax.ShapeDtypeStruct((DEPTH, D_MODEL, D_FF), BF16),
                   jax.ShapeDtypeStruct((DEPTH, D_FF, D_MODEL), BF16)],
        scratch_shapes=[pltpu.VMEM((TM, GATE_DIM), F32), pltpu.VMEM((TM, GATE_DIM), BF16),
                        pltpu.VMEM((TM, GATE_DIM), BF16),
                        pltpu.VMEM((A_GROUPS, GMLP_BLOCK, GMLP_BLOCK), BF16)],
        compiler_params=_params("arbitrary"),
        name="gmlp_mixer",
    )(h, g, w_in, lng, lnb, ws, bs, w_out, w1f, w2f)


def _mlp_residual(h, g_ref, w1_ref, w2_ref, a_scr, anchors=None):
    inv_ms = 1.0 / (jnp.mean(h * h, axis=-1, keepdims=True) + EPS)
    hg = (h * g_ref[...]).astype(BF16)
    n_chunks = D_FF // NCHUNK
    slab = h.shape[0] // n_chunks
    for c in range(n_chunks):
        cols = slice(c * NCHUNK, (c + 1) * NCHUNK)
        a = jnp.maximum(jnp.dot(hg, w1_ref[:, cols], preferred_element_type=F32), 0.0)
        if anchors is None:
            a_scr[:, cols] = (a * a).astype(BF16)
        else:
            for r in range(n_chunks):
                rows = slice(r * slab, (r + 1) * slab)
                blk = a[rows] + anchors[c] if r == c else a[rows]
                a_scr[rows, cols] = (blk * blk).astype(BF16)
    return h + inv_ms * jnp.dot(a_scr[...], w2_ref[...], preferred_element_type=F32)


def _mlp_rope_kernel(h_ref, g_ref, w1_ref, w2_ref, pos_ref, rope_ref, o_ref, cos_ref, sin_ref, a_scr):
    half = QK_ROPE // 2
    n_slabs = D_FF // NCHUNK
    slab = TM_MLP // n_slabs
    pos = pos_ref[0].astype(F32)
    zeros = jnp.zeros((half, slab), F32)
    anchors = []
    for c in range(n_slabs):
        toks = slice(c * slab, (c + 1) * slab)
        sin_cos = jnp.sin(rope_ref[:QK_ROPE, :] * pos[:, toks] + rope_ref[QK_ROPE:, :])
        sin_t, cos_t = sin_cos[:half], sin_cos[half:]
        cos_tab = jnp.concatenate([cos_t, cos_t, cos_t, cos_t], axis=0).T
        sin_tab = jnp.concatenate([-sin_t, zeros, sin_t, zeros], axis=0).T
        cos_ref[toks, :] = cos_tab
        sin_ref[toks, :] = sin_tab
        anchors.append((cos_tab[:, 0:1] + sin_tab[:, 0:1]) * 0.0)
    o_ref[...] = _mlp_residual(h_ref[...], g_ref, w1_ref, w2_ref, a_scr, anchors)


def _mlp_final_kernel(h_ref, g_ref, w1_ref, w2_ref, gf_ref, o_ref, a_scr):
    o_ref[...] = _rms(_mlp_residual(h_ref[...], g_ref, w1_ref, w2_ref, a_scr), gf_ref[...])


def _mlp_specs(layer):
    tok = pl.BlockSpec((TM_MLP, D_MODEL), lambda t: (t, 0))
    w1 = pl.BlockSpec((None, D_MODEL, D_FF), lambda t: (layer, 0, 0), pipeline_mode=pl.Buffered(1))
    w2 = pl.BlockSpec((None, D_FF, D_MODEL), lambda t: (layer, 0, 0), pipeline_mode=pl.Buffered(1))
    return tok, [tok, _layer_gain(layer), w1, w2]


def _mlp_rope(h, g, w1, w2, pos, rope_tab, layer):
    tok, in_specs = _mlp_specs(layer)
    tab = pl.BlockSpec((TM_MLP, LANES), lambda t: (t, 0))
    return pl.pallas_call(
        _mlp_rope_kernel,
        grid=(TOKENS // TM_MLP,),
        in_specs=in_specs + [pl.BlockSpec((1, 1, TM_MLP), lambda t: (t, 0, 0)),
                             _resident((2 * QK_ROPE, LANES))],
        out_specs=[tok, tab, tab],
        out_shape=[jax.ShapeDtypeStruct((TOKENS, D_MODEL), F32),
                   jax.ShapeDtypeStruct((TOKENS, LANES), F32),
                   jax.ShapeDtypeStruct((TOKENS, LANES), F32)],
        scratch_shapes=[pltpu.VMEM((TM_MLP, D_FF), BF16)],
        compiler_params=_params("arbitrary"),
        name="mlp_rope",
    )(h, g, w1, w2, pos, rope_tab)


def _mlp_final(h, g, w1, w2, gf, layer):
    tok, in_specs = _mlp_specs(layer)
    return pl.pallas_call(
        _mlp_final_kernel,
        grid=(TOKENS // TM_MLP,),
        in_specs=in_specs + [_resident((1, D_MODEL))],
        out_specs=tok,
        out_shape=jax.ShapeDtypeStruct((TOKENS, D_MODEL), F32),
        scratch_shapes=[pltpu.VMEM((TM_MLP, D_FF), BF16)],
        compiler_params=_params("arbitrary"),
        name="mlp_final",
    )(h, g, w1, w2, gf)


def _rope(x, cos, sin_signed):
    return x * cos + pltpu.roll(x, LANES // 2, 1) * sin_signed


def _proj_kernel(h_ref, cos_ref, sin_ref, gkv_ref, gq_ref, wkva_ref, gkva_ref, wk_ref, wvt_ref,
                 wqa_ref, gqa_ref, wqb_ref, q_ref, kn_ref, kpe_ref, vt_ref):
    h = h_ref[0]
    cos = cos_ref[0]
    sin_signed = sin_ref[0]

    ms = jnp.mean(h * h, axis=-1, keepdims=True) + EPS
    r = lax.rsqrt(ms)
    eps_latent = EPS * ms

    def latent_rms(x, g):
        return x * lax.rsqrt(jnp.mean(x * x, axis=-1, keepdims=True) + eps_latent) * g

    ckv = jnp.dot((h * gkv_ref[...]).astype(BF16), wkva_ref[...], preferred_element_type=F32)
    c_kv = latent_rms(ckv[:, :KV_LORA], gkva_ref[...]).astype(BF16)
    kpe_ref[0] = (_rope(ckv[:, KV_LORA:], cos, sin_signed) * r).astype(BF16)
    kn = jnp.dot(c_kv, wk_ref[...], preferred_element_type=F32).astype(BF16)
    for hd in range(B_HEADS):
        kn_ref[0, hd] = kn[:, hd * QK_NOPE:(hd + 1) * QK_NOPE]
    vt = lax.dot_general(wvt_ref[...], c_kv, NT_DIMS, preferred_element_type=F32).astype(BF16)
    for kb in range(TM_PROJ // TK):
        vt_ref[0, kb] = vt[:, kb * TK:(kb + 1) * TK]

    scale = (QK_NOPE + QK_ROPE) ** -0.5 * LOG2_E
    cq = jnp.dot((h * gq_ref[...]).astype(BF16), wqa_ref[...].astype(BF16), preferred_element_type=F32)
    cq = latent_rms(cq, gqa_ref[...] * scale).astype(BF16)
    for hd in range(B_HEADS):
        q = jnp.dot(cq, wqb_ref[:, hd * QK_PAD:(hd + 1) * QK_PAD], preferred_element_type=F32)
        q_ref[0, hd, :, :QK_NOPE] = q[:, :QK_NOPE].astype(BF16)
        q_ref[0, hd, :, QK_NOPE:] = _rope(q[:, QK_NOPE:], cos, sin_signed).astype(BF16)


def _proj(h, cos_tab, sin_tab, gkv, gq, wkva, gkva, wk, wvt, wqa, gqa, wqb):
    n_t = SEQ // TM_PROJ
    tab = pl.BlockSpec((1, TM_PROJ, LANES), lambda b, t: (b, t, 0))
    return pl.pallas_call(
        _proj_kernel,
        grid=(BATCH, n_t),
        in_specs=[pl.BlockSpec((1, TM_PROJ, D_MODEL), lambda b, t: (b, t, 0)), tab, tab,
                  _resident((1, D_MODEL)), _layer_gain(1),
                  _resident((D_MODEL, KV_LORA + LANES)), _resident((1, KV_LORA)),
                  _resident((KV_LORA, B_HEADS * QK_NOPE)), _resident((B_HEADS * V_HEAD, KV_LORA)),
                  _resident((D_MODEL, Q_LORA)), _resident((1, Q_LORA)),
                  _resident((Q_LORA, B_HEADS * QK_PAD))],
        out_specs=[pl.BlockSpec((1, B_HEADS, TM_PROJ, QK_PAD), lambda b, t: (b, 0, t, 0)),
                   pl.BlockSpec((1, B_HEADS, TM_PROJ, QK_NOPE), lambda b, t: (b, 0, t, 0)),
                   pl.BlockSpec((1, TM_PROJ, LANES), lambda b, t: (b, t, 0)),
                   pl.BlockSpec((1, TM_PROJ // TK, B_HEADS * V_HEAD, TK), lambda b, t: (b, t, 0, 0))],
        out_shape=[jax.ShapeDtypeStruct((BATCH, B_HEADS, SEQ, QK_PAD), BF16),
                   jax.ShapeDtypeStruct((BATCH, B_HEADS, SEQ, QK_NOPE), BF16),
                   jax.ShapeDtypeStruct((BATCH, SEQ, LANES), BF16),
                   jax.ShapeDtypeStruct((BATCH, SEQ // TK, B_HEADS * V_HEAD, TK), BF16)],
        compiler_params=_params("arbitrary", "arbitrary"),
        name="qkv_proj",
    )(h, cos_tab, sin_tab, gkv, gq, wkva, gkva, wk, wvt, wqa, gqa, wqb)


def _attn_kernel(h_ref, q_ref, kn_ref, kpe_ref, vt_ref, wo_ref, o_ref, sa_scr, sb_scr, mta_scr, mtb_scr,
                 m_scr, acc_scr, ot_scr, wo_scr):
    k = pl.program_id(1)
    kc_id = lax.broadcasted_iota(jnp.int32, (TK, TQ), 0) >> CHUNK_SHIFT
    qc_id = lax.broadcasted_iota(jnp.int32, (TK, TQ), 1) >> CHUNK_SHIFT
    diag_ok = kc_id <= qc_id
    ones_rows = jnp.ones((ONES_ROWS, TK), BF16)

    @pl.when(jnp.logical_and(pl.program_id(0) == 0, k == 0))
    def _():
        wo_scr[...] = wo_ref[...].astype(BF16)

    def score_head(hd, sub, j, s_scr, mt_scr, diagonal):
        rows = pl.ds(pl.multiple_of(j * TK, TK), TK)
        k_cat = jnp.concatenate([kn_ref[0, hd, rows, :], kpe_ref[0, rows, :]], axis=-1)
        q_h = q_ref[0, hd, sub * TQ:(sub + 1) * TQ, :]
        s = lax.dot_general(k_cat, q_h, NT_DIMS, preferred_element_type=F32)
        if diagonal:
            s = jnp.where(diag_ok, s, NEG)
        s_scr[hd] = s
        mt_scr[hd] = jnp.max(s, axis=0, keepdims=True)

    def softmax_head(hd, sub, j, s_scr, mt_scr, first):
        slot = sub % 2
        v_aug = jnp.concatenate([vt_ref[0, j, hd * V_HEAD:(hd + 1) * V_HEAD, :], ones_rows], axis=0)
        if first:
            m_new = mt_scr[hd]
            p = jnp.exp2(s_scr[hd] - m_new)
            acc_scr[slot, hd] = jnp.dot(v_aug, p.astype(BF16), preferred_element_type=F32)
        else:
            m = m_scr[slot, hd]
            m_new = jnp.maximum(m, mt_scr[hd])
            alpha = jnp.exp2(m - m_new)
            p = jnp.exp2(s_scr[hd] - m_new)
            acc_scr[slot, hd] = alpha * acc_scr[slot, hd] + jnp.dot(v_aug, p.astype(BF16),
                                                                    preferred_element_type=F32)
        m_scr[slot, hd] = m_new

    def score_stage(*args):
        for hd in range(B_HEADS):
            score_head(hd, *args)

    def softmax_stage(*args):
        for hd in range(B_HEADS):
            softmax_head(hd, *args)

    def paired_stage(score_args, softmax_args):
        for hd in range(B_HEADS):
            score_head(hd, *score_args)
            softmax_head(hd, *softmax_args)

    def first_tiles(sub, i):
        score_stage(sub, i, sb_scr, mtb_scr, True)
        paired_stage((sub, 0, sa_scr, mta_scr, False), (sub, i, sb_scr, mtb_scr, True))

    def tile_pairs(sub, n_pairs):
        def tile_pair(u, _):
            j = 2 * u
            paired_stage((sub, j + 1, sb_scr, mtb_scr, False), (sub, j, sa_scr, mta_scr, False))
            paired_stage((sub, j + 2, sa_scr, mta_scr, False), (sub, j + 1, sb_scr, mtb_scr, False))
            return 0

        lax.fori_loop(0, n_pairs, tile_pair, 0)

    def last_tiles_even(sub, i):
        paired_stage((sub, i - 1, sb_scr, mtb_scr, False), (sub, i - 2, sa_scr, mta_scr, False))
        softmax_stage(sub, i - 1, sb_scr, mtb_scr, False)

    def project(sub):
        slot = sub % 2
        q_rows = slice(sub * TQ, (sub + 1) * TQ)
        for hd in range(B_HEADS):
            inv_l = 1.0 / acc_scr[slot, hd, V_HEAD:V_HEAD + 1, :]
            ot_scr[slot, hd * V_HEAD:(hd + 1) * V_HEAD, :] = (acc_scr[slot, hd, :V_HEAD, :] * inv_l).astype(BF16)
        out = lax.dot_general(ot_scr[slot], wo_scr[...], TN_DIMS, preferred_element_type=F32)
        o_ref[0, q_rows, :] = h_ref[0, q_rows, :] + out

    for sub in range(Q_TILES_PER_STEP):
        i = Q_TILES_PER_STEP * k + sub
        if sub > 0:
            project(sub - 1)
        first_tiles(sub, i)
        tile_pairs(sub, jnp.maximum(i - 1, 0) >> 1)
        if sub % 2 == 1:
            softmax_stage(sub, i - 1, sa_scr, mta_scr, False)
        elif sub == 0:
            pl.when(k > 0)(functools.partial(last_tiles_even, sub, i))
        else:
            last_tiles_even(sub, i)
    project(Q_TILES_PER_STEP - 1)


def _attn(h, q, kn, kpe, vt, wo):
    tq2 = Q_TILES_PER_STEP * TQ
    return pl.pallas_call(
        _attn_kernel,
        grid=(BATCH, SEQ // tq2),
        in_specs=[pl.BlockSpec((1, tq2, D_MODEL), lambda b, k: (b, k, 0)),
                  pl.BlockSpec((1, B_HEADS, tq2, QK_PAD), lambda b, k: (b, 0, k, 0)),
                  pl.BlockSpec((1, B_HEADS, SEQ, QK_NOPE), lambda b, k: (b, 0, 0, 0)),
                  pl.BlockSpec((1, SEQ, LANES), lambda b, k: (b, 0, 0)),
                  pl.BlockSpec((1, SEQ // TK, B_HEADS * V_HEAD, TK), lambda b, k: (b, 0, 0, 0)),
                  _resident((B_HEADS * V_HEAD, D_MODEL))],
        out_specs=pl.BlockSpec((1, tq2, D_MODEL), lambda b, k: (b, k, 0)),
        out_shape=jax.ShapeDtypeStruct((BATCH, SEQ, D_MODEL), F32),
        scratch_shapes=[pltpu.VMEM((B_HEADS, TK, TQ), F32), pltpu.VMEM((B_HEADS, TK, TQ), F32),
                        pltpu.VMEM((B_HEADS, 1, TQ), F32), pltpu.VMEM((B_HEADS, 1, TQ), F32),
                        pltpu.VMEM((2, B_HEADS, 1, TQ), F32),
                        pltpu.VMEM((2, B_HEADS, V_HEAD + ONES_ROWS, TQ), F32),
                        pltpu.VMEM((2, B_HEADS * V_HEAD, TQ), BF16),
                        pltpu.VMEM((B_HEADS * V_HEAD, D_MODEL), BF16)],
        compiler_params=_params("arbitrary", "arbitrary"),
        name="mla_attention",
    )(h, q, kn, kpe, vt, wo)


def _row(v):
    return v.reshape(1, -1)


def _spread_rope(w):
    half = QK_ROPE // 2
    zeros = jnp.zeros(w.shape[:-1] + (half,), w.dtype)
    return jnp.concatenate([w[..., :half], zeros, w[..., half:], zeros], axis=-1)


def kernel(x, positions, norm_mix_g, norm_mlp_g, a_w_in, a_ln_v_g, a_ln_v_b, a_w_s, a_b_s, a_w_out,
           b_w_q_a, b_q_norm_g, b_w_q_b, b_w_o, kv_src_norm_g, kv_w_a, kv_a_norm_g, kv_w_b,
           mlp_w1, mlp_w2, final_norm_g):
    w_in = a_w_in[0]
    w_out = a_w_out[0]
    wkva = jnp.concatenate([kv_w_a[:, :KV_LORA], _spread_rope(kv_w_a[:, KV_LORA:])], axis=-1).astype(BF16)
    kvb = kv_w_b.reshape(KV_LORA, B_HEADS, QK_NOPE + V_HEAD)
    wk = kvb[:, :, :QK_NOPE].reshape(KV_LORA, B_HEADS * QK_NOPE).astype(BF16)
    wvt = kvb[:, :, QK_NOPE:].reshape(KV_LORA, B_HEADS * V_HEAD).T.astype(BF16)
    wqa = b_w_q_a[0]
    wqb = b_w_q_b[0].reshape(Q_LORA, B_HEADS, QK_NOPE + QK_ROPE)
    wqb = jnp.concatenate([wqb[:, :, :QK_NOPE], _spread_rope(wqb[:, :, QK_NOPE:])], axis=-1)
    wqb = wqb.reshape(Q_LORA, B_HEADS * QK_PAD).astype(BF16)
    wo = b_w_o[0]

    inv_freq = ROPE_THETA ** (-jnp.arange(0, QK_ROPE, 2, dtype=F32) / QK_ROPE)
    phase = jnp.concatenate([jnp.zeros((QK_ROPE // 2,), F32), jnp.full((QK_ROPE // 2,), HALF_PI, F32)])
    rope_tab = jnp.broadcast_to(jnp.concatenate([inv_freq, inv_freq, phase])[:, None], (2 * QK_ROPE, LANES))

    mix_g = norm_mix_g.reshape(DEPTH, 1, D_MODEL)
    mlp_g = norm_mlp_g.reshape(DEPTH, 1, D_MODEL)
    h = x.reshape(TOKENS, D_MODEL)
    h, w1, w2 = _gmlp(h, mix_g, w_in, _row(a_ln_v_g[0]), _row(a_ln_v_b[0]), a_w_s[0],
                      a_b_s[0].reshape(A_GROUPS, GMLP_BLOCK, 1), w_out, mlp_w1, mlp_w2)
    h, cos_tab, sin_tab = _mlp_rope(h, mlp_g, w1, w2,
                                    positions.reshape(TOKENS // TM_MLP, 1, TM_MLP),
                                    rope_tab, 0)

    h3 = h.reshape(BATCH, SEQ, D_MODEL)
    q, kn, kpe, vt = _proj(h3, cos_tab.reshape(BATCH, SEQ, LANES), sin_tab.reshape(BATCH, SEQ, LANES),
                           _row(kv_src_norm_g), mix_g, wkva, _row(kv_a_norm_g), wk, wvt,
                           wqa, _row(b_q_norm_g[0]), wqb)
    h3 = _attn(h3, q, kn, kpe, vt, wo)

    h = _mlp_final(h3.reshape(TOKENS, D_MODEL), mlp_g, w1, w2, _row(final_norm_g), 1)
    return h.reshape(BATCH, SEQ, D_MODEL)
```

```python
import functools

import jax
import jax.numpy as jnp
import numpy as np
from jax import lax
from jax.experimental import pallas as pl
from jax.experimental.pallas import tpu as pltpu

D_MODEL = 1024
BATCH = 8
SEQ = 2048
TOKENS = BATCH * SEQ
DEPTH = 2
CHUNK = 64
CHUNK_SHIFT = CHUNK.bit_length() - 1
GMLP_BLOCK = 128
GATE_DIM = 2 * D_MODEL
A_GROUPS = 8
A_GROUP_DIM = GATE_DIM // A_GROUPS
B_HEADS = 8
QK_NOPE = 128
QK_ROPE = 64
V_HEAD = 128
Q_LORA = 384
KV_LORA = 256
ROPE_THETA = 10000.0
D_FF = 4 * D_MODEL
EPS = 1e-6

LANES = 128
QK_PAD = 2 * LANES
VMEM_LIMIT = 58 * 1024 * 1024

TM = 512
TM_PROJ = 1024
TM_MLP = 1024
NCHUNK = 512
TQ = 256
TK = 256
Q_TILES_PER_STEP = 4
ONES_ROWS = 16

assert Q_TILES_PER_STEP % 2 == 0 and SEQ % (Q_TILES_PER_STEP * TQ) == 0 and TQ == TK
assert TM_MLP // (D_FF // NCHUNK) == LANES

F32 = jnp.float32
BF16 = jnp.bfloat16
NEG = float(np.finfo(np.float32).min)
SQRT_HALF = float(np.sqrt(0.5))
LOG2_E = float(np.log2(np.e))
HALF_PI = float(np.pi / 2)
NT_DIMS = (((1,), (1,)), ((), ()))
TN_DIMS = (((0,), (0,)), ((), ()))


def _rms(x, g):
    ms = jnp.mean(x * x, axis=-1, keepdims=True)
    return x * lax.rsqrt(ms + EPS) * g


def _resident(shape):
    nd = len(shape)
    return pl.BlockSpec(shape, lambda *_: (0,) * nd, pipeline_mode=pl.Buffered(1))


def _layer_gain(layer):
    return pl.BlockSpec((None, 1, D_MODEL), lambda *_: (layer, 0, 0), pipeline_mode=pl.Buffered(1))


def _params(*sem):
    return pltpu.CompilerParams(dimension_semantics=sem, vmem_limit_bytes=VMEM_LIMIT)


def _gelu_x2(z):
    return z * (1.0 + lax.erf(z * SQRT_HALF))


def _gmlp_kernel(h_ref, g_ref, w_in_ref, lng_ref, lnb_ref, ws_ref, bs_ref, w_out_ref, w1f_ref, w2f_ref,
                 o_ref, w1b_ref, w2b_ref, v_scr, vn_scr, gated_scr, ws_scr):
    w1b_ref[...] = w1f_ref[...].astype(BF16)
    w2b_ref[...] = w2f_ref[...].astype(BF16)

    h = h_ref[...]
    hn = _rms(h, g_ref[...]).astype(BF16)
    n_half = GATE_DIM // NCHUNK
    row_blocks = [slice(r * GMLP_BLOCK, (r + 1) * GMLP_BLOCK) for r in range(TM // GMLP_BLOCK)]

    for c in range(n_half):
        w_cols = slice(GATE_DIM + c * NCHUNK, GATE_DIM + (c + 1) * NCHUNK)
        v_scr[:, c * NCHUNK:(c + 1) * NCHUNK] = _gelu_x2(
            jnp.dot(hn, w_in_ref[:, w_cols].astype(BF16), preferred_element_type=F32))

    ii = lax.broadcasted_iota(jnp.int32, (GMLP_BLOCK, GMLP_BLOCK), 0) >> CHUNK_SHIFT
    jj = lax.broadcasted_iota(jnp.int32, (GMLP_BLOCK, GMLP_BLOCK), 1) >> CHUNK_SHIFT
    for g in range(A_GROUPS):
        ws_scr[g] = jnp.where(jj <= ii, 0.5 * ws_ref[g], 0.0).astype(BF16)

    for rows in row_blocks:
        v = v_scr[rows, :]
        mu = jnp.mean(v, axis=-1, keepdims=True)
        var = jnp.mean(jnp.square(v - mu), axis=-1, keepdims=True)
        vn_scr[rows, :] = ((v - mu) * lax.rsqrt(var + 4.0 * EPS) * lng_ref[...] + lnb_ref[...]).astype(BF16)

    for c in range(n_half):
        u = _gelu_x2(jnp.dot(hn, w_in_ref[:, c * NCHUNK:(c + 1) * NCHUNK].astype(BF16),
                          preferred_element_type=F32))
        for gl in range(NCHUNK // A_GROUP_DIM):
            g = c * (NCHUNK // A_GROUP_DIM) + gl
            cols = slice(g * A_GROUP_DIM, (g + 1) * A_GROUP_DIM)
            for rows in row_blocks:
                sv = jnp.dot(ws_scr[g], vn_scr[rows, cols], preferred_element_type=F32) + 0.5 * bs_ref[g]
                u_blk = u[rows, gl * A_GROUP_DIM:(gl + 1) * A_GROUP_DIM]
                gated_scr[rows, cols] = (u_blk * sv).astype(BF16)

    o_ref[...] = h + jnp.dot(gated_scr[...], w_out_ref[...].astype(BF16), preferred_element_type=F32)


def _gmlp(h, g, w_in, lng, lnb, ws, bs, w_out, w1f, w2f):
    steps = TOKENS // TM
    tok = pl.BlockSpec((TM, D_MODEL), lambda t: (t, 0))
    w1_slab = pl.BlockSpec((DEPTH, D_MODEL // steps, D_FF), lambda t: (0, t, 0))
    w2_slab = pl.BlockSpec((DEPTH, D_FF // steps, D_MODEL), lambda t: (0, t, 0))
    return pl.pallas_call(
        _gmlp_kernel,
        grid=(steps,),
        in_specs=[tok, _layer_gain(0), _resident((D_MODEL, 2 * GATE_DIM)),
                  _resident((1, GATE_DIM)), _resident((1, GATE_DIM)),
                  _resident((A_GROUPS, GMLP_BLOCK, GMLP_BLOCK)),
                  _resident((A_GROUPS, GMLP_BLOCK, 1)), _resident((GATE_DIM, D_MODEL)),
                  w1_slab, w2_slab],
        out_specs=[tok, w1_slab, w2_slab],
        out_shape=[jax.ShapeDtypeStruct((TOKENS, D_MODEL), F32),
                   jax.ShapeDtypeStruct((DEPTH, D_MODEL, D_FF), BF16),
                   jax.ShapeDtypeStruct((DEPTH, D_FF, D_MODEL), BF16)],
        scratch_shapes=[pltpu.VMEM((TM, GATE_DIM), F32), pltpu.VMEM((TM, GATE_DIM), BF16),
                        pltpu.VMEM((TM, GATE_DIM), BF16),
                        pltpu.VMEM((A_GROUPS, GMLP_BLOCK, GMLP_BLOCK), BF16)],
        compiler_params=_params("arbitrary"),
        name="gmlp_mixer",
    )(h, g, w_in, lng, lnb, ws, bs, w_out, w1f, w2f)


def _mlp_residual(h, g_ref, w1_ref, w2_ref, a_scr, anchors=None):
    inv_ms = 1.0 / (jnp.mean(h * h, axis=-1, keepdims=True) + EPS)
    hg = (h * g_ref[...]).astype(BF16)
    n_chunks = D_FF // NCHUNK
    slab = h.shape[0] // n_chunks
    for c in range(n_chunks):
        cols = slice(c * NCHUNK, (c + 1) * NCHUNK)
        a = jnp.maximum(jnp.dot(hg, w1_ref[:, cols], preferred_element_type=F32), 0.0)
        if anchors is None:
            a_scr[:, cols] = (a * a).astype(BF16)
        else:
            for r in range(n_chunks):
                rows = slice(r * slab, (r + 1) * slab)
                blk = a[rows] + anchors[c] if r == c else a[rows]
                a_scr[rows, cols] = (blk * blk).astype(BF16)
    return h + inv_ms * jnp.dot(a_scr[...], w2_ref[...], preferred_element_type=F32)


def _mlp_rope_kernel(h_ref, g_ref, w1_ref, w2_ref, pos_ref, rope_ref, o_ref, cos_ref, sin_ref, a_scr):
    half = QK_ROPE // 2
    n_slabs = D_FF // NCHUNK
    slab = TM_MLP // n_slabs
    pos = pos_ref[0].astype(F32)
    zeros = jnp.zeros((half, slab), F32)
    anchors = []
    for c in range(n_slabs):
        toks = slice(c * slab, (c + 1) * slab)
        sin_cos = jnp.sin(rope_ref[:QK_ROPE, :] * pos[:, toks] + rope_ref[QK_ROPE:, :])
        sin_t, cos_t = sin_cos[:half], sin_cos[half:]
        cos_tab = jnp.concatenate([cos_t, cos_t, cos_t, cos_t], axis=0).T
        sin_tab = jnp.concatenate([-sin_t, zeros, sin_t, zeros], axis=0).T
        cos_ref[toks, :] = cos_tab
        sin_ref[toks, :] = sin_tab
        anchors.append((cos_tab[:, 0:1] + sin_tab[:, 0:1]) * 0.0)
    o_ref[...] = _mlp_residual(h_ref[...], g_ref, w1_ref, w2_ref, a_scr, anchors)


def _mlp_final_kernel(h_ref, g_ref, w1_ref, w2_ref, gf_ref, o_ref, a_scr):
    o_ref[...] = _rms(_mlp_residual(h_ref[...], g_ref, w1_ref, w2_ref, a_scr), gf_ref[...])


def _mlp_specs(layer):
    tok = pl.BlockSpec((TM_MLP, D_MODEL), lambda t: (t, 0))
    w1 = pl.BlockSpec((None, D_MODEL, D_FF), lambda t: (layer, 0, 0), pipeline_mode=pl.Buffered(1))
    w2 = pl.BlockSpec((None, D_FF, D_MODEL), lambda t: (layer, 0, 0), pipeline_mode=pl.Buffered(1))
    return tok, [tok, _layer_gain(layer), w1, w2]


def _mlp_rope(h, g, w1, w2, pos, rope_tab, layer):
    tok, in_specs = _mlp_specs(layer)
    tab = pl.BlockSpec((TM_MLP, LANES), lambda t: (t, 0))
    return pl.pallas_call(
        _mlp_rope_kernel,
        grid=(TOKENS // TM_MLP,),
        in_specs=in_specs + [pl.BlockSpec((1, 1, TM_MLP), lambda t: (t, 0, 0)),
                             _resident((2 * QK_ROPE, LANES))],
        out_specs=[tok, tab, tab],
        out_shape=[jax.ShapeDtypeStruct((TOKENS, D_MODEL), F32),
                   jax.ShapeDtypeStruct((TOKENS, LANES), F32),
                   jax.ShapeDtypeStruct((TOKENS, LANES), F32)],
        scratch_shapes=[pltpu.VMEM((TM_MLP, D_FF), BF16)],
        compiler_params=_params("arbitrary"),
        name="mlp_rope",
    )(h, g, w1, w2, pos, rope_tab)


def _mlp_final(h, g, w1, w2, gf, layer):
    tok, in_specs = _mlp_specs(layer)
    return pl.pallas_call(
        _mlp_final_kernel,
        grid=(TOKENS // TM_MLP,),
        in_specs=in_specs + [_resident((1, D_MODEL))],
        out_specs=tok,
        out_shape=jax.ShapeDtypeStruct((TOKENS, D_MODEL), F32),
        scratch_shapes=[pltpu.VMEM((TM_MLP, D_FF), BF16)],
        compiler_params=_params("arbitrary"),
        name="mlp_final",
    )(h, g, w1, w2, gf)


def _rope(x, cos, sin_signed):
    return x * cos + pltpu.roll(x, LANES // 2, 1) * sin_signed


def _proj_kernel(h_ref, cos_ref, sin_ref, gkv_ref, gq_ref, wkva_ref, gkva_ref, wk_ref, wvt_ref,
                 wqa_ref, gqa_ref, wqb_ref, q_ref, kn_ref, kpe_ref, vt_ref):
    h = h_ref[0]
    cos = cos_ref[0]
    sin_signed = sin_ref[0]

    ms = jnp.mean(h * h, axis=-1, keepdims=True) + EPS
    r = lax.rsqrt(ms)
    eps_latent = EPS * ms

    def latent_rms(x, g):
        return x * lax.rsqrt(jnp.mean(x * x, axis=-1, keepdims=True) + eps_latent) * g

    ckv = jnp.dot((h * gkv_ref[...]).astype(BF16), wkva_ref[...], preferred_element_type=F32)
    c_kv = latent_rms(ckv[:, :KV_LORA], gkva_ref[...]).astype(BF16)
    kpe_ref[0] = (_rope(ckv[:, KV_LORA:], cos, sin_signed) * r).astype(BF16)
    kn = jnp.dot(c_kv, wk_ref[...], preferred_element_type=F32).astype(BF16)
    for hd in range(B_HEADS):
        kn_ref[0, hd] = kn[:, hd * QK_NOPE:(hd + 1) * QK_NOPE]
    vt = lax.dot_general(wvt_ref[...], c_kv, NT_DIMS, preferred_element_type=F32).astype(BF16)
    for kb in range(TM_PROJ // TK):
        vt_ref[0, kb] = vt[:, kb * TK:(kb + 1) * TK]

    scale = (QK_NOPE + QK_ROPE) ** -0.5 * LOG2_E
    cq = jnp.dot((h * gq_ref[...]).astype(BF16), wqa_ref[...].astype(BF16), preferred_element_type=F32)
    cq = latent_rms(cq, gqa_ref[...] * scale).astype(BF16)
    for hd in range(B_HEADS):
        q = jnp.dot(cq, wqb_ref[:, hd * QK_PAD:(hd + 1) * QK_PAD], preferred_element_type=F32)
        q_ref[0, hd, :, :QK_NOPE] = q[:, :QK_NOPE].astype(BF16)
        q_ref[0, hd, :, QK_NOPE:] = _rope(q[:, QK_NOPE:], cos, sin_signed).astype(BF16)


def _proj(h, cos_tab, sin_tab, gkv, gq, wkva, gkva, wk, wvt, wqa, gqa, wqb):
    n_t = SEQ // TM_PROJ
    tab = pl.BlockSpec((1, TM_PROJ, LANES), lambda b, t: (b, t, 0))
    return pl.pallas_call(
        _proj_kernel,
        grid=(BATCH, n_t),
        in_specs=[pl.BlockSpec((1, TM_PROJ, D_MODEL), lambda b, t: (b, t, 0)), tab, tab,
                  _resident((1, D_MODEL)), _layer_gain(1),
                  _resident((D_MODEL, KV_LORA + LANES)), _resident((1, KV_LORA)),
                  _resident((KV_LORA, B_HEADS * QK_NOPE)), _resident((B_HEADS * V_HEAD, KV_LORA)),
                  _resident((D_MODEL, Q_LORA)), _resident((1, Q_LORA)),
                  _resident((Q_LORA, B_HEADS * QK_PAD))],
        out_specs=[pl.BlockSpec((1, B_HEADS, TM_PROJ, QK_PAD), lambda b, t: (b, 0, t, 0)),
                   pl.BlockSpec((1, B_HEADS, TM_PROJ, QK_NOPE), lambda b, t: (b, 0, t, 0)),
                   pl.BlockSpec((1, TM_PROJ, LANES), lambda b, t: (b, t, 0)),
                   pl.BlockSpec((1, TM_PROJ // TK, B_HEADS * V_HEAD, TK), lambda b, t: (b, t, 0, 0))],
        out_shape=[jax.ShapeDtypeStruct((BATCH, B_HEADS, SEQ, QK_PAD), BF16),
                   jax.ShapeDtypeStruct((BATCH, B_HEADS, SEQ, QK_NOPE), BF16),
                   jax.ShapeDtypeStruct((BATCH, SEQ, LANES), BF16),
                   jax.ShapeDtypeStruct((BATCH, SEQ // TK, B_HEADS * V_HEAD, TK), BF16)],
        compiler_params=_params("arbitrary", "arbitrary"),
        name="qkv_proj",
    )(h, cos_tab, sin_tab, gkv, gq, wkva, gkva, wk, wvt, wqa, gqa, wqb)


def _attn_kernel(h_ref, q_ref, kn_ref, kpe_ref, vt_ref, wo_ref, o_ref, sa_scr, sb_scr, mta_scr, mtb_scr,
                 m_scr, acc_scr, ot_scr, wo_scr):
    k = pl.program_id(1)
    kc_id = lax.broadcasted_iota(jnp.int32, (TK, TQ), 0) >> CHUNK_SHIFT
    qc_id = lax.broadcasted_iota(jnp.int32, (TK, TQ), 1) >> CHUNK_SHIFT
    diag_ok = kc_id <= qc_id
    ones_rows = jnp.ones((ONES_ROWS, TK), BF16)

    @pl.when(jnp.logical_and(pl.program_id(0) == 0, k == 0))
    def _():
        wo_scr[...] = wo_ref[...].astype(BF16)

    def score_head(hd, sub, j, s_scr, mt_scr, diagonal):
        rows = pl.ds(pl.multiple_of(j * TK, TK), TK)
        k_cat = jnp.concatenate([kn_ref[0, hd, rows, :], kpe_ref[0, rows, :]], axis=-1)
        q_h = q_ref[0, hd, sub * TQ:(sub + 1) * TQ, :]
        s = lax.dot_general(k_cat, q_h, NT_DIMS, preferred_element_type=F32)
        if diagonal:
            s = jnp.where(diag_ok, s, NEG)
        s_scr[hd] = s
        mt_scr[hd] = jnp.max(s, axis=0, keepdims=True)

    def softmax_head(hd, sub, j, s_scr, mt_scr, first):
        slot = sub % 2
        v_aug = jnp.concatenate([vt_ref[0, j, hd * V_HEAD:(hd + 1) * V_HEAD, :], ones_rows], axis=0)
        if first:
            m_new = mt_scr[hd]
            p = jnp.exp2(s_scr[hd] - m_new)
            acc_scr[slot, hd] = jnp.dot(v_aug, p.astype(BF16), preferred_element_type=F32)
        else:
            m = m_scr[slot, hd]
            m_new = jnp.maximum(m, mt_scr[hd])
            alpha = jnp.exp2(m - m_new)
            p = jnp.exp2(s_scr[hd] - m_new)
            acc_scr[slot, hd] = alpha * acc_scr[slot, hd] + jnp.dot(v_aug, p.astype(BF16),
                                                                    preferred_element_type=F32)
        m_scr[slot, hd] = m_new

    def score_stage(*args):
        for hd in range(B_HEADS):
            score_head(hd, *args)

    def softmax_stage(*args):
        for hd in range(B_HEADS):
            softmax_head(hd, *args)

    def paired_stage(score_args, softmax_args):
        score_head(0, *score_args)
        for hd in range(B_HEADS):
            if hd + 1 < B_HEADS:
                score_head(hd + 1, *score_args)
            softmax_head(hd, *softmax_args)

    def first_tiles(sub, i):
        score_stage(sub, i, sb_scr, mtb_scr, True)
        paired_stage((sub, 0, sa_scr, mta_scr, False), (sub, i, sb_scr, mtb_scr, True))

    def tile_pairs(sub, n_pairs):
        def tile_pair(u, _):
            j = 2 * u
            paired_stage((sub, j + 1, sb_scr, mtb_scr, False), (sub, j, sa_scr, mta_scr, False))
            paired_stage((sub, j + 2, sa_scr, mta_scr, False), (sub, j + 1, sb_scr, mtb_scr, False))
            return 0

        lax.fori_loop(0, n_pairs, tile_pair, 0)

    def last_tiles_even(sub, i):
        paired_stage((sub, i - 1, sb_scr, mtb_scr, False), (sub, i - 2, sa_scr, mta_scr, False))
        softmax_stage(sub, i - 1, sb_scr, mtb_scr, False)

    def project(sub):
        slot = sub % 2
        q_rows = slice(sub * TQ, (sub + 1) * TQ)
        for hd in range(B_HEADS):
            inv_l = 1.0 / acc_scr[slot, hd, V_HEAD:V_HEAD + 1, :]
            ot_scr[slot, hd * V_HEAD:(hd + 1) * V_HEAD, :] = (acc_scr[slot, hd, :V_HEAD, :] * inv_l).astype(BF16)
        out = lax.dot_general(ot_scr[slot], wo_scr[...], TN_DIMS, preferred_element_type=F32)
        o_ref[0, q_rows, :] = h_ref[0, q_rows, :] + out

    for sub in range(Q_TILES_PER_STEP):
        i = Q_TILES_PER_STEP * k + sub
        if sub > 0:
            project(sub - 1)
        first_tiles(sub, i)
        tile_pairs(sub, jnp.maximum(i - 1, 0) >> 1)
        if sub % 2 == 1:
            softmax_stage(sub, i - 1, sa_scr, mta_scr, False)
        elif sub == 0:
            pl.when(k > 0)(functools.partial(last_tiles_even, sub, i))
        else:
            last_tiles_even(sub, i)
    project(Q_TILES_PER_STEP - 1)


def _attn(h, q, kn, kpe, vt, wo):
    tq2 = Q_TILES_PER_STEP * TQ
    return pl.pallas_call(
        _attn_kernel,
        grid=(BATCH, SEQ // tq2),
        in_specs=[pl.BlockSpec((1, tq2, D_MODEL), lambda b, k: (b, k, 0)),
                  pl.BlockSpec((1, B_HEADS, tq2, QK_PAD), lambda b, k: (b, 0, k, 0)),
                  pl.BlockSpec((1, B_HEADS, SEQ, QK_NOPE), lambda b, k: (b, 0, 0, 0)),
                  pl.BlockSpec((1, SEQ, LANES), lambda b, k: (b, 0, 0)),
                  pl.BlockSpec((1, SEQ // TK, B_HEADS * V_HEAD, TK), lambda b, k: (b, 0, 0, 0)),
                  _resident((B_HEADS * V_HEAD, D_MODEL))],
        out_specs=pl.BlockSpec((1, tq2, D_MODEL), lambda b, k: (b, k, 0)),
        out_shape=jax.ShapeDtypeStruct((BATCH, SEQ, D_MODEL), F32),
        scratch_shapes=[pltpu.VMEM((B_HEADS, TK, TQ), F32), pltpu.VMEM((B_HEADS, TK, TQ), F32),
                        pltpu.VMEM((B_HEADS, 1, TQ), F32), pltpu.VMEM((B_HEADS, 1, TQ), F32),
                        pltpu.VMEM((2, B_HEADS, 1, TQ), F32),
                        pltpu.VMEM((2, B_HEADS, V_HEAD + ONES_ROWS, TQ), F32),
                        pltpu.VMEM((2, B_HEADS * V_HEAD, TQ), BF16),
                        pltpu.VMEM((B_HEADS * V_HEAD, D_MODEL), BF16)],
        compiler_params=_params("arbitrary", "arbitrary"),
        name="mla_attention",
    )(h, q, kn, kpe, vt, wo)


def _row(v):
    return v.reshape(1, -1)


def _spread_rope(w):
    half = QK_ROPE // 2
    zeros = jnp.zeros(w.shape[:-1] + (half,), w.dtype)
    return jnp.concatenate([w[..., :half], zeros, w[..., half:], zeros], axis=-1)


def kernel(x, positions, norm_mix_g, norm_mlp_g, a_w_in, a_ln_v_g, a_ln_v_b, a_w_s, a_b_s, a_w_out,
           b_w_q_a, b_q_norm_g, b_w_q_b, b_w_o, kv_src_norm_g, kv_w_a, kv_a_norm_g, kv_w_b,
           mlp_w1, mlp_w2, final_norm_g):
    w_in = a_w_in[0]
    w_out = a_w_out[0]
    wkva = jnp.concatenate([kv_w_a[:, :KV_LORA], _spread_rope(kv_w_a[:, KV_LORA:])], axis=-1).astype(BF16)
    kvb = kv_w_b.reshape(KV_LORA, B_HEADS, QK_NOPE + V_HEAD)
    wk = kvb[:, :, :QK_NOPE].reshape(KV_LORA, B_HEADS * QK_NOPE).astype(BF16)
    wvt = kvb[:, :, QK_NOPE:].reshape(KV_LORA, B_HEADS * V_HEAD).T.astype(BF16)
    wqa = b_w_q_a[0]
    wqb = b_w_q_b[0].reshape(Q_LORA, B_HEADS, QK_NOPE + QK_ROPE)
    wqb = jnp.concatenate([wqb[:, :, :QK_NOPE], _spread_rope(wqb[:, :, QK_NOPE:])], axis=-1)
    wqb = wqb.reshape(Q_LORA, B_HEADS * QK_PAD).astype(BF16)
    wo = b_w_o[0]

    inv_freq = ROPE_THETA ** (-jnp.arange(0, QK_ROPE, 2, dtype=F32) / QK_ROPE)
    phase = jnp.concatenate([jnp.zeros((QK_ROPE // 2,), F32), jnp.full((QK_ROPE // 2,), HALF_PI, F32)])
    rope_tab = jnp.broadcast_to(jnp.concatenate([inv_freq, inv_freq, phase])[:, None], (2 * QK_ROPE, LANES))

    mix_g = norm_mix_g.reshape(DEPTH, 1, D_MODEL)
    mlp_g = norm_mlp_g.reshape(DEPTH, 1, D_MODEL)
    h = x.reshape(TOKENS, D_MODEL)
    h, w1, w2 = _gmlp(h, mix_g, w_in, _row(a_ln_v_g[0]), _row(a_ln_v_b[0]), a_w_s[0],
                      a_b_s[0].reshape(A_GROUPS, GMLP_BLOCK, 1), w_out, mlp_w1, mlp_w2)
    h, cos_tab, sin_tab = _mlp_rope(h, mlp_g, w1, w2,
                                    positions.reshape(TOKENS // TM_MLP, 1, TM_MLP),
                                    rope_tab, 0)

    h3 = h.reshape(BATCH, SEQ, D_MODEL)
    q, kn, kpe, vt = _proj(h3, cos_tab.reshape(BATCH, SEQ, LANES), sin_tab.reshape(BATCH, SEQ, LANES),
                           _row(kv_src_norm_g), mix_g, wkva, _row(kv_a_norm_g), wk, wvt,
                           wqa, _row(b_q_norm_g[0]), wqb)
    h3 = _attn(h3, q, kn, kpe, vt, wo)

    h = _mlp_final(h3.reshape(TOKENS, D_MODEL), mlp_g, w1, w2, _row(final_norm_g), 1)
    return h.reshape(BATCH, SEQ, D_MODEL)
```

```python
import functools

import jax
import jax.numpy as jnp
import numpy as np
from jax import lax
from jax.experimental import pallas as pl
from jax.experimental.pallas import tpu as pltpu

D_MODEL = 1024
BATCH = 8
SEQ = 2048
TOKENS = BATCH * SEQ
DEPTH = 2
CHUNK = 64
CHUNK_SHIFT = CHUNK.bit_length() - 1
GMLP_BLOCK = 128
GATE_DIM = 2 * D_MODEL
A_GROUPS = 8
A_GROUP_DIM = GATE_DIM // A_GROUPS
B_HEADS = 8
QK_NOPE = 128
QK_ROPE = 64
V_HEAD = 128
Q_LORA = 384
KV_LORA = 256
ROPE_THETA = 10000.0
D_FF = 4 * D_MODEL
EPS = 1e-6

LANES = 128
QK_PAD = 2 * LANES
VMEM_LIMIT = 58 * 1024 * 1024

TM = 512
TM_PROJ = 1024
TM_MLP = 1024
NCHUNK = 512
TQ = 256
TK = 256
Q_TILES_PER_STEP = 4
ONES_ROWS = 16

assert Q_TILES_PER_STEP % 2 == 0 and SEQ % (Q_TILES_PER_STEP * TQ) == 0 and TQ == TK
assert TM_MLP // (D_FF // NCHUNK) == LANES

F32 = jnp.float32
BF16 = jnp.bfloat16
NEG = float(np.finfo(np.float32).min)
SQRT_HALF = float(np.sqrt(0.5))
LOG2_E = float(np.log2(np.e))
HALF_PI = float(np.pi / 2)
NT_DIMS = (((1,), (1,)), ((), ()))
TN_DIMS = (((0,), (0,)), ((), ()))


def _rms(x, g):
    ms = jnp.mean(x * x, axis=-1, keepdims=True)
    return x * lax.rsqrt(ms + EPS) * g


def _resident(shape):
    nd = len(shape)
    return pl.BlockSpec(shape, lambda *_: (0,) * nd, pipeline_mode=pl.Buffered(1))


def _layer_gain(layer):
    return pl.BlockSpec((None, 1, D_MODEL), lambda *_: (layer, 0, 0), pipeline_mode=pl.Buffered(1))


def _params(*sem):
    return pltpu.CompilerParams(dimension_semantics=sem, vmem_limit_bytes=VMEM_LIMIT)


def _gelu_x2(z):
    return z * (1.0 + lax.erf(z * SQRT_HALF))


def _gmlp_kernel(h_ref, g_ref, w_in_ref, lng_ref, lnb_ref, ws_ref, bs_ref, w_out_ref, w1f_ref, w2f_ref,
                 o_ref, w1b_ref, w2b_ref, v_scr, vn_scr, gated_scr, ws_scr):
    w1b_ref[...] = w1f_ref[...].astype(BF16)
    w2b_ref[...] = w2f_ref[...].astype(BF16)

    h = h_ref[...]
    hn = _rms(h, g_ref[...]).astype(BF16)
    n_half = GATE_DIM // NCHUNK
    row_blocks = [slice(r * GMLP_BLOCK, (r + 1) * GMLP_BLOCK) for r in range(TM // GMLP_BLOCK)]

    for c in range(n_half):
        w_cols = slice(GATE_DIM + c * NCHUNK, GATE_DIM + (c + 1) * NCHUNK)
        v_scr[:, c * NCHUNK:(c + 1) * NCHUNK] = _gelu_x2(
            jnp.dot(hn, w_in_ref[:, w_cols].astype(BF16), preferred_element_type=F32))

    ii = lax.broadcasted_iota(jnp.int32, (GMLP_BLOCK, GMLP_BLOCK), 0) >> CHUNK_SHIFT
    jj = lax.broadcasted_iota(jnp.int32, (GMLP_BLOCK, GMLP_BLOCK), 1) >> CHUNK_SHIFT
    for g in range(A_GROUPS):
        ws_scr[g] = jnp.where(jj <= ii, 0.5 * ws_ref[g], 0.0).astype(BF16)

    def u_chunk(c):
        return _gelu_x2(jnp.dot(hn, w_in_ref[:, c * NCHUNK:(c + 1) * NCHUNK].astype(BF16),
                                preferred_element_type=F32))

    u_ahead = [u_chunk(0), u_chunk(1)]
    for rows in row_blocks:
        v = v_scr[rows, :]
        mu = jnp.mean(v, axis=-1, keepdims=True)
        var = jnp.mean(jnp.square(v - mu), axis=-1, keepdims=True)
        vn_scr[rows, :] = ((v - mu) * lax.rsqrt(var + 4.0 * EPS) * lng_ref[...] + lnb_ref[...]).astype(BF16)

    for c in range(n_half):
        u = u_ahead.pop(0)
        if c + 2 < n_half:
            u_ahead.append(u_chunk(c + 2))
        for gl in range(NCHUNK // A_GROUP_DIM):
            g = c * (NCHUNK // A_GROUP_DIM) + gl
            cols = slice(g * A_GROUP_DIM, (g + 1) * A_GROUP_DIM)
            for rows in row_blocks:
                sv = jnp.dot(ws_scr[g], vn_scr[rows, cols], preferred_element_type=F32) + 0.5 * bs_ref[g]
                u_blk = u[rows, gl * A_GROUP_DIM:(gl + 1) * A_GROUP_DIM]
                gated_scr[rows, cols] = (u_blk * sv).astype(BF16)

    o_ref[...] = h + jnp.dot(gated_scr[...], w_out_ref[...].astype(BF16), preferred_element_type=F32)


def _gmlp(h, g, w_in, lng, lnb, ws, bs, w_out, w1f, w2f):
    steps = TOKENS // TM
    tok = pl.BlockSpec((TM, D_MODEL), lambda t: (t, 0))
    w1_slab = pl.BlockSpec((DEPTH, D_MODEL // steps, D_FF), lambda t: (0, t, 0))
    w2_slab = pl.BlockSpec((DEPTH, D_FF // steps, D_MODEL), lambda t: (0, t, 0))
    return pl.pallas_call(
        _gmlp_kernel,
        grid=(steps,),
        in_specs=[tok, _layer_gain(0), _resident((D_MODEL, 2 * GATE_DIM)),
                  _resident((1, GATE_DIM)), _resident((1, GATE_DIM)),
                  _resident((A_GROUPS, GMLP_BLOCK, GMLP_BLOCK)),
                  _resident((A_GROUPS, GMLP_BLOCK, 1)), _resident((GATE_DIM, D_MODEL)),
                  w1_slab, w2_slab],
        out_specs=[tok, w1_slab, w2_slab],
        out_shape=[jax.ShapeDtypeStruct((TOKENS, D_MODEL), F32),
                   jax.ShapeDtypeStruct((DEPTH, D_MODEL, D_FF), BF16),
                   jax.ShapeDtypeStruct((DEPTH, D_FF, D_MODEL), BF16)],
        scratch_shapes=[pltpu.VMEM((TM, GATE_DIM), F32), pltpu.VMEM((TM, GATE_DIM), BF16),
                        pltpu.VMEM((TM, GATE_DIM), BF16),
                        pltpu.VMEM((A_GROUPS, GMLP_BLOCK, GMLP_BLOCK), BF16)],
        compiler_params=_params("arbitrary"),
        name="gmlp_mixer",
    )(h, g, w_in, lng, lnb, ws, bs, w_out, w1f, w2f)


def _mlp_residual(h, g_ref, w1_ref, w2_ref, a_scr, anchors=None):
    inv_ms = 1.0 / (jnp.mean(h * h, axis=-1, keepdims=True) + EPS)
    hg = (h * g_ref[...]).astype(BF16)
    n_chunks = D_FF // NCHUNK
    slab = h.shape[0] // n_chunks
    for c in range(n_chunks):
        cols = slice(c * NCHUNK, (c + 1) * NCHUNK)
        a = jnp.maximum(jnp.dot(hg, w1_ref[:, cols], preferred_element_type=F32), 0.0)
        if anchors is None:
            a_scr[:, cols] = (a * a).astype(BF16)
        else:
            for r in range(n_chunks):
                rows = slice(r * slab, (r + 1) * slab)
                blk = a[rows] + anchors[c] if r == c else a[rows]
                a_scr[rows, cols] = (blk * blk).astype(BF16)
    return h + inv_ms * jnp.dot(a_scr[...], w2_ref[...], preferred_element_type=F32)


def _mlp_rope_kernel(h_ref, g_ref, w1_ref, w2_ref, pos_ref, rope_ref, o_ref, cos_ref, sin_ref, a_scr):
    half = QK_ROPE // 2
    n_slabs = D_FF // NCHUNK
    slab = TM_MLP // n_slabs
    pos = pos_ref[0].astype(F32)
    zeros = jnp.zeros((half, slab), F32)
    anchors = []
    for c in range(n_slabs):
        toks = slice(c * slab, (c + 1) * slab)
        sin_cos = jnp.sin(rope_ref[:QK_ROPE, :] * pos[:, toks] + rope_ref[QK_ROPE:, :])
        sin_t, cos_t = sin_cos[:half], sin_cos[half:]
        cos_tab = jnp.concatenate([cos_t, cos_t, cos_t, cos_t], axis=0).T
        sin_tab = jnp.concatenate([-sin_t, zeros, sin_t, zeros], axis=0).T
        cos_ref[toks, :] = cos_tab
        sin_ref[toks, :] = sin_tab
        anchors.append((cos_tab[:, 0:1] + sin_tab[:, 0:1]) * 0.0)
    o_ref[...] = _mlp_residual(h_ref[...], g_ref, w1_ref, w2_ref, a_scr, anchors)


def _mlp_final_kernel(h_ref, g_ref, w1_ref, w2_ref, gf_ref, o_ref, a_scr):
    o_ref[...] = _rms(_mlp_residual(h_ref[...], g_ref, w1_ref, w2_ref, a_scr), gf_ref[...])


def _mlp_specs(layer):
    tok = pl.BlockSpec((TM_MLP, D_MODEL), lambda t: (t, 0))
    w1 = pl.BlockSpec((None, D_MODEL, D_FF), lambda t: (layer, 0, 0), pipeline_mode=pl.Buffered(1))
    w2 = pl.BlockSpec((None, D_FF, D_MODEL), lambda t: (layer, 0, 0), pipeline_mode=pl.Buffered(1))
    return tok, [tok, _layer_gain(layer), w1, w2]


def _mlp_rope(h, g, w1, w2, pos, rope_tab, layer):
    tok, in_specs = _mlp_specs(layer)
    tab = pl.BlockSpec((TM_MLP, LANES), lambda t: (t, 0))
    return pl.pallas_call(
        _mlp_rope_kernel,
        grid=(TOKENS // TM_MLP,),
        in_specs=in_specs + [pl.BlockSpec((1, 1, TM_MLP), lambda t: (t, 0, 0)),
                             _resident((2 * QK_ROPE, LANES))],
        out_specs=[tok, tab, tab],
        out_shape=[jax.ShapeDtypeStruct((TOKENS, D_MODEL), F32),
                   jax.ShapeDtypeStruct((TOKENS, LANES), F32),
                   jax.ShapeDtypeStruct((TOKENS, LANES), F32)],
        scratch_shapes=[pltpu.VMEM((TM_MLP, D_FF), BF16)],
        compiler_params=_params("arbitrary"),
        name="mlp_rope",
    )(h, g, w1, w2, pos, rope_tab)


def _mlp_final(h, g, w1, w2, gf, layer):
    tok, in_specs = _mlp_specs(layer)
    return pl.pallas_call(
        _mlp_final_kernel,
        grid=(TOKENS // TM_MLP,),
        in_specs=in_specs + [_resident((1, D_MODEL))],
        out_specs=tok,
        out_shape=jax.ShapeDtypeStruct((TOKENS, D_MODEL), F32),
        scratch_shapes=[pltpu.VMEM((TM_MLP, D_FF), BF16)],
        compiler_params=_params("arbitrary"),
        name="mlp_final",
    )(h, g, w1, w2, gf)


def _rope(x, cos, sin_signed):
    return x * cos + pltpu.roll(x, LANES // 2, 1) * sin_signed


def _proj_kernel(h_ref, cos_ref, sin_ref, gkv_ref, gq_ref, wkva_ref, gkva_ref, wk_ref, wvt_ref,
                 wqa_ref, gqa_ref, wqb_ref, q_ref, kn_ref, kpe_ref, vt_ref):
    h = h_ref[0]
    cos = cos_ref[0]
    sin_signed = sin_ref[0]

    ms = jnp.mean(h * h, axis=-1, keepdims=True) + EPS
    r = lax.rsqrt(ms)
    eps_latent = EPS * ms

    def latent_rms(x, g):
        return x * lax.rsqrt(jnp.mean(x * x, axis=-1, keepdims=True) + eps_latent) * g

    ckv = jnp.dot((h * gkv_ref[...]).astype(BF16), wkva_ref[...], preferred_element_type=F32)
    c_kv = latent_rms(ckv[:, :KV_LORA], gkva_ref[...]).astype(BF16)
    kpe_ref[0] = (_rope(ckv[:, KV_LORA:], cos, sin_signed) * r).astype(BF16)
    kn = jnp.dot(c_kv, wk_ref[...], preferred_element_type=F32).astype(BF16)
    for hd in range(B_HEADS):
        kn_ref[0, hd] = kn[:, hd * QK_NOPE:(hd + 1) * QK_NOPE]
    vt = lax.dot_general(wvt_ref[...], c_kv, NT_DIMS, preferred_element_type=F32).astype(BF16)
    for kb in range(TM_PROJ // TK):
        vt_ref[0, kb] = vt[:, kb * TK:(kb + 1) * TK]

    scale = (QK_NOPE + QK_ROPE) ** -0.5 * LOG2_E
    cq = jnp.dot((h * gq_ref[...]).astype(BF16), wqa_ref[...].astype(BF16), preferred_element_type=F32)
    cq = latent_rms(cq, gqa_ref[...] * scale).astype(BF16)
    for hd in range(B_HEADS):
        q = jnp.dot(cq, wqb_ref[:, hd * QK_PAD:(hd + 1) * QK_PAD], preferred_element_type=F32)
        q_ref[0, hd, :, :QK_NOPE] = q[:, :QK_NOPE].astype(BF16)
        q_ref[0, hd, :, QK_NOPE:] = _rope(q[:, QK_NOPE:], cos, sin_signed).astype(BF16)


def _proj(h, cos_tab, sin_tab, gkv, gq, wkva, gkva, wk, wvt, wqa, gqa, wqb):
    n_t = SEQ // TM_PROJ
    tab = pl.BlockSpec((1, TM_PROJ, LANES), lambda b, t: (b, t, 0))
    return pl.pallas_call(
        _proj_kernel,
        grid=(BATCH, n_t),
        in_specs=[pl.BlockSpec((1, TM_PROJ, D_MODEL), lambda b, t: (b, t, 0)), tab, tab,
                  _resident((1, D_MODEL)), _layer_gain(1),
                  _resident((D_MODEL, KV_LORA + LANES)), _resident((1, KV_LORA)),
                  _resident((KV_LORA, B_HEADS * QK_NOPE)), _resident((B_HEADS * V_HEAD, KV_LORA)),
                  _resident((D_MODEL, Q_LORA)), _resident((1, Q_LORA)),
                  _resident((Q_LORA, B_HEADS * QK_PAD))],
        out_specs=[pl.BlockSpec((1, B_HEADS, TM_PROJ, QK_PAD), lambda b, t: (b, 0, t, 0)),
                   pl.BlockSpec((1, B_HEADS, TM_PROJ, QK_NOPE), lambda b, t: (b, 0, t, 0)),
                   pl.BlockSpec((1, TM_PROJ, LANES), lambda b, t: (b, t, 0)),
                   pl.BlockSpec((1, TM_PROJ // TK, B_HEADS * V_HEAD, TK), lambda b, t: (b, t, 0, 0))],
        out_shape=[jax.ShapeDtypeStruct((BATCH, B_HEADS, SEQ, QK_PAD), BF16),
                   jax.ShapeDtypeStruct((BATCH, B_HEADS, SEQ, QK_NOPE), BF16),
                   jax.ShapeDtypeStruct((BATCH, SEQ, LANES), BF16),
                   jax.ShapeDtypeStruct((BATCH, SEQ // TK, B_HEADS * V_HEAD, TK), BF16)],
        compiler_params=_params("arbitrary", "arbitrary"),
        name="qkv_proj",
    )(h, cos_tab, sin_tab, gkv, gq, wkva, gkva, wk, wvt, wqa, gqa, wqb)


def _attn_kernel(h_ref, q_ref, kn_ref, kpe_ref, vt_ref, wo_ref, o_ref, sa_scr, sb_scr, mta_scr, mtb_scr,
                 m_scr, acc_scr, ot_scr, wo_scr):
    k = pl.program_id(1)
    kc_id = lax.broadcasted_iota(jnp.int32, (TK, TQ), 0) >> CHUNK_SHIFT
    qc_id = lax.broadcasted_iota(jnp.int32, (TK, TQ), 1) >> CHUNK_SHIFT
    diag_ok = kc_id <= qc_id
    ones_rows = jnp.ones((ONES_ROWS, TK), BF16)

    @pl.when(jnp.logical_and(pl.program_id(0) == 0, k == 0))
    def _():
        wo_scr[...] = wo_ref[...].astype(BF16)

    def score_head(hd, sub, j, s_scr, mt_scr, diagonal):
        rows = pl.ds(pl.multiple_of(j * TK, TK), TK)
        k_cat = jnp.concatenate([kn_ref[0, hd, rows, :], kpe_ref[0, rows, :]], axis=-1)
        q_h = q_ref[0, hd, sub * TQ:(sub + 1) * TQ, :]
        s = lax.dot_general(k_cat, q_h, NT_DIMS, preferred_element_type=F32)
        if diagonal:
            s = jnp.where(diag_ok, s, NEG)
        s_scr[hd] = s
        mt_scr[hd] = jnp.max(s, axis=0, keepdims=True)

    def softmax_head(hd, sub, j, s_scr, mt_scr, first):
        slot = sub % 2
        v_aug = jnp.concatenate([vt_ref[0, j, hd * V_HEAD:(hd + 1) * V_HEAD, :], ones_rows], axis=0)
        if first:
            m_new = mt_scr[hd]
            p = jnp.exp2(s_scr[hd] - m_new)
            acc_scr[slot, hd] = jnp.dot(v_aug, p.astype(BF16), preferred_element_type=F32)
        else:
            m = m_scr[slot, hd]
            m_new = jnp.maximum(m, mt_scr[hd])
            alpha = jnp.exp2(m - m_new)
            p = jnp.exp2(s_scr[hd] - m_new)
            acc_scr[slot, hd] = alpha * acc_scr[slot, hd] + jnp.dot(v_aug, p.astype(BF16),
                                                                    preferred_element_type=F32)
        m_scr[slot, hd] = m_new

    def score_stage(*args):
        for hd in range(B_HEADS):
            score_head(hd, *args)

    def softmax_stage(*args):
        for hd in range(B_HEADS):
            softmax_head(hd, *args)

    def paired_stage(score_args, softmax_args):
        score_head(0, *score_args)
        for hd in range(B_HEADS):
            if hd + 1 < B_HEADS:
                score_head(hd + 1, *score_args)
            softmax_head(hd, *softmax_args)

    def first_tiles(sub, i):
        score_stage(sub, i, sb_scr, mtb_scr, True)
        paired_stage((sub, 0, sa_scr, mta_scr, False), (sub, i, sb_scr, mtb_scr, True))

    def tile_pairs(sub, n_pairs):
        def tile_pair(u, _):
            j = 2 * u
            paired_stage((sub, j + 1, sb_scr, mtb_scr, False), (sub, j, sa_scr, mta_scr, False))
            paired_stage((sub, j + 2, sa_scr, mta_scr, False), (sub, j + 1, sb_scr, mtb_scr, False))
            return 0

        lax.fori_loop(0, n_pairs, tile_pair, 0)

    def last_tiles_even(sub, i):
        paired_stage((sub, i - 1, sb_scr, mtb_scr, False), (sub, i - 2, sa_scr, mta_scr, False))
        softmax_stage(sub, i - 1, sb_scr, mtb_scr, False)

    def project(sub):
        slot = sub % 2
        q_rows = slice(sub * TQ, (sub + 1) * TQ)
        for hd in range(B_HEADS):
            inv_l = 1.0 / acc_scr[slot, hd, V_HEAD:V_HEAD + 1, :]
            ot_scr[slot, hd * V_HEAD:(hd + 1) * V_HEAD, :] = (acc_scr[slot, hd, :V_HEAD, :] * inv_l).astype(BF16)
        out = lax.dot_general(ot_scr[slot], wo_scr[...], TN_DIMS, preferred_element_type=F32)
        o_ref[0, q_rows, :] = h_ref[0, q_rows, :] + out

    for sub in range(Q_TILES_PER_STEP):
        i = Q_TILES_PER_STEP * k + sub
        if sub > 0:
            project(sub - 1)
        first_tiles(sub, i)
        tile_pairs(sub, jnp.maximum(i - 1, 0) >> 1)
        if sub % 2 == 1:
            softmax_stage(sub, i - 1, sa_scr, mta_scr, False)
        elif sub == 0:
            pl.when(k > 0)(functools.partial(last_tiles_even, sub, i))
        else:
            last_tiles_even(sub, i)
    project(Q_TILES_PER_STEP - 1)


def _attn(h, q, kn, kpe, vt, wo):
    tq2 = Q_TILES_PER_STEP * TQ
    return pl.pallas_call(
        _attn_kernel,
        grid=(BATCH, SEQ // tq2),
        in_specs=[pl.BlockSpec((1, tq2, D_MODEL), lambda b, k: (b, k, 0)),
                  pl.BlockSpec((1, B_HEADS, tq2, QK_PAD), lambda b, k: (b, 0, k, 0)),
                  pl.BlockSpec((1, B_HEADS, SEQ, QK_NOPE), lambda b, k: (b, 0, 0, 0)),
                  pl.BlockSpec((1, SEQ, LANES), lambda b, k: (b, 0, 0)),
                  pl.BlockSpec((1, SEQ // TK, B_HEADS * V_HEAD, TK), lambda b, k: (b, 0, 0, 0)),
                  _resident((B_HEADS * V_HEAD, D_MODEL))],
        out_specs=pl.BlockSpec((1, tq2, D_MODEL), lambda b, k: (b, k, 0)),
        out_shape=jax.ShapeDtypeStruct((BATCH, SEQ, D_MODEL), F32),
        scratch_shapes=[pltpu.VMEM((B_HEADS, TK, TQ), F32), pltpu.VMEM((B_HEADS, TK, TQ), F32),
                        pltpu.VMEM((B_HEADS, 1, TQ), F32), pltpu.VMEM((B_HEADS, 1, TQ), F32),
                        pltpu.VMEM((2, B_HEADS, 1, TQ), F32),
                        pltpu.VMEM((2, B_HEADS, V_HEAD + ONES_ROWS, TQ), F32),
                        pltpu.VMEM((2, B_HEADS * V_HEAD, TQ), BF16),
                        pltpu.VMEM((B_HEADS * V_HEAD, D_MODEL), BF16)],
        compiler_params=_params("arbitrary", "arbitrary"),
        name="mla_attention",
    )(h, q, kn, kpe, vt, wo)


def _row(v):
    return v.reshape(1, -1)


def _spread_rope(w):
    half = QK_ROPE // 2
    zeros = jnp.zeros(w.shape[:-1] + (half,), w.dtype)
    return jnp.concatenate([w[..., :half], zeros, w[..., half:], zeros], axis=-1)


def kernel(x, positions, norm_mix_g, norm_mlp_g, a_w_in, a_ln_v_g, a_ln_v_b, a_w_s, a_b_s, a_w_out,
           b_w_q_a, b_q_norm_g, b_w_q_b, b_w_o, kv_src_norm_g, kv_w_a, kv_a_norm_g, kv_w_b,
           mlp_w1, mlp_w2, final_norm_g):
    w_in = a_w_in[0]
    w_out = a_w_out[0]
    wkva = jnp.concatenate([kv_w_a[:, :KV_LORA], _spread_rope(kv_w_a[:, KV_LORA:])], axis=-1).astype(BF16)
    kvb = kv_w_b.reshape(KV_LORA, B_HEADS, QK_NOPE + V_HEAD)
    wk = kvb[:, :, :QK_NOPE].reshape(KV_LORA, B_HEADS * QK_NOPE).astype(BF16)
    wvt = kvb[:, :, QK_NOPE:].reshape(KV_LORA, B_HEADS * V_HEAD).T.astype(BF16)
    wqa = b_w_q_a[0]
    wqb = b_w_q_b[0].reshape(Q_LORA, B_HEADS, QK_NOPE + QK_ROPE)
    wqb = jnp.concatenate([wqb[:, :, :QK_NOPE], _spread_rope(wqb[:, :, QK_NOPE:])], axis=-1)
    wqb = wqb.reshape(Q_LORA, B_HEADS * QK_PAD).astype(BF16)
    wo = b_w_o[0]

    inv_freq = ROPE_THETA ** (-jnp.arange(0, QK_ROPE, 2, dtype=F32) / QK_ROPE)
    phase = jnp.concatenate([jnp.zeros((QK_ROPE // 2,), F32), jnp.full((QK_ROPE // 2,), HALF_PI, F32)])
    rope_tab = jnp.broadcast_to(jnp.concatenate([inv_freq, inv_freq, phase])[:, None], (2 * QK_ROPE, LANES))

    mix_g = norm_mix_g.reshape(DEPTH, 1, D_MODEL)
    mlp_g = norm_mlp_g.reshape(DEPTH, 1, D_MODEL)
    h = x.reshape(TOKENS, D_MODEL)
    h, w1, w2 = _gmlp(h, mix_g, w_in, _row(a_ln_v_g[0]), _row(a_ln_v_b[0]), a_w_s[0],
                      a_b_s[0].reshape(A_GROUPS, GMLP_BLOCK, 1), w_out, mlp_w1, mlp_w2)
    h, cos_tab, sin_tab = _mlp_rope(h, mlp_g, w1, w2,
                                    positions.reshape(TOKENS // TM_MLP, 1, TM_MLP),
                                    rope_tab, 0)

    h3 = h.reshape(BATCH, SEQ, D_MODEL)
    q, kn, kpe, vt = _proj(h3, cos_tab.reshape(BATCH, SEQ, LANES), sin_tab.reshape(BATCH, SEQ, LANES),
                           _row(kv_src_norm_g), mix_g, wkva, _row(kv_a_norm_g), wk, wvt,
                           wqa, _row(b_q_norm_g[0]), wqb)
    h3 = _attn(h3, q, kn, kpe, vt, wo)

    h = _mlp_final(h3.reshape(TOKENS, D_MODEL), mlp_g, w1, w2, _row(final_norm_g), 1)
    return h.reshape(BATCH, SEQ, D_MODEL)
```

```python
import functools

import jax
import jax.numpy as jnp
import numpy as np
from jax import lax
from jax.experimental import pallas as pl
from jax.experimental.pallas import tpu as pltpu

D_MODEL = 1024
BATCH = 8
SEQ = 2048
TOKENS = BATCH * SEQ
DEPTH = 2
CHUNK = 64
CHUNK_SHIFT = CHUNK.bit_length() - 1
GMLP_BLOCK = 128
GATE_DIM = 2 * D_MODEL
A_GROUPS = 8
A_GROUP_DIM = GATE_DIM // A_GROUPS
B_HEADS = 8
QK_NOPE = 128
QK_ROPE = 64
V_HEAD = 128
Q_LORA = 384
KV_LORA = 256
ROPE_THETA = 10000.0
D_FF = 4 * D_MODEL
EPS = 1e-6

LANES = 128
QK_PAD = 2 * LANES
VMEM_LIMIT = 58 * 1024 * 1024

TM = 512
TM_PROJ = 1024
TM_MLP = 1024
NCHUNK = 512
TQ = 256
TK = 256
Q_TILES_PER_STEP = 4
ONES_ROWS = 16

assert Q_TILES_PER_STEP % 2 == 0 and SEQ % (Q_TILES_PER_STEP * TQ) == 0 and TQ == TK
assert TM_MLP // (D_FF // NCHUNK) == LANES

F32 = jnp.float32
BF16 = jnp.bfloat16
NEG = float(np.finfo(np.float32).min)
SQRT_HALF = float(np.sqrt(0.5))
LOG2_E = float(np.log2(np.e))
HALF_PI = float(np.pi / 2)
NT_DIMS = (((1,), (1,)), ((), ()))
TN_DIMS = (((0,), (0,)), ((), ()))


def _rms(x, g):
    ms = jnp.mean(x * x, axis=-1, keepdims=True)
    return x * lax.rsqrt(ms + EPS) * g


def _resident(shape):
    nd = len(shape)
    return pl.BlockSpec(shape, lambda *_: (0,) * nd, pipeline_mode=pl.Buffered(1))


def _layer_gain(layer):
    return pl.BlockSpec((None, 1, D_MODEL), lambda *_: (layer, 0, 0), pipeline_mode=pl.Buffered(1))


def _params(*sem):
    return pltpu.CompilerParams(dimension_semantics=sem, vmem_limit_bytes=VMEM_LIMIT)


def _gelu_x2(z):
    return z * (1.0 + lax.erf(z * SQRT_HALF))


def _gmlp_kernel(h_ref, g_ref, w_in_ref, lng_ref, lnb_ref, ws_ref, bs_ref, w_out_ref, w1f_ref, w2f_ref,
                 o_ref, w1b_ref, w2b_ref, v_scr, vn_scr, gated_scr, ws_scr):
    w1b_ref[...] = w1f_ref[...].astype(BF16)
    w2b_ref[...] = w2f_ref[...].astype(BF16)

    h = h_ref[...]
    hn = _rms(h, g_ref[...]).astype(BF16)
    n_half = GATE_DIM // NCHUNK
    row_blocks = [slice(r * GMLP_BLOCK, (r + 1) * GMLP_BLOCK) for r in range(TM // GMLP_BLOCK)]

    def u_chunk(c):
        return _gelu_x2(jnp.dot(hn, w_in_ref[:, c * NCHUNK:(c + 1) * NCHUNK].astype(BF16),
                                preferred_element_type=F32))

    u_ahead = []
    for c in range(n_half):
        if c == n_half - 2:
            u_ahead.append(u_chunk(0))
        w_cols = slice(GATE_DIM + c * NCHUNK, GATE_DIM + (c + 1) * NCHUNK)
        v_scr[:, c * NCHUNK:(c + 1) * NCHUNK] = _gelu_x2(
            jnp.dot(hn, w_in_ref[:, w_cols].astype(BF16), preferred_element_type=F32))

    ii = lax.broadcasted_iota(jnp.int32, (GMLP_BLOCK, GMLP_BLOCK), 0) >> CHUNK_SHIFT
    jj = lax.broadcasted_iota(jnp.int32, (GMLP_BLOCK, GMLP_BLOCK), 1) >> CHUNK_SHIFT
    for g in range(A_GROUPS):
        ws_scr[g] = jnp.where(jj <= ii, 0.5 * ws_ref[g], 0.0).astype(BF16)

    u_ahead.append(u_chunk(1))
    for rows in row_blocks:
        v = v_scr[rows, :]
        mu = jnp.mean(v, axis=-1, keepdims=True)
        var = jnp.mean(jnp.square(v - mu), axis=-1, keepdims=True)
        vn_scr[rows, :] = ((v - mu) * lax.rsqrt(var + 4.0 * EPS) * lng_ref[...] + lnb_ref[...]).astype(BF16)

    for c in range(n_half):
        u = u_ahead.pop(0)
        if c + 2 < n_half:
            u_ahead.append(u_chunk(c + 2))
        for gl in range(NCHUNK // A_GROUP_DIM):
            g = c * (NCHUNK // A_GROUP_DIM) + gl
            cols = slice(g * A_GROUP_DIM, (g + 1) * A_GROUP_DIM)
            for rows in row_blocks:
                sv = jnp.dot(ws_scr[g], vn_scr[rows, cols], preferred_element_type=F32) + 0.5 * bs_ref[g]
                u_blk = u[rows, gl * A_GROUP_DIM:(gl + 1) * A_GROUP_DIM]
                gated_scr[rows, cols] = (u_blk * sv).astype(BF16)

    o_ref[...] = h + jnp.dot(gated_scr[...], w_out_ref[...].astype(BF16), preferred_element_type=F32)


def _gmlp(h, g, w_in, lng, lnb, ws, bs, w_out, w1f, w2f):
    steps = TOKENS // TM
    tok = pl.BlockSpec((TM, D_MODEL), lambda t: (t, 0))
    w1_slab = pl.BlockSpec((DEPTH, D_MODEL // steps, D_FF), lambda t: (0, t, 0))
    w2_slab = pl.BlockSpec((DEPTH, D_FF // steps, D_MODEL), lambda t: (0, t, 0))
    return pl.pallas_call(
        _gmlp_kernel,
        grid=(steps,),
        in_specs=[tok, _layer_gain(0), _resident((D_MODEL, 2 * GATE_DIM)),
                  _resident((1, GATE_DIM)), _resident((1, GATE_DIM)),
                  _resident((A_GROUPS, GMLP_BLOCK, GMLP_BLOCK)),
                  _resident((A_GROUPS, GMLP_BLOCK, 1)), _resident((GATE_DIM, D_MODEL)),
                  w1_slab, w2_slab],
        out_specs=[tok, w1_slab, w2_slab],
        out_shape=[jax.ShapeDtypeStruct((TOKENS, D_MODEL), F32),
                   jax.ShapeDtypeStruct((DEPTH, D_MODEL, D_FF), BF16),
                   jax.ShapeDtypeStruct((DEPTH, D_FF, D_MODEL), BF16)],
        scratch_shapes=[pltpu.VMEM((TM, GATE_DIM), F32), pltpu.VMEM((TM, GATE_DIM), BF16),
                        pltpu.VMEM((TM, GATE_DIM), BF16),
                        pltpu.VMEM((A_GROUPS, GMLP_BLOCK, GMLP_BLOCK), BF16)],
        compiler_params=_params("arbitrary"),
        name="gmlp_mixer",
    )(h, g, w_in, lng, lnb, ws, bs, w_out, w1f, w2f)


def _mlp_residual(h, g_ref, w1_ref, w2_ref, a_scr, anchors=None):
    inv_ms = 1.0 / (jnp.mean(h * h, axis=-1, keepdims=True) + EPS)
    hg = (h * g_ref[...]).astype(BF16)
    n_chunks = D_FF // NCHUNK
    slab = h.shape[0] // n_chunks
    for c in range(n_chunks):
        cols = slice(c * NCHUNK, (c + 1) * NCHUNK)
        a = jnp.maximum(jnp.dot(hg, w1_ref[:, cols], preferred_element_type=F32), 0.0)
        if anchors is None:
            a_scr[:, cols] = (a * a).astype(BF16)
        else:
            for r in range(n_chunks):
                rows = slice(r * slab, (r + 1) * slab)
                blk = a[rows] + anchors[c] if r == c else a[rows]
                a_scr[rows, cols] = (blk * blk).astype(BF16)
    return h + inv_ms * jnp.dot(a_scr[...], w2_ref[...], preferred_element_type=F32)


def _mlp_rope_kernel(h_ref, g_ref, w1_ref, w2_ref, pos_ref, rope_ref, o_ref, cos_ref, sin_ref, a_scr):
    half = QK_ROPE // 2
    n_slabs = D_FF // NCHUNK
    slab = TM_MLP // n_slabs
    pos = pos_ref[0].astype(F32)
    zeros = jnp.zeros((half, slab), F32)
    anchors = []
    for c in range(n_slabs):
        toks = slice(c * slab, (c + 1) * slab)
        sin_cos = jnp.sin(rope_ref[:QK_ROPE, :] * pos[:, toks] + rope_ref[QK_ROPE:, :])
        sin_t, cos_t = sin_cos[:half], sin_cos[half:]
        cos_tab = jnp.concatenate([cos_t, cos_t, cos_t, cos_t], axis=0).T
        sin_tab = jnp.concatenate([-sin_t, zeros, sin_t, zeros], axis=0).T
        cos_ref[toks, :] = cos_tab
        sin_ref[toks, :] = sin_tab
        anchors.append((cos_tab[:, 0:1] + sin_tab[:, 0:1]) * 0.0)
    o_ref[...] = _mlp_residual(h_ref[...], g_ref, w1_ref, w2_ref, a_scr, anchors)


def _mlp_final_kernel(h_ref, g_ref, w1_ref, w2_ref, gf_ref, o_ref, a_scr):
    o_ref[...] = _rms(_mlp_residual(h_ref[...], g_ref, w1_ref, w2_ref, a_scr), gf_ref[...])


def _mlp_specs(layer):
    tok = pl.BlockSpec((TM_MLP, D_MODEL), lambda t: (t, 0))
    w1 = pl.BlockSpec((None, D_MODEL, D_FF), lambda t: (layer, 0, 0), pipeline_mode=pl.Buffered(1))
    w2 = pl.BlockSpec((None, D_FF, D_MODEL), lambda t: (layer, 0, 0), pipeline_mode=pl.Buffered(1))
    return tok, [tok, _layer_gain(layer), w1, w2]


def _mlp_rope(h, g, w1, w2, pos, rope_tab, layer):
    tok, in_specs = _mlp_specs(layer)
    tab = pl.BlockSpec((TM_MLP, LANES), lambda t: (t, 0))
    return pl.pallas_call(
        _mlp_rope_kernel,
        grid=(TOKENS // TM_MLP,),
        in_specs=in_specs + [pl.BlockSpec((1, 1, TM_MLP), lambda t: (t, 0, 0)),
                             _resident((2 * QK_ROPE, LANES))],
        out_specs=[tok, tab, tab],
        out_shape=[jax.ShapeDtypeStruct((TOKENS, D_MODEL), F32),
                   jax.ShapeDtypeStruct((TOKENS, LANES), F32),
                   jax.ShapeDtypeStruct((TOKENS, LANES), F32)],
        scratch_shapes=[pltpu.VMEM((TM_MLP, D_FF), BF16)],
        compiler_params=_params("arbitrary"),
        name="mlp_rope",
    )(h, g, w1, w2, pos, rope_tab)


def _mlp_final(h, g, w1, w2, gf, layer):
    tok, in_specs = _mlp_specs(layer)
    return pl.pallas_call(
        _mlp_final_kernel,
        grid=(TOKENS // TM_MLP,),
        in_specs=in_specs + [_resident((1, D_MODEL))],
        out_specs=tok,
        out_shape=jax.ShapeDtypeStruct((TOKENS, D_MODEL), F32),
        scratch_shapes=[pltpu.VMEM((TM_MLP, D_FF), BF16)],
        compiler_params=_params("arbitrary"),
        name="mlp_final",
    )(h, g, w1, w2, gf)


def _rope(x, cos, sin_signed):
    return x * cos + pltpu.roll(x, LANES // 2, 1) * sin_signed


def _proj_kernel(h_ref, cos_ref, sin_ref, gkv_ref, gq_ref, wkva_ref, gkva_ref, wk_ref, wvt_ref,
                 wqa_ref, gqa_ref, wqb_ref, q_ref, kn_ref, kpe_ref, vt_ref):
    h = h_ref[0]
    cos = cos_ref[0]
    sin_signed = sin_ref[0]

    ms = jnp.mean(h * h, axis=-1, keepdims=True) + EPS
    r = lax.rsqrt(ms)
    eps_latent = EPS * ms

    def latent_rms(x, g):
        return x * lax.rsqrt(jnp.mean(x * x, axis=-1, keepdims=True) + eps_latent) * g

    ckv = jnp.dot((h * gkv_ref[...]).astype(BF16), wkva_ref[...], preferred_element_type=F32)
    c_kv = latent_rms(ckv[:, :KV_LORA], gkva_ref[...]).astype(BF16)
    kpe_ref[0] = (_rope(ckv[:, KV_LORA:], cos, sin_signed) * r).astype(BF16)
    kn = jnp.dot(c_kv, wk_ref[...], preferred_element_type=F32).astype(BF16)
    for hd in range(B_HEADS):
        kn_ref[0, hd] = kn[:, hd * QK_NOPE:(hd + 1) * QK_NOPE]
    vt = lax.dot_general(wvt_ref[...], c_kv, NT_DIMS, preferred_element_type=F32).astype(BF16)
    for kb in range(TM_PROJ // TK):
        vt_ref[0, kb] = vt[:, kb * TK:(kb + 1) * TK]

    scale = (QK_NOPE + QK_ROPE) ** -0.5 * LOG2_E
    cq = jnp.dot((h * gq_ref[...]).astype(BF16), wqa_ref[...].astype(BF16), preferred_element_type=F32)
    cq = latent_rms(cq, gqa_ref[...] * scale).astype(BF16)
    for hd in range(B_HEADS):
        q = jnp.dot(cq, wqb_ref[:, hd * QK_PAD:(hd + 1) * QK_PAD], preferred_element_type=F32)
        q_ref[0, hd, :, :QK_NOPE] = q[:, :QK_NOPE].astype(BF16)
        q_ref[0, hd, :, QK_NOPE:] = _rope(q[:, QK_NOPE:], cos, sin_signed).astype(BF16)


def _proj(h, cos_tab, sin_tab, gkv, gq, wkva, gkva, wk, wvt, wqa, gqa, wqb):
    n_t = SEQ // TM_PROJ
    tab = pl.BlockSpec((1, TM_PROJ, LANES), lambda b, t: (b, t, 0))
    return pl.pallas_call(
        _proj_kernel,
        grid=(BATCH, n_t),
        in_specs=[pl.BlockSpec((1, TM_PROJ, D_MODEL), lambda b, t: (b, t, 0)), tab, tab,
                  _resident((1, D_MODEL)), _layer_gain(1),
                  _resident((D_MODEL, KV_LORA + LANES)), _resident((1, KV_LORA)),
                  _resident((KV_LORA, B_HEADS * QK_NOPE)), _resident((B_HEADS * V_HEAD, KV_LORA)),
                  _resident((D_MODEL, Q_LORA)), _resident((1, Q_LORA)),
                  _resident((Q_LORA, B_HEADS * QK_PAD))],
        out_specs=[pl.BlockSpec((1, B_HEADS, TM_PROJ, QK_PAD), lambda b, t: (b, 0, t, 0)),
                   pl.BlockSpec((1, B_HEADS, TM_PROJ, QK_NOPE), lambda b, t: (b, 0, t, 0)),
                   pl.BlockSpec((1, TM_PROJ, LANES), lambda b, t: (b, t, 0)),
                   pl.BlockSpec((1, TM_PROJ // TK, B_HEADS * V_HEAD, TK), lambda b, t: (b, t, 0, 0))],
        out_shape=[jax.ShapeDtypeStruct((BATCH, B_HEADS, SEQ, QK_PAD), BF16),
                   jax.ShapeDtypeStruct((BATCH, B_HEADS, SEQ, QK_NOPE), BF16),
                   jax.ShapeDtypeStruct((BATCH, SEQ, LANES), BF16),
                   jax.ShapeDtypeStruct((BATCH, SEQ // TK, B_HEADS * V_HEAD, TK), BF16)],
        compiler_params=_params("arbitrary", "arbitrary"),
        name="qkv_proj",
    )(h, cos_tab, sin_tab, gkv, gq, wkva, gkva, wk, wvt, wqa, gqa, wqb)


def _attn_kernel(h_ref, q_ref, kn_ref, kpe_ref, vt_ref, wo_ref, o_ref, sa_scr, sb_scr, mta_scr, mtb_scr,
                 m_scr, acc_scr, ot_scr, wo_scr):
    k = pl.program_id(1)
    kc_id = lax.broadcasted_iota(jnp.int32, (TK, TQ), 0) >> CHUNK_SHIFT
    qc_id = lax.broadcasted_iota(jnp.int32, (TK, TQ), 1) >> CHUNK_SHIFT
    diag_ok = kc_id <= qc_id
    ones_rows = jnp.ones((ONES_ROWS, TK), BF16)

    @pl.when(jnp.logical_and(pl.program_id(0) == 0, k == 0))
    def _():
        wo_scr[...] = wo_ref[...].astype(BF16)

    def score_head(hd, sub, j, s_scr, mt_scr, diagonal):
        rows = pl.ds(pl.multiple_of(j * TK, TK), TK)
        k_cat = jnp.concatenate([kn_ref[0, hd, rows, :], kpe_ref[0, rows, :]], axis=-1)
        q_h = q_ref[0, hd, sub * TQ:(sub + 1) * TQ, :]
        s = lax.dot_general(k_cat, q_h, NT_DIMS, preferred_element_type=F32)
        if diagonal:
            s = jnp.where(diag_ok, s, NEG)
        s_scr[hd] = s
        mt_scr[hd] = jnp.max(s, axis=0, keepdims=True)

    def softmax_head(hd, sub, j, s_scr, mt_scr, first):
        slot = sub % 2
        v_aug = jnp.concatenate([vt_ref[0, j, hd * V_HEAD:(hd + 1) * V_HEAD, :], ones_rows], axis=0)
        if first:
            m_new = mt_scr[hd]
            p = jnp.exp2(s_scr[hd] - m_new)
            acc_scr[slot, hd] = jnp.dot(v_aug, p.astype(BF16), preferred_element_type=F32)
        else:
            m = m_scr[slot, hd]
            m_new = jnp.maximum(m, mt_scr[hd])
            alpha = jnp.exp2(m - m_new)
            p = jnp.exp2(s_scr[hd] - m_new)
            acc_scr[slot, hd] = alpha * acc_scr[slot, hd] + jnp.dot(v_aug, p.astype(BF16),
                                                                    preferred_element_type=F32)
        m_scr[slot, hd] = m_new

    def score_stage(*args):
        for hd in range(B_HEADS):
            score_head(hd, *args)

    def softmax_stage(*args):
        for hd in range(B_HEADS):
            softmax_head(hd, *args)

    def paired_stage(score_args, softmax_args):
        score_head(0, *score_args)
        for hd in range(B_HEADS):
            if hd + 1 < B_HEADS:
                score_head(hd + 1, *score_args)
            softmax_head(hd, *softmax_args)

    def first_tiles(sub, i):
        score_stage(sub, i, sb_scr, mtb_scr, True)
        paired_stage((sub, 0, sa_scr, mta_scr, False), (sub, i, sb_scr, mtb_scr, True))

    def tile_pairs(sub, n_pairs):
        def tile_pair(u, _):
            j = 2 * u
            paired_stage((sub, j + 1, sb_scr, mtb_scr, False), (sub, j, sa_scr, mta_scr, False))
            paired_stage((sub, j + 2, sa_scr, mta_scr, False), (sub, j + 1, sb_scr, mtb_scr, False))
            return 0

        lax.fori_loop(0, n_pairs, tile_pair, 0)

    def last_tiles_even(sub, i):
        paired_stage((sub, i - 1, sb_scr, mtb_scr, False), (sub, i - 2, sa_scr, mta_scr, False))
        softmax_stage(sub, i - 1, sb_scr, mtb_scr, False)

    def project(sub):
        slot = sub % 2
        q_rows = slice(sub * TQ, (sub + 1) * TQ)
        for hd in range(B_HEADS):
            inv_l = 1.0 / acc_scr[slot, hd, V_HEAD:V_HEAD + 1, :]
            ot_scr[slot, hd * V_HEAD:(hd + 1) * V_HEAD, :] = (acc_scr[slot, hd, :V_HEAD, :] * inv_l).astype(BF16)
        out = lax.dot_general(ot_scr[slot], wo_scr[...], TN_DIMS, preferred_element_type=F32)
        o_ref[0, q_rows, :] = h_ref[0, q_rows, :] + out

    for sub in range(Q_TILES_PER_STEP):
        i = Q_TILES_PER_STEP * k + sub
        if sub > 0:
            project(sub - 1)
        first_tiles(sub, i)
        tile_pairs(sub, jnp.maximum(i - 1, 0) >> 1)
        if sub % 2 == 1:
            softmax_stage(sub, i - 1, sa_scr, mta_scr, False)
        elif sub == 0:
            pl.when(k > 0)(functools.partial(last_tiles_even, sub, i))
        else:
            last_tiles_even(sub, i)
    project(Q_TILES_PER_STEP - 1)


def _attn(h, q, kn, kpe, vt, wo):
    tq2 = Q_TILES_PER_STEP * TQ
    return pl.pallas_call(
        _attn_kernel,
        grid=(BATCH, SEQ // tq2),
        in_specs=[pl.BlockSpec((1, tq2, D_MODEL), lambda b, k: (b, k, 0)),
                  pl.BlockSpec((1, B_HEADS, tq2, QK_PAD), lambda b, k: (b, 0, k, 0)),
                  pl.BlockSpec((1, B_HEADS, SEQ, QK_NOPE), lambda b, k: (b, 0, 0, 0)),
                  pl.BlockSpec((1, SEQ, LANES), lambda b, k: (b, 0, 0)),
                  pl.BlockSpec((1, SEQ // TK, B_HEADS * V_HEAD, TK), lambda b, k: (b, 0, 0, 0)),
                  _resident((B_HEADS * V_HEAD, D_MODEL))],
        out_specs=pl.BlockSpec((1, tq2, D_MODEL), lambda b, k: (b, k, 0)),
        out_shape=jax.ShapeDtypeStruct((BATCH, SEQ, D_MODEL), F32),
        scratch_shapes=[pltpu.VMEM((B_HEADS, TK, TQ), F32), pltpu.VMEM((B_HEADS, TK, TQ), F32),
                        pltpu.VMEM((B_HEADS, 1, TQ), F32), pltpu.VMEM((B_HEADS, 1, TQ), F32),
                        pltpu.VMEM((2, B_HEADS, 1, TQ), F32),
                        pltpu.VMEM((2, B_HEADS, V_HEAD + ONES_ROWS, TQ), F32),
                        pltpu.VMEM((2, B_HEADS * V_HEAD, TQ), BF16),
                        pltpu.VMEM((B_HEADS * V_HEAD, D_MODEL), BF16)],
        compiler_params=_params("arbitrary", "arbitrary"),
        name="mla_attention",
    )(h, q, kn, kpe, vt, wo)


def _row(v):
    return v.reshape(1, -1)


def _spread_rope(w):
    half = QK_ROPE // 2
    zeros = jnp.zeros(w.shape[:-1] + (half,), w.dtype)
    return jnp.concatenate([w[..., :half], zeros, w[..., half:], zeros], axis=-1)


def kernel(x, positions, norm_mix_g, norm_mlp_g, a_w_in, a_ln_v_g, a_ln_v_b, a_w_s, a_b_s, a_w_out,
           b_w_q_a, b_q_norm_g, b_w_q_b, b_w_o, kv_src_norm_g, kv_w_a, kv_a_norm_g, kv_w_b,
           mlp_w1, mlp_w2, final_norm_g):
    w_in = a_w_in[0]
    w_out = a_w_out[0]
    wkva = jnp.concatenate([kv_w_a[:, :KV_LORA], _spread_rope(kv_w_a[:, KV_LORA:])], axis=-1).astype(BF16)
    kvb = kv_w_b.reshape(KV_LORA, B_HEADS, QK_NOPE + V_HEAD)
    wk = kvb[:, :, :QK_NOPE].reshape(KV_LORA, B_HEADS * QK_NOPE).astype(BF16)
    wvt = kvb[:, :, QK_NOPE:].reshape(KV_LORA, B_HEADS * V_HEAD).T.astype(BF16)
    wqa = b_w_q_a[0]
    wqb = b_w_q_b[0].reshape(Q_LORA, B_HEADS, QK_NOPE + QK_ROPE)
    wqb = jnp.concatenate([wqb[:, :, :QK_NOPE], _spread_rope(wqb[:, :, QK_NOPE:])], axis=-1)
    wqb = wqb.reshape(Q_LORA, B_HEADS * QK_PAD).astype(BF16)
    wo = b_w_o[0]

    inv_freq = ROPE_THETA ** (-jnp.arange(0, QK_ROPE, 2, dtype=F32) / QK_ROPE)
    phase = jnp.concatenate([jnp.zeros((QK_ROPE // 2,), F32), jnp.full((QK_ROPE // 2,), HALF_PI, F32)])
    rope_tab = jnp.broadcast_to(jnp.concatenate([inv_freq, inv_freq, phase])[:, None], (2 * QK_ROPE, LANES))

    mix_g = norm_mix_g.reshape(DEPTH, 1, D_MODEL)
    mlp_g = norm_mlp_g.reshape(DEPTH, 1, D_MODEL)
    h = x.reshape(TOKENS, D_MODEL)
    h, w1, w2 = _gmlp(h, mix_g, w_in, _row(a_ln_v_g[0]), _row(a_ln_v_b[0]), a_w_s[0],
                      a_b_s[0].reshape(A_GROUPS, GMLP_BLOCK, 1), w_out, mlp_w1, mlp_w2)
    h, cos_tab, sin_tab = _mlp_rope(h, mlp_g, w1, w2,
                                    positions.reshape(TOKENS // TM_MLP, 1, TM_MLP),
                                    rope_tab, 0)

    h3 = h.reshape(BATCH, SEQ, D_MODEL)
    q, kn, kpe, vt = _proj(h3, cos_tab.reshape(BATCH, SEQ, LANES), sin_tab.reshape(BATCH, SEQ, LANES),
                           _row(kv_src_norm_g), mix_g, wkva, _row(kv_a_norm_g), wk, wvt,
                           wqa, _row(b_q_norm_g[0]), wqb)
    h3 = _attn(h3, q, kn, kpe, vt, wo)

    h = _mlp_final(h3.reshape(TOKENS, D_MODEL), mlp_g, w1, w2, _row(final_norm_g), 1)
    return h.reshape(BATCH, SEQ, D_MODEL)
```

```python
import functools

import jax
import jax.numpy as jnp
import numpy as np
from jax import lax
from jax.experimental import pallas as pl
from jax.experimental.pallas import tpu as pltpu

D_MODEL = 1024
BATCH = 8
SEQ = 2048
TOKENS = BATCH * SEQ
DEPTH = 2
CHUNK = 64
CHUNK_SHIFT = CHUNK.bit_length() - 1
GMLP_BLOCK = 128
GATE_DIM = 2 * D_MODEL
A_GROUPS = 8
A_GROUP_DIM = GATE_DIM // A_GROUPS
B_HEADS = 8
QK_NOPE = 128
QK_ROPE = 64
V_HEAD = 128
Q_LORA = 384
KV_LORA = 256
ROPE_THETA = 10000.0
D_FF = 4 * D_MODEL
EPS = 1e-6

LANES = 128
QK_PAD = 2 * LANES
VMEM_LIMIT = 58 * 1024 * 1024

TM = 512
TM_PROJ = 1024
TM_MLP = 1024
NCHUNK = 512
TQ = 256
TK = 256
Q_TILES_PER_STEP = 4
ONES_ROWS = 16

assert Q_TILES_PER_STEP % 2 == 0 and SEQ % (Q_TILES_PER_STEP * TQ) == 0 and TQ == TK
assert TM_MLP // (D_FF // NCHUNK) == LANES

F32 = jnp.float32
BF16 = jnp.bfloat16
NEG = float(np.finfo(np.float32).min)
SQRT_HALF = float(np.sqrt(0.5))
LOG2_E = float(np.log2(np.e))
HALF_PI = float(np.pi / 2)
NT_DIMS = (((1,), (1,)), ((), ()))
TN_DIMS = (((0,), (0,)), ((), ()))


def _rms(x, g):
    ms = jnp.mean(x * x, axis=-1, keepdims=True)
    return x * lax.rsqrt(ms + EPS) * g


def _resident(shape):
    nd = len(shape)
    return pl.BlockSpec(shape, lambda *_: (0,) * nd, pipeline_mode=pl.Buffered(1))


def _layer_gain(layer):
    return pl.BlockSpec((None, 1, D_MODEL), lambda *_: (layer, 0, 0), pipeline_mode=pl.Buffered(1))


def _params(*sem):
    return pltpu.CompilerParams(dimension_semantics=sem, vmem_limit_bytes=VMEM_LIMIT)


def _gelu_x2(z):
    return z * (1.0 + lax.erf(z * SQRT_HALF))


def _gmlp_kernel(h_ref, g_ref, w_in_ref, lng_ref, lnb_ref, ws_ref, bs_ref, w_out_ref, w1f_ref, w2f_ref,
                 o_ref, w1b_ref, w2b_ref, v_scr, vn_scr, gated_scr, ws_scr):
    w1b_ref[...] = w1f_ref[...].astype(BF16)
    w2b_ref[...] = w2f_ref[...].astype(BF16)

    h = h_ref[...]
    hn = _rms(h, g_ref[...]).astype(BF16)
    n_half = GATE_DIM // NCHUNK
    row_blocks = [slice(r * GMLP_BLOCK, (r + 1) * GMLP_BLOCK) for r in range(TM // GMLP_BLOCK)]

    def u_chunk(c):
        return _gelu_x2(jnp.dot(hn, w_in_ref[:, c * NCHUNK:(c + 1) * NCHUNK].astype(BF16),
                                preferred_element_type=F32))

    u_ahead = []
    for c in range(n_half):
        if c == n_half - 2:
            u_ahead.append(u_chunk(0))
        w_cols = slice(GATE_DIM + c * NCHUNK, GATE_DIM + (c + 1) * NCHUNK)
        v_scr[:, c * NCHUNK:(c + 1) * NCHUNK] = _gelu_x2(
            jnp.dot(hn, w_in_ref[:, w_cols].astype(BF16), preferred_element_type=F32))

    ii = lax.broadcasted_iota(jnp.int32, (GMLP_BLOCK, GMLP_BLOCK), 0) >> CHUNK_SHIFT
    jj = lax.broadcasted_iota(jnp.int32, (GMLP_BLOCK, GMLP_BLOCK), 1) >> CHUNK_SHIFT
    for g in range(A_GROUPS):
        ws_scr[g] = jnp.where(jj <= ii, 0.5 * ws_ref[g], 0.0).astype(BF16)

    u_ahead.append(u_chunk(1))
    for rows in row_blocks:
        v = v_scr[rows, :]
        mu = jnp.mean(v, axis=-1, keepdims=True)
        var = jnp.mean(jnp.square(v - mu), axis=-1, keepdims=True)
        vn_scr[rows, :] = ((v - mu) * lax.rsqrt(var + 4.0 * EPS) * lng_ref[...] + lnb_ref[...]).astype(BF16)

    for c in range(n_half):
        u = u_ahead.pop(0)
        if c + 2 < n_half:
            u_ahead.append(u_chunk(c + 2))
        for gl in range(NCHUNK // A_GROUP_DIM):
            g = c * (NCHUNK // A_GROUP_DIM) + gl
            cols = slice(g * A_GROUP_DIM, (g + 1) * A_GROUP_DIM)
            for rows in row_blocks:
                sv = jnp.dot(ws_scr[g], vn_scr[rows, cols], preferred_element_type=F32) + 0.5 * bs_ref[g]
                u_blk = u[rows, gl * A_GROUP_DIM:(gl + 1) * A_GROUP_DIM]
                gated_scr[rows, cols] = (u_blk * sv).astype(BF16)

    o_ref[...] = h + jnp.dot(gated_scr[...], w_out_ref[...].astype(BF16), preferred_element_type=F32)


def _gmlp(h, g, w_in, lng, lnb, ws, bs, w_out, w1f, w2f):
    steps = TOKENS // TM
    tok = pl.BlockSpec((TM, D_MODEL), lambda t: (t, 0))
    w1_slab = pl.BlockSpec((DEPTH, D_MODEL // steps, D_FF), lambda t: (0, t, 0))
    w2_slab = pl.BlockSpec((DEPTH, D_FF // steps, D_MODEL), lambda t: (0, t, 0))
    return pl.pallas_call(
        _gmlp_kernel,
        grid=(steps,),
        in_specs=[tok, _layer_gain(0), _resident((D_MODEL, 2 * GATE_DIM)),
                  _resident((1, GATE_DIM)), _resident((1, GATE_DIM)),
                  _resident((A_GROUPS, GMLP_BLOCK, GMLP_BLOCK)),
                  _resident((A_GROUPS, GMLP_BLOCK, 1)), _resident((GATE_DIM, D_MODEL)),
                  w1_slab, w2_slab],
        out_specs=[tok, w1_slab, w2_slab],
        out_shape=[jax.ShapeDtypeStruct((TOKENS, D_MODEL), F32),
                   jax.ShapeDtypeStruct((DEPTH, D_MODEL, D_FF), BF16),
                   jax.ShapeDtypeStruct((DEPTH, D_FF, D_MODEL), BF16)],
        scratch_shapes=[pltpu.VMEM((TM, GATE_DIM), F32), pltpu.VMEM((TM, GATE_DIM), BF16),
                        pltpu.VMEM((TM, GATE_DIM), BF16),
                        pltpu.VMEM((A_GROUPS, GMLP_BLOCK, GMLP_BLOCK), BF16)],
        compiler_params=_params("arbitrary"),
        name="gmlp_mixer",
    )(h, g, w_in, lng, lnb, ws, bs, w_out, w1f, w2f)


def _mlp_residual(h, g_ref, w1_ref, w2_ref, a_scr, anchors=None):
    inv_ms = 1.0 / (jnp.mean(h * h, axis=-1, keepdims=True) + EPS)
    hg = (h * g_ref[...]).astype(BF16)
    n_chunks = D_FF // NCHUNK
    slab = h.shape[0] // n_chunks
    for c in range(n_chunks):
        cols = slice(c * NCHUNK, (c + 1) * NCHUNK)
        a = jnp.maximum(jnp.dot(hg, w1_ref[:, cols], preferred_element_type=F32), 0.0)
        if anchors is None:
            a_scr[:, cols] = (a * a).astype(BF16)
        else:
            for r in range(n_chunks):
                rows = slice(r * slab, (r + 1) * slab)
                blk = a[rows] + anchors[r] if min(r + 1, n_chunks - 1) == c else a[rows]
                a_scr[rows, cols] = (blk * blk).astype(BF16)
    return h + inv_ms * jnp.dot(a_scr[...], w2_ref[...], preferred_element_type=F32)


def _mlp_rope_kernel(h_ref, g_ref, w1_ref, w2_ref, pos_ref, rope_ref, o_ref, cos_ref, sin_ref, a_scr):
    half = QK_ROPE // 2
    n_slabs = D_FF // NCHUNK
    slab = TM_MLP // n_slabs
    pos = pos_ref[0].astype(F32)
    zeros = jnp.zeros((half, slab), F32)
    anchors = []
    for c in range(n_slabs):
        toks = slice(c * slab, (c + 1) * slab)
        sin_cos = jnp.sin(rope_ref[:QK_ROPE, :] * pos[:, toks] + rope_ref[QK_ROPE:, :])
        sin_t, cos_t = sin_cos[:half], sin_cos[half:]
        cos_tab = jnp.concatenate([cos_t, cos_t, cos_t, cos_t], axis=0).T
        sin_tab = jnp.concatenate([-sin_t, zeros, sin_t, zeros], axis=0).T
        cos_ref[toks, :] = cos_tab
        sin_ref[toks, :] = sin_tab
        anchors.append((cos_tab[:, 0:1] + sin_tab[:, 0:1]) * 0.0)
    o_ref[...] = _mlp_residual(h_ref[...], g_ref, w1_ref, w2_ref, a_scr, anchors)


def _mlp_final_kernel(h_ref, g_ref, w1_ref, w2_ref, gf_ref, o_ref, a_scr):
    o_ref[...] = _rms(_mlp_residual(h_ref[...], g_ref, w1_ref, w2_ref, a_scr), gf_ref[...])


def _mlp_specs(layer):
    tok = pl.BlockSpec((TM_MLP, D_MODEL), lambda t: (t, 0))
    w1 = pl.BlockSpec((None, D_MODEL, D_FF), lambda t: (layer, 0, 0), pipeline_mode=pl.Buffered(1))
    w2 = pl.BlockSpec((None, D_FF, D_MODEL), lambda t: (layer, 0, 0), pipeline_mode=pl.Buffered(1))
    return tok, [tok, _layer_gain(layer), w1, w2]


def _mlp_rope(h, g, w1, w2, pos, rope_tab, layer):
    tok, in_specs = _mlp_specs(layer)
    tab = pl.BlockSpec((TM_MLP, LANES), lambda t: (t, 0))
    return pl.pallas_call(
        _mlp_rope_kernel,
        grid=(TOKENS // TM_MLP,),
        in_specs=in_specs + [pl.BlockSpec((1, 1, TM_MLP), lambda t: (t, 0, 0)),
                             _resident((2 * QK_ROPE, LANES))],
        out_specs=[tok, tab, tab],
        out_shape=[jax.ShapeDtypeStruct((TOKENS, D_MODEL), F32),
                   jax.ShapeDtypeStruct((TOKENS, LANES), F32),
                   jax.ShapeDtypeStruct((TOKENS, LANES), F32)],
        scratch_shapes=[pltpu.VMEM((TM_MLP, D_FF), BF16)],
        compiler_params=_params("arbitrary"),
        name="mlp_rope",
    )(h, g, w1, w2, pos, rope_tab)


def _mlp_final(h, g, w1, w2, gf, layer):
    tok, in_specs = _mlp_specs(layer)
    return pl.pallas_call(
        _mlp_final_kernel,
        grid=(TOKENS // TM_MLP,),
        in_specs=in_specs + [_resident((1, D_MODEL))],
        out_specs=tok,
        out_shape=jax.ShapeDtypeStruct((TOKENS, D_MODEL), F32),
        scratch_shapes=[pltpu.VMEM((TM_MLP, D_FF), BF16)],
        compiler_params=_params("arbitrary"),
        name="mlp_final",
    )(h, g, w1, w2, gf)


def _rope(x, cos, sin_signed):
    return x * cos + pltpu.roll(x, LANES // 2, 1) * sin_signed


def _proj_kernel(h_ref, cos_ref, sin_ref, gkv_ref, gq_ref, wkva_ref, gkva_ref, wk_ref, wvt_ref,
                 wqa_ref, gqa_ref, wqb_ref, q_ref, kn_ref, kpe_ref, vt_ref):
    h = h_ref[0]
    cos = cos_ref[0]
    sin_signed = sin_ref[0]

    ms = jnp.mean(h * h, axis=-1, keepdims=True) + EPS
    r = lax.rsqrt(ms)
    eps_latent = EPS * ms

    def latent_rms(x, g):
        return x * lax.rsqrt(jnp.mean(x * x, axis=-1, keepdims=True) + eps_latent) * g

    ckv = jnp.dot((h * gkv_ref[...]).astype(BF16), wkva_ref[...], preferred_element_type=F32)
    c_kv = latent_rms(ckv[:, :KV_LORA], gkva_ref[...]).astype(BF16)
    kpe_ref[0] = (_rope(ckv[:, KV_LORA:], cos, sin_signed) * r).astype(BF16)
    kn = jnp.dot(c_kv, wk_ref[...], preferred_element_type=F32).astype(BF16)
    for hd in range(B_HEADS):
        kn_ref[0, hd] = kn[:, hd * QK_NOPE:(hd + 1) * QK_NOPE]
    vt = lax.dot_general(wvt_ref[...], c_kv, NT_DIMS, preferred_element_type=F32).astype(BF16)
    for kb in range(TM_PROJ // TK):
        vt_ref[0, kb] = vt[:, kb * TK:(kb + 1) * TK]

    scale = (QK_NOPE + QK_ROPE) ** -0.5 * LOG2_E
    cq = jnp.dot((h * gq_ref[...]).astype(BF16), wqa_ref[...].astype(BF16), preferred_element_type=F32)
    cq = latent_rms(cq, gqa_ref[...] * scale).astype(BF16)
    for hd in range(B_HEADS):
        q = jnp.dot(cq, wqb_ref[:, hd * QK_PAD:(hd + 1) * QK_PAD], preferred_element_type=F32)
        q_ref[0, hd, :, :QK_NOPE] = q[:, :QK_NOPE].astype(BF16)
        q_ref[0, hd, :, QK_NOPE:] = _rope(q[:, QK_NOPE:], cos, sin_signed).astype(BF16)


def _proj(h, cos_tab, sin_tab, gkv, gq, wkva, gkva, wk, wvt, wqa, gqa, wqb):
    n_t = SEQ // TM_PROJ
    tab = pl.BlockSpec((1, TM_PROJ, LANES), lambda b, t: (b, t, 0))
    return pl.pallas_call(
        _proj_kernel,
        grid=(BATCH, n_t),
        in_specs=[pl.BlockSpec((1, TM_PROJ, D_MODEL), lambda b, t: (b, t, 0)), tab, tab,
                  _resident((1, D_MODEL)), _layer_gain(1),
                  _resident((D_MODEL, KV_LORA + LANES)), _resident((1, KV_LORA)),
                  _resident((KV_LORA, B_HEADS * QK_NOPE)), _resident((B_HEADS * V_HEAD, KV_LORA)),
                  _resident((D_MODEL, Q_LORA)), _resident((1, Q_LORA)),
                  _resident((Q_LORA, B_HEADS * QK_PAD))],
        out_specs=[pl.BlockSpec((1, B_HEADS, TM_PROJ, QK_PAD), lambda b, t: (b, 0, t, 0)),
                   pl.BlockSpec((1, B_HEADS, TM_PROJ, QK_NOPE), lambda b, t: (b, 0, t, 0)),
                   pl.BlockSpec((1, TM_PROJ, LANES), lambda b, t: (b, t, 0)),
                   pl.BlockSpec((1, TM_PROJ // TK, B_HEADS * V_HEAD, TK), lambda b, t: (b, t, 0, 0))],
        out_shape=[jax.ShapeDtypeStruct((BATCH, B_HEADS, SEQ, QK_PAD), BF16),
                   jax.ShapeDtypeStruct((BATCH, B_HEADS, SEQ, QK_NOPE), BF16),
                   jax.ShapeDtypeStruct((BATCH, SEQ, LANES), BF16),
                   jax.ShapeDtypeStruct((BATCH, SEQ // TK, B_HEADS * V_HEAD, TK), BF16)],
        compiler_params=_params("arbitrary", "arbitrary"),
        name="qkv_proj",
    )(h, cos_tab, sin_tab, gkv, gq, wkva, gkva, wk, wvt, wqa, gqa, wqb)


def _attn_kernel(h_ref, q_ref, kn_ref, kpe_ref, vt_ref, wo_ref, o_ref, sa_scr, sb_scr, mta_scr, mtb_scr,
                 m_scr, acc_scr, ot_scr, wo_scr):
    k = pl.program_id(1)
    kc_id = lax.broadcasted_iota(jnp.int32, (TK, TQ), 0) >> CHUNK_SHIFT
    qc_id = lax.broadcasted_iota(jnp.int32, (TK, TQ), 1) >> CHUNK_SHIFT
    diag_ok = kc_id <= qc_id
    ones_rows = jnp.ones((ONES_ROWS, TK), BF16)

    @pl.when(jnp.logical_and(pl.program_id(0) == 0, k == 0))
    def _():
        wo_scr[...] = wo_ref[...].astype(BF16)

    def score_head(hd, sub, j, s_scr, mt_scr, diagonal):
        rows = pl.ds(pl.multiple_of(j * TK, TK), TK)
        k_cat = jnp.concatenate([kn_ref[0, hd, rows, :], kpe_ref[0, rows, :]], axis=-1)
        q_h = q_ref[0, hd, sub * TQ:(sub + 1) * TQ, :]
        s = lax.dot_general(k_cat, q_h, NT_DIMS, preferred_element_type=F32)
        if diagonal:
            s = jnp.where(diag_ok, s, NEG)
        s_scr[hd] = s
        mt_scr[hd] = jnp.max(s, axis=0, keepdims=True)

    def softmax_head(hd, sub, j, s_scr, mt_scr, first):
        slot = sub % 2
        v_aug = jnp.concatenate([vt_ref[0, j, hd * V_HEAD:(hd + 1) * V_HEAD, :], ones_rows], axis=0)
        if first:
            m_new = mt_scr[hd]
            p = jnp.exp2(s_scr[hd] - m_new)
            acc_scr[slot, hd] = jnp.dot(v_aug, p.astype(BF16), preferred_element_type=F32)
        else:
            m = m_scr[slot, hd]
            m_new = jnp.maximum(m, mt_scr[hd])
            alpha = jnp.exp2(m - m_new)
            p = jnp.exp2(s_scr[hd] - m_new)
            acc_scr[slot, hd] = alpha * acc_scr[slot, hd] + jnp.dot(v_aug, p.astype(BF16),
                                                                    preferred_element_type=F32)
        m_scr[slot, hd] = m_new

    def score_stage(*args):
        for hd in range(B_HEADS):
            score_head(hd, *args)

    def softmax_stage(*args):
        for hd in range(B_HEADS):
            softmax_head(hd, *args)

    def paired_stage(score_args, softmax_args):
        score_head(0, *score_args)
        for hd in range(B_HEADS):
            if hd + 1 < B_HEADS:
                score_head(hd + 1, *score_args)
            softmax_head(hd, *softmax_args)

    def first_tiles(sub, i):
        score_stage(sub, i, sb_scr, mtb_scr, True)
        paired_stage((sub, 0, sa_scr, mta_scr, False), (sub, i, sb_scr, mtb_scr, True))

    def tile_pairs(sub, n_pairs):
        def tile_pair(u, _):
            j = 2 * u
            paired_stage((sub, j + 1, sb_scr, mtb_scr, False), (sub, j, sa_scr, mta_scr, False))
            paired_stage((sub, j + 2, sa_scr, mta_scr, False), (sub, j + 1, sb_scr, mtb_scr, False))
            return 0

        lax.fori_loop(0, n_pairs, tile_pair, 0)

    def last_tiles_even(sub, i):
        paired_stage((sub, i - 1, sb_scr, mtb_scr, False), (sub, i - 2, sa_scr, mta_scr, False))
        softmax_stage(sub, i - 1, sb_scr, mtb_scr, False)

    def project(sub):
        slot = sub % 2
        q_rows = slice(sub * TQ, (sub + 1) * TQ)
        for hd in range(B_HEADS):
            inv_l = 1.0 / acc_scr[slot, hd, V_HEAD:V_HEAD + 1, :]
            ot_scr[slot, hd * V_HEAD:(hd + 1) * V_HEAD, :] = (acc_scr[slot, hd, :V_HEAD, :] * inv_l).astype(BF16)
        out = lax.dot_general(ot_scr[slot], wo_scr[...], TN_DIMS, preferred_element_type=F32)
        o_ref[0, q_rows, :] = h_ref[0, q_rows, :] + out

    for sub in range(Q_TILES_PER_STEP):
        i = Q_TILES_PER_STEP * k + sub
        if sub > 0:
            project(sub - 1)
        first_tiles(sub, i)
        tile_pairs(sub, jnp.maximum(i - 1, 0) >> 1)
        if sub % 2 == 1:
            softmax_stage(sub, i - 1, sa_scr, mta_scr, False)
        elif sub == 0:
            pl.when(k > 0)(functools.partial(last_tiles_even, sub, i))
        else:
            last_tiles_even(sub, i)
    project(Q_TILES_PER_STEP - 1)


def _attn(h, q, kn, kpe, vt, wo):
    tq2 = Q_TILES_PER_STEP * TQ
    return pl.pallas_call(
        _attn_kernel,
        grid=(BATCH, SEQ // tq2),
        in_specs=[pl.BlockSpec((1, tq2, D_MODEL), lambda b, k: (b, k, 0)),
                  pl.BlockSpec((1, B_HEADS, tq2, QK_PAD), lambda b, k: (b, 0, k, 0)),
                  pl.BlockSpec((1, B_HEADS, SEQ, QK_NOPE), lambda b, k: (b, 0, 0, 0)),
                  pl.BlockSpec((1, SEQ, LANES), lambda b, k: (b, 0, 0)),
                  pl.BlockSpec((1, SEQ // TK, B_HEADS * V_HEAD, TK), lambda b, k: (b, 0, 0, 0)),
                  _resident((B_HEADS * V_HEAD, D_MODEL))],
        out_specs=pl.BlockSpec((1, tq2, D_MODEL), lambda b, k: (b, k, 0)),
        out_shape=jax.ShapeDtypeStruct((BATCH, SEQ, D_MODEL), F32),
        scratch_shapes=[pltpu.VMEM((B_HEADS, TK, TQ), F32), pltpu.VMEM((B_HEADS, TK, TQ), F32),
                        pltpu.VMEM((B_HEADS, 1, TQ), F32), pltpu.VMEM((B_HEADS, 1, TQ), F32),
                        pltpu.VMEM((2, B_HEADS, 1, TQ), F32),
                        pltpu.VMEM((2, B_HEADS, V_HEAD + ONES_ROWS, TQ), F32),
                        pltpu.VMEM((2, B_HEADS * V_HEAD, TQ), BF16),
                        pltpu.VMEM((B_HEADS * V_HEAD, D_MODEL), BF16)],
        compiler_params=_params("arbitrary", "arbitrary"),
        name="mla_attention",
    )(h, q, kn, kpe, vt, wo)


def _row(v):
    return v.reshape(1, -1)


def _spread_rope(w):
    half = QK_ROPE // 2
    zeros = jnp.zeros(w.shape[:-1] + (half,), w.dtype)
    return jnp.concatenate([w[..., :half], zeros, w[..., half:], zeros], axis=-1)


def kernel(x, positions, norm_mix_g, norm_mlp_g, a_w_in, a_ln_v_g, a_ln_v_b, a_w_s, a_b_s, a_w_out,
           b_w_q_a, b_q_norm_g, b_w_q_b, b_w_o, kv_src_norm_g, kv_w_a, kv_a_norm_g, kv_w_b,
           mlp_w1, mlp_w2, final_norm_g):
    w_in = a_w_in[0]
    w_out = a_w_out[0]
    wkva = jnp.concatenate([kv_w_a[:, :KV_LORA], _spread_rope(kv_w_a[:, KV_LORA:])], axis=-1).astype(BF16)
    kvb = kv_w_b.reshape(KV_LORA, B_HEADS, QK_NOPE + V_HEAD)
    wk = kvb[:, :, :QK_NOPE].reshape(KV_LORA, B_HEADS * QK_NOPE).astype(BF16)
    wvt = kvb[:, :, QK_NOPE:].reshape(KV_LORA, B_HEADS * V_HEAD).T.astype(BF16)
    wqa = b_w_q_a[0]
    wqb = b_w_q_b[0].reshape(Q_LORA, B_HEADS, QK_NOPE + QK_ROPE)
    wqb = jnp.concatenate([wqb[:, :, :QK_NOPE], _spread_rope(wqb[:, :, QK_NOPE:])], axis=-1)
    wqb = wqb.reshape(Q_LORA, B_HEADS * QK_PAD).astype(BF16)
    wo = b_w_o[0]

    inv_freq = ROPE_THETA ** (-jnp.arange(0, QK_ROPE, 2, dtype=F32) / QK_ROPE)
    phase = jnp.concatenate([jnp.zeros((QK_ROPE // 2,), F32), jnp.full((QK_ROPE // 2,), HALF_PI, F32)])
    rope_tab = jnp.broadcast_to(jnp.concatenate([inv_freq, inv_freq, phase])[:, None], (2 * QK_ROPE, LANES))

    mix_g = norm_mix_g.reshape(DEPTH, 1, D_MODEL)
    mlp_g = norm_mlp_g.reshape(DEPTH, 1, D_MODEL)
    h = x.reshape(TOKENS, D_MODEL)
    h, w1, w2 = _gmlp(h, mix_g, w_in, _row(a_ln_v_g[0]), _row(a_ln_v_b[0]), a_w_s[0],
                      a_b_s[0].reshape(A_GROUPS, GMLP_BLOCK, 1), w_out, mlp_w1, mlp_w2)
    h, cos_tab, sin_tab = _mlp_rope(h, mlp_g, w1, w2,
                                    positions.reshape(TOKENS // TM_MLP, 1, TM_MLP),
                                    rope_tab, 0)

    h3 = h.reshape(BATCH, SEQ, D_MODEL)
    q, kn, kpe, vt = _proj(h3, cos_tab.reshape(BATCH, SEQ, LANES), sin_tab.reshape(BATCH, SEQ, LANES),
                           _row(kv_src_norm_g), mix_g, wkva, _row(kv_a_norm_g), wk, wvt,
                           wqa, _row(b_q_norm_g[0]), wqb)
    h3 = _attn(h3, q, kn, kpe, vt, wo)

    h = _mlp_final(h3.reshape(TOKENS, D_MODEL), mlp_g, w1, w2, _row(final_norm_g), 1)
    return h.reshape(BATCH, SEQ, D_MODEL)
```
